```python
import math
import jax
import jax.numpy as jnp
from jax import lax
import numpy as np

D_MODEL = 1024
BATCH = 8
SEQ = 2048
DEPTH = 4
DEC_BATCH = 32
DEC_SEQ = 1
PAST_LEN = 8192
PAGE_SIZE = 128

HEAD_DIM = 64
ROT_DIM = HEAD_DIM // 4
ROPE_THETA = 500000.0
ATTN_SCALE = HEAD_DIM ** -0.5
NORM_EPS = 1e-6
NEG_INF = -1e30
N_EVEN = (DEPTH + 1) // 2
N_ODD = DEPTH // 2

SSM_HEADS = 16
SSM_HEAD_DIM = 64
SSM_D_INNER = SSM_HEADS * SSM_HEAD_DIM
SSM_GROUPS = 2
SSM_STATE = 128
SSM_CONV = 4
SSM_CONV_DIM = SSM_D_INNER + 2 * SSM_GROUPS * SSM_STATE
SSD_CHUNK = 128

NSA_HEADS = 16
NSA_KV_HEADS = 4
NSA_GQ = NSA_HEADS // NSA_KV_HEADS
NSA_CMP_BLOCK = 32
NSA_CMP_STRIDE = 16
NSA_CMP_HIDDEN = 64
NSA_SEL_BLOCK = 64
NSA_TOPN = 8
NSA_LOCAL_BLOCKS = 2
NSA_FORCE = 1e6
NSA_WINDOW = 512

MOBA_HEADS = 16
MOBA_KV_HEADS = 4
MOBA_GQ = MOBA_HEADS // MOBA_KV_HEADS
MOBA_BLOCK = 256
MOBA_TOPK = 3

PEER_HEADS = 8
PEER_N_KEYS = 128
PEER_N_EXPERTS = PEER_N_KEYS * PEER_N_KEYS
PEER_KEY_DIM = 256
PEER_TOPK = 16

EVEN_WIDTHS = (SSM_D_INNER, SSM_CONV_DIM, SSM_HEADS, NSA_HEADS * HEAD_DIM,
               2 * NSA_KV_HEADS * HEAD_DIM, 2 * NSA_KV_HEADS * HEAD_DIM, 2 * NSA_KV_HEADS * HEAD_DIM,
               3 * NSA_HEADS)
EVEN_IN = SSM_D_INNER + SSM_CONV_DIM + SSM_HEADS + NSA_HEADS * HEAD_DIM + 6 * NSA_KV_HEADS * HEAD_DIM + 3 * NSA_HEADS
EVEN_OUT = SSM_D_INNER + NSA_HEADS * HEAD_DIM
ODD_WIDTHS = (MOBA_HEADS * HEAD_DIM, 2 * MOBA_KV_HEADS * HEAD_DIM)
ODD_IN = MOBA_HEADS * HEAD_DIM + 2 * MOBA_KV_HEADS * HEAD_DIM

WIN_QBLOCK = 128
SEL_QBLOCK = 16
MOBA_QBLOCK = 4
PEER_TBLOCK = 128

kernel_name = "hybrid_ssd_nsa_moba_peer_step"


def rmsnorm(x, g):
    xf = x.astype(jnp.float32)
    y = xf * lax.rsqrt(jnp.mean(xf * xf, axis=-1, keepdims=True) + NORM_EPS)
    return (y * g.astype(jnp.float32)).astype(x.dtype)


def split_cols(a, widths):
    outs, off = [], 0
    for w in widths:
        outs.append(a[..., off:off + w])
        off += w
    return outs


def pad_axis1(a, n):
    return jnp.pad(a, [(0, 0), (0, n - a.shape[1])] + [(0, 0)] * (a.ndim - 2))


def qblocks(T, qmax):
    qb = min(qmax, T)
    nb = -(-T // qb)
    return qb, nb, nb * qb


def run_blocks(fn, nb, qb, T):
    out = lax.map(fn, jnp.arange(nb))
    out = jnp.moveaxis(out, 0, 1)
    return out.reshape(out.shape[:1] + (nb * qb,) + out.shape[3:])[:, :T]


def masked_softmax(s, mask):
    s = jnp.where(mask, s.astype(jnp.float32), NEG_INF)
    m = jnp.max(s, axis=-1, keepdims=True)
    p = jnp.exp(s - m) * mask
    return p / jnp.maximum(jnp.sum(p, axis=-1, keepdims=True), 1e-30)


def partial_rope(x, pos):
    half = ROT_DIM // 2
    inv = ROPE_THETA ** (-jnp.arange(0, ROT_DIM, 2, dtype=jnp.float32) / ROT_DIM)
    ang = pos.astype(jnp.float32)[:, None] * inv[None, :]
    cos = jnp.cos(ang)[:, None, :]
    sin = jnp.sin(ang)[:, None, :]
    x1 = x[..., :half].astype(jnp.float32)
    x2 = x[..., half:ROT_DIM].astype(jnp.float32)
    rot = jnp.concatenate([x1 * cos - x2 * sin, x2 * cos + x1 * sin], axis=-1).astype(x.dtype)
    return jnp.concatenate([rot, x[..., ROT_DIM:]], axis=-1)


def rope_keys(kv, pos):
    return jnp.stack([partial_rope(kv[:, :, 0], pos), kv[:, :, 1]], axis=2)


def gather_pages(pool, page_table):
    g = pool[page_table]
    return g.reshape((g.shape[0], g.shape[1] * g.shape[2]) + g.shape[3:])


def ssd_chunked(x, dt, a, b_h, c_h, h0):
    B, L, H, P = x.shape
    N = b_h.shape[-1]
    f32 = jnp.float32
    Q = min(SSD_CHUNK, L)
    nc = -(-L // Q)
    Lp = nc * Q
    xdt = pad_axis1(x.astype(f32) * dt[..., None], Lp).reshape(B, nc, Q, H, P)
    da = pad_axis1(dt * a, Lp).reshape(B, nc, Q, H)
    bc = pad_axis1(b_h.astype(f32), Lp).reshape(B, nc, Q, H, N)
    cc = pad_axis1(c_h.astype(f32), Lp).reshape(B, nc, Q, H, N)
    acum = jnp.cumsum(da, axis=2)
    causal = jnp.tril(jnp.ones((Q, Q), bool))
    seg = acum[:, :, :, None, :] - acum[:, :, None, :, :]
    decay_in = jnp.exp(jnp.where(causal[None, None, :, :, None], seg, NEG_INF))
    scores = jnp.einsum('bclhn,bcshn->bclsh', cc, bc) * decay_in
    y_diag = jnp.einsum('bclsh,bcshp->bclhp', scores, xdt)
    decay_out = jnp.exp(acum[:, :, -1:] - acum)
    chunk_states = jnp.einsum('bcshn,bcshp->bchpn', bc * decay_out[..., None], xdt)
    chunk_decay = jnp.exp(acum[:, :, -1])

    def step(h, inp):
        st, dec = inp
        return h * dec[:, :, None, None] + st, h

    h_last, h_enter = lax.scan(step, h0.astype(f32),
                               (jnp.moveaxis(chunk_states, 1, 0), jnp.moveaxis(chunk_decay, 1, 0)))
    h_enter = jnp.moveaxis(h_enter, 0, 1)
    y_off = jnp.einsum('bclhn,bchpn->bclhp', cc * jnp.exp(acum)[..., None], h_enter)
    y = (y_diag + y_off).reshape(B, Lp, H, P)[:, :L]
    return y, h_last


def mamba2_ssd(z, xbc, dt_raw, conv_state, ssm_state, conv_w, conv_b, dt_bias, a_log, d_skip, norm_g):
    B, T, _ = xbc.shape
    f32 = jnp.float32
    xpad = jnp.concatenate([conv_state.astype(xbc.dtype), xbc], axis=1)
    new_conv = xpad[:, T:]
    conv = conv_b
    for k in range(SSM_CONV):
        conv = conv + xpad[:, k:k + T] * conv_w[k]
    xbc_c = jax.nn.silu(conv)
    n_bc = SSM_GROUPS * SSM_STATE
    rep = SSM_HEADS // SSM_GROUPS
    xs = xbc_c[..., :SSM_D_INNER].reshape(B, T, SSM_HEADS, SSM_HEAD_DIM)
    b_h = jnp.repeat(xbc_c[..., SSM_D_INNER:SSM_D_INNER + n_bc].reshape(B, T, SSM_GROUPS, SSM_STATE), rep, axis=2)
    c_h = jnp.repeat(xbc_c[..., SSM_D_INNER + n_bc:].reshape(B, T, SSM_GROUPS, SSM_STATE), rep, axis=2)
    dt = jax.nn.softplus(dt_raw.astype(f32) + dt_bias.astype(f32))
    a = -jnp.exp(a_log.astype(f32))
    y, h_last = ssd_chunked(xs, dt, a, b_h, c_h, ssm_state)
    y = y + d_skip.astype(f32)[:, None] * xs.astype(f32)
    y = y.reshape(B, T, SSM_D_INNER) * jax.nn.silu(z.astype(f32))
    yg = y.reshape(B, T, SSM_GROUPS, SSM_D_INNER // SSM_GROUPS)
    yg = yg * lax.rsqrt(jnp.mean(yg * yg, axis=-1, keepdims=True) + NORM_EPS)
    y = yg.reshape(B, T, SSM_D_INNER) * norm_g.astype(f32)
    return y.astype(xbc.dtype), new_conv, h_last.astype(ssm_state.dtype)


def nsa_compress(kv_all, pos_emb, w1, b1, w2, b2):
    B, L = kv_all.shape[:2]
    S = NSA_CMP_STRIDE
    r = NSA_CMP_BLOCK // S
    n_seg = L // S
    n_cmp = n_seg - r + 1
    seg = kv_all[:, :n_seg * S].reshape(B, n_seg, S, 2, NSA_KV_HEADS, HEAD_DIM)
    h = b1[None, None, :, None, :]
    for j in range(r):
        pe = jnp.transpose(pos_emb[:, j * S:(j + 1) * S], (1, 0, 2))[:, :, None, :]
        h = h + jnp.einsum('bnlcgd,cldh->bncgh', seg[:, j:j + n_cmp] + pe, w1[:, j * S:(j + 1) * S])
    h = jax.nn.gelu(h)
    return jnp.einsum('bncgh,chd->bncgd', h, w2) + b2[None, None, :, None, :]


def nsa_selected(qr, kvs_all, sel_idx, sel_ok, pos0):
    B, T = qr.shape[:2]
    Lk = kvs_all.shape[1]
    n_sel = -(-Lk // NSA_SEL_BLOCK)
    topn = sel_idx.shape[-1]
    kvb = pad_axis1(kvs_all, n_sel * NSA_SEL_BLOCK).reshape(B, n_sel, NSA_SEL_BLOCK, 2, NSA_KV_HEADS, HEAD_DIM)
    kvb = jnp.transpose(kvb, (0, 4, 1, 2, 3, 5))
    qb, nb, Tp = qblocks(T, SEL_QBLOCK)
    qp, ip, okp = pad_axis1(qr, Tp), pad_axis1(sel_idx, Tp), pad_axis1(sel_ok, Tp)
    bi = jnp.arange(B)[:, None, None, None]
    gi = jnp.arange(NSA_KV_HEADS)[None, None, :, None]
    n_keys = topn * NSA_SEL_BLOCK

    def blk(i):
        start = i * qb
        q_b = lax.dynamic_slice_in_dim(qp, start, qb, 1)
        i_b = lax.dynamic_slice_in_dim(ip, start, qb, 1)
        ok_b = lax.dynamic_slice_in_dim(okp, start, qb, 1)
        qpos = pos0 + start + jnp.arange(qb)
        g = kvb[bi, gi, i_b]
        kpos = i_b[..., None] * NSA_SEL_BLOCK + jnp.arange(NSA_SEL_BLOCK)
        mask = (ok_b[..., None] & (kpos <= qpos[None, :, None, None, None])).reshape(B, qb, NSA_KV_HEADS, 1, n_keys)
        kk = g[..., 0, :].reshape(B, qb, NSA_KV_HEADS, n_keys, HEAD_DIM)
        vv = g[..., 1, :].reshape(B, qb, NSA_KV_HEADS, n_keys, HEAD_DIM)
        p = masked_softmax(jnp.einsum('bqgjd,bqgkd->bqgjk', q_b, kk) * ATTN_SCALE, mask)
        return jnp.einsum('bqgjk,bqgkd->bqgjd', p, vv)

    return run_blocks(blk, nb, qb, T)


def nsa_window(qr, kvw_ext, pos0):
    B, T = qr.shape[:2]
    Wb = kvw_ext.shape[1] - T
    W = NSA_WINDOW
    qb, nb, Tp = qblocks(T, WIN_QBLOCK)
    qp = pad_axis1(qr, Tp)
    kvp = jnp.pad(kvw_ext, [(0, 0), (W, Tp - T), (0, 0), (0, 0), (0, 0)])
    n_kp = W + Wb + Tp
    idx = jnp.arange(n_kp)
    kpos = (pos0 - Wb - W) + idx
    kval = (idx >= W) & (idx < W + Wb + T)

    def blk(i):
        start = i * qb
        q_b = lax.dynamic_slice_in_dim(qp, start, qb, 1)
        kv_b = lax.dynamic_slice_in_dim(kvp, start + Wb, W + qb, 1)
        kp_b = lax.dynamic_slice_in_dim(kpos, start + Wb, W + qb, 0)
        ok_b = lax.dynamic_slice_in_dim(kval, start + Wb, W + qb, 0)
        qpos = pos0 + start + jnp.arange(qb)
        mask = ok_b[None, :] & (kp_b[None, :] <= qpos[:, None]) & (qpos[:, None] - kp_b[None, :] < W)
        s = jnp.einsum('bqgjd,bkgd->bqgjk', q_b, kv_b[:, :, 0]) * ATTN_SCALE
        p = masked_softmax(s, mask[None, :, None, None, :])
        return jnp.einsum('bqgjk,bkgd->bqgjd', p, kv_b[:, :, 1])

    return run_blocks(blk, nb, qb, T)


def nsa_attention(q, kvc_all, kvs_all, kvw_ext, gates, pos0, cmp_pos, cmp_w1, cmp_b1, cmp_w2, cmp_b2):
    B, T = q.shape[:2]
    pos_q = pos0 + jnp.arange(T, dtype=jnp.int32)
    qg = q.reshape(B, T, NSA_KV_HEADS, NSA_GQ, HEAD_DIM)
    qr = partial_rope(q, pos_q).reshape(B, T, NSA_KV_HEADS, NSA_GQ, HEAD_DIM)
    cmp = nsa_compress(kvc_all, cmp_pos, cmp_w1, cmp_b1, cmp_w2, cmp_b2)
    n_cmp = cmp.shape[1]
    cmp_start = NSA_CMP_STRIDE * jnp.arange(n_cmp, dtype=jnp.int32)
    cmp_mask = (cmp_start + NSA_CMP_BLOCK - 1)[None, :] <= pos_q[:, None]
    s = jnp.einsum('btgjd,bngd->btgjn', qg, cmp[:, :, 0]) * ATTN_SCALE
    p_cmp = masked_softmax(s, cmp_mask[None, :, None, None, :])
    o_cmp = jnp.einsum('btgjn,bngd->btgjd', p_cmp, cmp[:, :, 1])
    n_sel = -(-kvs_all.shape[1] // NSA_SEL_BLOCK)
    sel_start = NSA_SEL_BLOCK * jnp.arange(n_sel, dtype=jnp.int32)
    overlap = ((cmp_start[:, None] < sel_start[None, :] + NSA_SEL_BLOCK)
               & (cmp_start[:, None] + NSA_CMP_BLOCK > sel_start[None, :])).astype(jnp.float32)
    imp = jnp.einsum('btgjn,ns->btgs', p_cmp, overlap)
    blk_t = pos_q // NSA_SEL_BLOCK
    sidx = jnp.arange(n_sel, dtype=jnp.int32)
    valid = sidx[None, :] <= blk_t[:, None]
    forced = valid & ((sidx[None, :] == 0) | (sidx[None, :] > blk_t[:, None] - NSA_LOCAL_BLOCKS))
    imp = jnp.where(forced[None, :, None, :], NSA_FORCE, imp)
    imp = jnp.where(valid[None, :, None, :], imp, NEG_INF)
    sel_score, sel_idx = lax.top_k(imp, min(NSA_TOPN, n_sel))
    o_sel = nsa_selected(qr, kvs_all, sel_idx, sel_score > 0.5 * NEG_INF, pos0)
    o_win = nsa_window(qr, kvw_ext, pos0)
    g = jax.nn.sigmoid(gates.astype(jnp.float32)).reshape(B, T, NSA_KV_HEADS, NSA_GQ, 3)
    o = g[..., 0:1] * o_cmp + g[..., 1:2] * o_sel + g[..., 2:3] * o_win
    return o.reshape(B, T, NSA_HEADS * HEAD_DIM).astype(q.dtype)


def moba_attention(qr, kv_all, pos0):
    B, T = qr.shape[:2]
    Lk = kv_all.shape[1]
    nblk = -(-Lk // MOBA_BLOCK)
    kvb = pad_axis1(kv_all, nblk * MOBA_BLOCK).reshape(B, nblk, MOBA_BLOCK, 2, MOBA_KV_HEADS, HEAD_DIM)
    kvb = jnp.transpose(kvb, (0, 4, 1, 2, 3, 5))
    kmean = jnp.mean(kvb[..., 0, :].astype(jnp.float32), axis=3)
    pos_q = pos0 + jnp.arange(T, dtype=jnp.int32)
    qg = qr.reshape(B, T, MOBA_KV_HEADS, MOBA_GQ, HEAD_DIM)
    gate = jnp.einsum('btgjd,bgnd->btgjn', qg.astype(jnp.float32), kmean)
    past_ok = jnp.arange(nblk)[None, :] < (pos_q // MOBA_BLOCK)[:, None]
    gate = jnp.where(past_ok[None, :, None, None, :], gate, NEG_INF)
    k = min(MOBA_TOPK, nblk)
    sc, idx = lax.top_k(gate, k)
    ok = sc > 0.5 * NEG_INF
    qb, nb, Tp = qblocks(T, MOBA_QBLOCK)
    qp, ip, okp = pad_axis1(qg, Tp), pad_axis1(idx, Tp), pad_axis1(ok, Tp)
    bi = jnp.arange(B)[:, None, None, None, None]
    gi = jnp.arange(MOBA_KV_HEADS)[None, None, :, None, None]
    bo = jnp.arange(B)[:, None, None]
    go = jnp.arange(MOBA_KV_HEADS)[None, None, :]
    n_sel = k * MOBA_BLOCK

    def blk(i):
        start = i * qb
        q_b = lax.dynamic_slice_in_dim(qp, start, qb, 1)
        i_b = lax.dynamic_slice_in_dim(ip, start, qb, 1)
        ok_b = lax.dynamic_slice_in_dim(okp, start, qb, 1)
        qpos = pos0 + start + jnp.arange(qb)
        own_blk = jnp.minimum(qpos // MOBA_BLOCK, nblk - 1)
        g_sel = kvb[bi, gi, i_b]
        g_own = kvb[bo, go, own_blk[None, :, None]]
        s_sel = jnp.einsum('bqgjd,bqgjkld->bqgjkl', q_b, g_sel[..., 0, :]).reshape(B, qb, MOBA_KV_HEADS, MOBA_GQ, n_sel)
        s_own = jnp.einsum('bqgjd,bqgld->bqgjl', q_b, g_own[..., 0, :])
        m_sel = jnp.broadcast_to(ok_b[..., None], ok_b.shape + (MOBA_BLOCK,)).reshape(B, qb, MOBA_KV_HEADS, MOBA_GQ, n_sel)
        own_pos = own_blk[:, None] * MOBA_BLOCK + jnp.arange(MOBA_BLOCK)
        m_own = jnp.broadcast_to((own_pos <= qpos[:, None])[None, :, None, None, :], (B, qb, MOBA_KV_HEADS, MOBA_GQ, MOBA_BLOCK))
        p = masked_softmax(jnp.concatenate([s_sel, s_own], axis=-1) * ATTN_SCALE,
                           jnp.concatenate([m_sel, m_own], axis=-1))
        v_sel = g_sel[..., 1, :].reshape(B, qb, MOBA_KV_HEADS, MOBA_GQ, n_sel, HEAD_DIM)
        return (jnp.einsum('bqgjm,bqgjmd->bqgjd', p[..., :n_sel], v_sel)
                + jnp.einsum('bqgjl,bqgld->bqgjd', p[..., n_sel:], g_own[..., 1, :]))

    o = run_blocks(blk, nb, qb, T)
    return o.reshape(B, T, MOBA_HEADS * HEAD_DIM).astype(qr.dtype)


def peer_ffn(x, wq, subkeys, u, v):
    B, T, D = x.shape
    n = B * T
    tb = min(PEER_TBLOCK, n)
    nb = -(-n // tb)
    xt = jnp.pad(x.reshape(n, D), ((0, nb * tb - n), (0, 0)))
    half = PEER_KEY_DIM // 2
    kk = PEER_TOPK * PEER_TOPK

    def block(i):
        xb = lax.dynamic_slice_in_dim(xt, i * tb, tb, 0)
        q = (xb @ wq).reshape(tb, PEER_HEADS, 2, half)
        s = jnp.einsum('thcd,ckd->thck', q, subkeys).astype(jnp.float32)
        s1, i1 = lax.top_k(s[:, :, 0], PEER_TOPK)
        s2, i2 = lax.top_k(s[:, :, 1], PEER_TOPK)
        cand_s = (s1[..., :, None] + s2[..., None, :]).reshape(tb, PEER_HEADS, kk)
        cand_i = (i1[..., :, None] * PEER_N_KEYS + i2[..., None, :]).reshape(tb, PEER_HEADS, kk)
        top_s, pick = lax.top_k(cand_s, PEER_TOPK)
        expert = jnp.take_along_axis(cand_i, pick, axis=-1)
        gate = jax.nn.softmax(top_s, axis=-1)
        act = jax.nn.gelu(jnp.einsum('thkd,td->thk', u[expert], xb).astype(jnp.float32))
        return jnp.einsum('thk,thkd->td', (gate * act).astype(v.dtype), v[expert])

    y = lax.map(block, jnp.arange(nb))
    return y.reshape(nb * tb, D)[:n].reshape(B, T, D).astype(x.dtype)


def setup_inputs(seed: int = 0) -> dict:
    key = jax.random.key(seed)
    ks = iter(jax.random.split(key, 48))
    f32 = jnp.float32

    def nrm(shape, scale):
        return jax.random.normal(next(ks), shape, f32) * scale

    n_pages = PAST_LEN // PAGE_SIZE
    n_pool = (DEC_BATCH * n_pages * 5) // 4
    win_buf = min(NSA_WINDOW, PAST_LEN)
    nsa_tail = (2, NSA_KV_HEADS, HEAD_DIM)
    x_prompt = nrm((BATCH, SEQ, D_MODEL), 1.0)
    x_sample = nrm((DEC_BATCH, DEC_SEQ, D_MODEL), 1.0)
    cache_nsa_cmp_kv = nrm((N_EVEN, n_pool, PAGE_SIZE) + nsa_tail, 1.0)
    cache_nsa_sel_kv = nrm((N_EVEN, n_pool, PAGE_SIZE) + nsa_tail, 1.0)
    state_nsa_win_kv = nrm((N_EVEN, DEC_BATCH, win_buf) + nsa_tail, 1.0)
    state_ssm = nrm((N_EVEN, DEC_BATCH, SSM_HEADS, SSM_HEAD_DIM, SSM_STATE), 0.1)
    state_conv = nrm((N_EVEN, DEC_BATCH, SSM_CONV - 1, SSM_CONV_DIM), 1.0)
    cache_moba_kv = nrm((N_ODD, n_pool, PAGE_SIZE, 2, MOBA_KV_HEADS, HEAD_DIM), 1.0)
    page_table = jax.random.permutation(next(ks), n_pool)[:DEC_BATCH * n_pages].reshape(DEC_BATCH, n_pages).astype(jnp.int32)
    norm_mix = 1.0 + nrm((DEPTH, D_MODEL), 0.02)
    norm_ffn = 1.0 + nrm((DEPTH, D_MODEL), 0.02)
    norm_final = 1.0 + nrm((D_MODEL,), 0.02)
    w_in_even = nrm((N_EVEN, D_MODEL, EVEN_IN), D_MODEL ** -0.5)
    w_out_even = nrm((N_EVEN, EVEN_OUT, D_MODEL), EVEN_OUT ** -0.5)
    ssm_conv_w = nrm((N_EVEN, SSM_CONV, SSM_CONV_DIM), SSM_CONV ** -0.5)
    ssm_conv_b = nrm((N_EVEN, SSM_CONV_DIM), 0.01)
    dt0 = jnp.exp(jax.random.uniform(next(ks), (N_EVEN, SSM_HEADS), f32, math.log(1e-3), math.log(1e-1)))
    ssm_dt_bias = dt0 + jnp.log(-jnp.expm1(-dt0))
    ssm_a_log = jnp.log(jax.random.uniform(next(ks), (N_EVEN, SSM_HEADS), f32, 1.0, 16.0))
    ssm_d_skip = 1.0 + nrm((N_EVEN, SSM_HEADS), 0.1)
    ssm_norm = 1.0 + nrm((N_EVEN, SSM_D_INNER), 0.02)
    nsa_cmp_pos = nrm((N_EVEN, 2, NSA_CMP_BLOCK, HEAD_DIM), 0.02)
    nsa_cmp_w1 = nrm((N_EVEN, 2, NSA_CMP_BLOCK, HEAD_DIM, NSA_CMP_HIDDEN), (NSA_CMP_BLOCK * HEAD_DIM) ** -0.5)
    nsa_cmp_b1 = nrm((N_EVEN, 2, NSA_CMP_HIDDEN), 0.01)
    nsa_cmp_w2 = nrm((N_EVEN, 2, NSA_CMP_HIDDEN, HEAD_DIM), NSA_CMP_HIDDEN ** -0.5)
    nsa_cmp_b2 = nrm((N_EVEN, 2, HEAD_DIM), 0.01)
    nsa_gate_b = nrm((N_EVEN, 3 * NSA_HEADS), 0.01)
    w_in_odd = nrm((N_ODD, D_MODEL, ODD_IN), D_MODEL ** -0.5)
    w_out_odd = nrm((N_ODD, MOBA_HEADS * HEAD_DIM, D_MODEL), (MOBA_HEADS * HEAD_DIM) ** -0.5)
    peer_wq = nrm((DEPTH, D_MODEL, PEER_HEADS * PEER_KEY_DIM), D_MODEL ** -0.5)
    peer_subkeys = nrm((DEPTH, 2, PEER_N_KEYS, PEER_KEY_DIM // 2), (PEER_KEY_DIM // 2) ** -0.5)
    peer_u = nrm((DEPTH, PEER_N_EXPERTS, D_MODEL), D_MODEL ** -0.5)
    peer_v = nrm((DEPTH, PEER_N_EXPERTS, D_MODEL), (PEER_HEADS * PEER_TOPK) ** -0.5)
    return {"x_prompt": x_prompt, "x_sample": x_sample,
            "cache_nsa_cmp_kv": cache_nsa_cmp_kv, "cache_nsa_sel_kv": cache_nsa_sel_kv,
            "state_nsa_win_kv": state_nsa_win_kv, "state_ssm": state_ssm, "state_conv": state_conv,
            "cache_moba_kv": cache_moba_kv, "page_table": page_table,
            "norm_mix": norm_mix, "norm_ffn": norm_ffn, "norm_final": norm_final,
            "w_in_even": w_in_even, "w_out_even": w_out_even,
            "ssm_conv_w": ssm_conv_w, "ssm_conv_b": ssm_conv_b, "ssm_dt_bias": ssm_dt_bias,
            "ssm_a_log": ssm_a_log, "ssm_d_skip": ssm_d_skip, "ssm_norm": ssm_norm,
            "nsa_cmp_pos": nsa_cmp_pos, "nsa_cmp_w1": nsa_cmp_w1, "nsa_cmp_b1": nsa_cmp_b1,
            "nsa_cmp_w2": nsa_cmp_w2, "nsa_cmp_b2": nsa_cmp_b2, "nsa_gate_b": nsa_gate_b,
            "w_in_odd": w_in_odd, "w_out_odd": w_out_odd,
            "peer_wq": peer_wq, "peer_subkeys": peer_subkeys, "peer_u": peer_u, "peer_v": peer_v}


def reference(x_prompt, x_sample, cache_nsa_cmp_kv, cache_nsa_sel_kv, state_nsa_win_kv, state_ssm, state_conv,
              cache_moba_kv, page_table, norm_mix, norm_ffn, norm_final, w_in_even, w_out_even,
              ssm_conv_w, ssm_conv_b, ssm_dt_bias, ssm_a_log, ssm_d_skip, ssm_norm,
              nsa_cmp_pos, nsa_cmp_w1, nsa_cmp_b1, nsa_cmp_w2, nsa_cmp_b2, nsa_gate_b,
              w_in_odd, w_out_odd, peer_wq, peer_subkeys, peer_u, peer_v):

    def trunk(x, pos0, get_past):
        B, T, _ = x.shape
        pos_q = pos0 + jnp.arange(T, dtype=jnp.int32)
        new_cmp, new_sel, new_win, new_ssm, new_conv, new_moba = [], [], [], [], [], []
        for l in range(DEPTH):
            h = rmsnorm(x, norm_mix[l])
            if l % 2 == 0:
                e = l // 2
                z, xbc, dt_raw, q, kvc, kvs, kvw, gts = split_cols(h @ w_in_even[e], EVEN_WIDTHS)
                y_ssm, conv_new, ssm_new = mamba2_ssd(z, xbc, dt_raw, get_past('conv', e), get_past('ssm', e),
                                                      ssm_conv_w[e], ssm_conv_b[e], ssm_dt_bias[e], ssm_a_log[e],
                                                      ssm_d_skip[e], ssm_norm[e])
                kv_shape = (B, T, 2, NSA_KV_HEADS, HEAD_DIM)
                kvc = kvc.reshape(kv_shape)
                kvs = rope_keys(kvs.reshape(kv_shape), pos_q)
                kvw = rope_keys(kvw.reshape(kv_shape), pos_q)
                kvw_ext = jnp.concatenate([get_past('win', e), kvw], axis=1)
                y_nsa = nsa_attention(q.reshape(B, T, NSA_HEADS, HEAD_DIM),
                                      jnp.concatenate([get_past('cmp', e), kvc], axis=1),
                                      jnp.concatenate([get_past('sel', e), kvs], axis=1),
                                      kvw_ext, gts + nsa_gate_b[e], pos0,
                                      nsa_cmp_pos[e], nsa_cmp_w1[e], nsa_cmp_b1[e], nsa_cmp_w2[e], nsa_cmp_b2[e])
                mix = jnp.concatenate([y_ssm, y_nsa], axis=-1) @ w_out_even[e]
                keep = min(NSA_WINDOW, kvw_ext.shape[1])
                new_cmp.append(kvc)
                new_sel.append(kvs)
                new_win.append(kvw_ext[:, kvw_ext.shape[1] - keep:])
                new_ssm.append(ssm_new)
                new_conv.append(conv_new)
            else:
                o = l // 2
                q, kv = split_cols(h @ w_in_odd[o], ODD_WIDTHS)
                q = partial_rope(q.reshape(B, T, MOBA_HEADS, HEAD_DIM), pos_q)
                kv = rope_keys(kv.reshape(B, T, 2, MOBA_KV_HEADS, HEAD_DIM), pos_q)
                y_moba = moba_attention(q, jnp.concatenate([get_past('moba', o), kv], axis=1), pos0)
                mix = y_moba @ w_out_odd[o]
                new_moba.append(kv)
            x = x + mix.astype(x.dtype)
            x = x + peer_ffn(rmsnorm(x, norm_ffn[l]), peer_wq[l], peer_subkeys[l], peer_u[l], peer_v[l])
        return (rmsnorm(x, norm_final), jnp.stack(new_cmp), jnp.stack(new_sel), jnp.stack(new_win),
                jnp.stack(new_ssm), jnp.stack(new_conv), jnp.stack(new_moba))

    bp = x_prompt.shape[0]
    dtp = x_prompt.dtype

    def prompt_past(kind, i):
        if kind == 'ssm':
            return jnp.zeros((bp, SSM_HEADS, SSM_HEAD_DIM, SSM_STATE), state_ssm.dtype)
        if kind == 'conv':
            return jnp.zeros((bp, SSM_CONV - 1, SSM_CONV_DIM), dtp)
        if kind == 'moba':
            return jnp.zeros((bp, 0, 2, MOBA_KV_HEADS, HEAD_DIM), dtp)
        return jnp.zeros((bp, 0, 2, NSA_KV_HEADS, HEAD_DIM), dtp)

    def sample_past(kind, i):
        if kind == 'ssm':
            return state_ssm[i]
        if kind == 'conv':
            return state_conv[i]
        if kind == 'win':
            return state_nsa_win_kv[i]
        if kind == 'cmp':
            return gather_pages(cache_nsa_cmp_kv[i], page_table)
        if kind == 'sel':
            return gather_pages(cache_nsa_sel_kv[i], page_table)
        return gather_pages(cache_moba_kv[i], page_table)

    past_len = page_table.shape[1] * cache_nsa_cmp_kv.shape[2]
    y_prompt, p_cmp, p_sel, p_win, p_ssm, p_conv, p_moba = trunk(x_prompt, 0, prompt_past)
    y_sample, s_cmp, s_sel, s_win, s_ssm, s_conv, s_moba = trunk(x_sample, past_len, sample_past)
    return (y_prompt, y_sample, p_cmp, p_sel, p_win, p_ssm, p_conv, p_moba,
            s_cmp, s_sel, s_win, s_ssm, s_conv, s_moba)
```

```python
import functools
import math

import jax
import jax.numpy as jnp
from jax import lax
from jax.experimental import pallas as pl
from jax.experimental.pallas import tpu as pltpu

D_MODEL = 1024
DEPTH = 4
HEAD_DIM = 64
ROT_DIM = HEAD_DIM // 4
ROPE_THETA = 500000.0
ATTN_SCALE = HEAD_DIM ** -0.5
NORM_EPS = 1e-6
NEG_INF = -1e30

SSM_HEADS = 16
SSM_HEAD_DIM = 64
SSM_D_INNER = SSM_HEADS * SSM_HEAD_DIM
SSM_GROUPS = 2
SSM_STATE = 128
SSM_CONV = 4
SSM_CONV_DIM = SSM_D_INNER + 2 * SSM_GROUPS * SSM_STATE
SSD_CHUNK = 128

NSA_HEADS = 16
NSA_KV_HEADS = 4
NSA_GQ = NSA_HEADS // NSA_KV_HEADS
NSA_CMP_BLOCK = 32
NSA_CMP_STRIDE = 16
NSA_SEL_BLOCK = 64
NSA_TOPN = 8
NSA_LOCAL_BLOCKS = 2
NSA_FORCE = 1e6
NSA_WINDOW = 512

MOBA_HEADS = 16
MOBA_KV_HEADS = 4
MOBA_GQ = MOBA_HEADS // MOBA_KV_HEADS
MOBA_BLOCK = 256
MOBA_TOPK = 3

PEER_HEADS = 8
PEER_N_KEYS = 128
PEER_KEY_DIM = 256
PEER_TOPK = 16

EVEN_WIDTHS = (SSM_D_INNER, SSM_CONV_DIM, SSM_HEADS, NSA_HEADS * HEAD_DIM,
               2 * NSA_KV_HEADS * HEAD_DIM, 2 * NSA_KV_HEADS * HEAD_DIM, 2 * NSA_KV_HEADS * HEAD_DIM,
               3 * NSA_HEADS)
ODD_WIDTHS = (MOBA_HEADS * HEAD_DIM, 2 * MOBA_KV_HEADS * HEAD_DIM)

WIN_QBLOCK = 128
SEL_QBLOCK = 16
MOBA_QBLOCK = 4
PEER_TBLOCK = 128

VMEM_LIMIT_BYTES = 48 * 1024 * 1024


def _mm_kernel(x_ref, g_ref, w_ref, o_ref, *, normalize):
    x = x_ref[...]
    if normalize:
        x = x * lax.rsqrt(jnp.mean(x * x, axis=-1, keepdims=True) + NORM_EPS) * g_ref[...]
    o_ref[...] = jnp.dot(x.astype(jnp.bfloat16), w_ref[...], preferred_element_type=jnp.float32)


def norm_matmul(x, w, g=None):
    m, k = x.shape
    n = w.shape[1]
    tm = min(m, 256)
    assert m % tm == 0
    gg = jnp.ones((1, k), jnp.float32) if g is None else g.reshape(1, k).astype(jnp.float32)
    return pl.pallas_call(
        functools.partial(_mm_kernel, normalize=g is not None),
        grid=(m // tm,),
        in_specs=[pl.BlockSpec((tm, k), lambda i: (i, 0)),
                  pl.BlockSpec((1, k), lambda i: (0, 0)),
                  pl.BlockSpec((k, n), lambda i: (0, 0))],
        out_specs=pl.BlockSpec((tm, n), lambda i: (i, 0)),
        out_shape=jax.ShapeDtypeStruct((m, n), jnp.float32),
        compiler_params=pltpu.CompilerParams(dimension_semantics=("arbitrary",),
                                             vmem_limit_bytes=VMEM_LIMIT_BYTES),
        name="norm_matmul",
    )(x, gg, w.astype(jnp.bfloat16))


def rmsnorm(x, g):
    xf = x.astype(jnp.float32)
    y = xf * lax.rsqrt(jnp.mean(xf * xf, axis=-1, keepdims=True) + NORM_EPS)
    return (y * g.astype(jnp.float32)).astype(x.dtype)


def split_cols(a, widths):
    outs, off = [], 0
    for w in widths:
        outs.append(a[..., off:off + w])
        off += w
    return outs


def pad_axis1(a, n):
    return jnp.pad(a, [(0, 0), (0, n - a.shape[1])] + [(0, 0)] * (a.ndim - 2))


def qblocks(T, qmax):
    qb = min(qmax, T)
    nb = -(-T // qb)
    return qb, nb, nb * qb


def run_blocks(fn, nb, qb, T):
    out = lax.map(fn, jnp.arange(nb))
    out = jnp.moveaxis(out, 0, 1)
    return out.reshape(out.shape[:1] + (nb * qb,) + out.shape[3:])[:, :T]


def masked_softmax(s, mask):
    s = jnp.where(mask, s.astype(jnp.float32), NEG_INF)
    m = jnp.max(s, axis=-1, keepdims=True)
    p = jnp.exp(s - m) * mask
    return p / jnp.maximum(jnp.sum(p, axis=-1, keepdims=True), 1e-30)


def partial_rope(x, pos):
    half = ROT_DIM // 2
    inv = ROPE_THETA ** (-jnp.arange(0, ROT_DIM, 2, dtype=jnp.float32) / ROT_DIM)
    ang = pos.astype(jnp.float32)[:, None] * inv[None, :]
    cos = jnp.cos(ang)[:, None, :]
    sin = jnp.sin(ang)[:, None, :]
    x1 = x[..., :half].astype(jnp.float32)
    x2 = x[..., half:ROT_DIM].astype(jnp.float32)
    rot = jnp.concatenate([x1 * cos - x2 * sin, x2 * cos + x1 * sin], axis=-1).astype(x.dtype)
    return jnp.concatenate([rot, x[..., ROT_DIM:]], axis=-1)


def rope_keys(kv, pos):
    return jnp.stack([partial_rope(kv[:, :, 0], pos), kv[:, :, 1]], axis=2)


def gather_pages(pool, page_table):
    g = pool[page_table]
    return g.reshape((g.shape[0], g.shape[1] * g.shape[2]) + g.shape[3:])


def ssd_chunked(x, dt, a, b_h, c_h, h0):
    B, L, H, P = x.shape
    N = b_h.shape[-1]
    f32 = jnp.float32
    Q = min(SSD_CHUNK, L)
    nc = -(-L // Q)
    Lp = nc * Q
    xdt = pad_axis1(x.astype(f32) * dt[..., None], Lp).reshape(B, nc, Q, H, P)
    da = pad_axis1(dt * a, Lp).reshape(B, nc, Q, H)
    bc = pad_axis1(b_h.astype(f32), Lp).reshape(B, nc, Q, H, N)
    cc = pad_axis1(c_h.astype(f32), Lp).reshape(B, nc, Q, H, N)
    acum = jnp.cumsum(da, axis=2)
    causal = jnp.tril(jnp.ones((Q, Q), bool))
    seg = acum[:, :, :, None, :] - acum[:, :, None, :, :]
    decay_in = jnp.exp(jnp.where(causal[None, None, :, :, None], seg, NEG_INF))
    scores = jnp.einsum('bclhn,bcshn->bclsh', cc, bc) * decay_in
    y_diag = jnp.einsum('bclsh,bcshp->bclhp', scores, xdt)
    decay_out = jnp.exp(acum[:, :, -1:] - acum)
    chunk_states = jnp.einsum('bcshn,bcshp->bchpn', bc * decay_out[..., None], xdt)
    chunk_decay = jnp.exp(acum[:, :, -1])

    def step(h, inp):
        st, dec = inp
        return h * dec[:, :, None, None] + st, h

    h_last, h_enter = lax.scan(step, h0.astype(f32),
                               (jnp.moveaxis(chunk_states, 1, 0), jnp.moveaxis(chunk_decay, 1, 0)))
    h_enter = jnp.moveaxis(h_enter, 0, 1)
    y_off = jnp.einsum('bclhn,bchpn->bclhp', cc * jnp.exp(acum)[..., None], h_enter)
    y = (y_diag + y_off).reshape(B, Lp, H, P)[:, :L]
    return y, h_last


def mamba2_ssd(z, xbc, dt_raw, conv_state, ssm_state, conv_w, conv_b, dt_bias, a_log, d_skip, norm_g):
    B, T, _ = xbc.shape
    f32 = jnp.float32
    xpad = jnp.concatenate([conv_state.astype(xbc.dtype), xbc], axis=1)
    new_conv = xpad[:, T:]
    conv = conv_b
    for k in range(SSM_CONV):
        conv = conv + xpad[:, k:k + T] * conv_w[k]
    xbc_c = jax.nn.silu(conv)
    n_bc = SSM_GROUPS * SSM_STATE
    rep = SSM_HEADS // SSM_GROUPS
    xs = xbc_c[..., :SSM_D_INNER].reshape(B, T, SSM_HEADS, SSM_HEAD_DIM)
    b_h = jnp.repeat(xbc_c[..., SSM_D_INNER:SSM_D_INNER + n_bc].reshape(B, T, SSM_GROUPS, SSM_STATE), rep, axis=2)
    c_h = jnp.repeat(xbc_c[..., SSM_D_INNER + n_bc:].reshape(B, T, SSM_GROUPS, SSM_STATE), rep, axis=2)
    dt = jax.nn.softplus(dt_raw.astype(f32) + dt_bias.astype(f32))
    a = -jnp.exp(a_log.astype(f32))
    y, h_last = ssd_chunked(xs, dt, a, b_h, c_h, ssm_state)
    y = y + d_skip.astype(f32)[:, None] * xs.astype(f32)
    y = y.reshape(B, T, SSM_D_INNER) * jax.nn.silu(z.astype(f32))
    yg = y.reshape(B, T, SSM_GROUPS, SSM_D_INNER // SSM_GROUPS)
    yg = yg * lax.rsqrt(jnp.mean(yg * yg, axis=-1, keepdims=True) + NORM_EPS)
    y = yg.reshape(B, T, SSM_D_INNER) * norm_g.astype(f32)
    return y.astype(xbc.dtype), new_conv, h_last.astype(ssm_state.dtype)


def nsa_compress(kv_all, pos_emb, w1, b1, w2, b2):
    B, L = kv_all.shape[:2]
    S = NSA_CMP_STRIDE
    r = NSA_CMP_BLOCK // S
    n_seg = L // S
    n_cmp = n_seg - r + 1
    seg = kv_all[:, :n_seg * S].reshape(B, n_seg, S, 2, NSA_KV_HEADS, HEAD_DIM)
    h = b1[None, None, :, None, :]
    for j in range(r):
        pe = jnp.transpose(pos_emb[:, j * S:(j + 1) * S], (1, 0, 2))[:, :, None, :]
        h = h + jnp.einsum('bnlcgd,cldh->bncgh', seg[:, j:j + n_cmp] + pe, w1[:, j * S:(j + 1) * S])
    h = jax.nn.gelu(h)
    return jnp.einsum('bncgh,chd->bncgd', h, w2) + b2[None, None, :, None, :]


def nsa_selected(qr, kvs_all, sel_idx, sel_ok, pos0):
    B, T = qr.shape[:2]
    Lk = kvs_all.shape[1]
    n_sel = -(-Lk // NSA_SEL_BLOCK)
    topn = sel_idx.shape[-1]
    kvb = pad_axis1(kvs_all, n_sel * NSA_SEL_BLOCK).reshape(B, n_sel, NSA_SEL_BLOCK, 2, NSA_KV_HEADS, HEAD_DIM)
    kvb = jnp.transpose(kvb, (0, 4, 1, 2, 3, 5))
    qb, nb, Tp = qblocks(T, SEL_QBLOCK)
    qp, ip, okp = pad_axis1(qr, Tp), pad_axis1(sel_idx, Tp), pad_axis1(sel_ok, Tp)
    bi = jnp.arange(B)[:, None, None, None]
    gi = jnp.arange(NSA_KV_HEADS)[None, None, :, None]
    n_keys = topn * NSA_SEL_BLOCK

    def blk(i):
        start = i * qb
        q_b = lax.dynamic_slice_in_dim(qp, start, qb, 1)
        i_b = lax.dynamic_slice_in_dim(ip, start, qb, 1)
        ok_b = lax.dynamic_slice_in_dim(okp, start, qb, 1)
        qpos = pos0 + start + jnp.arange(qb)
        g = kvb[bi, gi, i_b]
        kpos = i_b[..., None] * NSA_SEL_BLOCK + jnp.arange(NSA_SEL_BLOCK)
        mask = (ok_b[..., None] & (kpos <= qpos[None, :, None, None, None])).reshape(B, qb, NSA_KV_HEADS, 1, n_keys)
        kk = g[..., 0, :].reshape(B, qb, NSA_KV_HEADS, n_keys, HEAD_DIM)
        vv = g[..., 1, :].reshape(B, qb, NSA_KV_HEADS, n_keys, HEAD_DIM)
        p = masked_softmax(jnp.einsum('bqgjd,bqgkd->bqgjk', q_b, kk) * ATTN_SCALE, mask)
        return jnp.einsum('bqgjk,bqgkd->bqgjd', p, vv)

    return run_blocks(blk, nb, qb, T)


def nsa_window(qr, kvw_ext, pos0):
    B, T = qr.shape[:2]
    Wb = kvw_ext.shape[1] - T
    W = NSA_WINDOW
    qb, nb, Tp = qblocks(T, WIN_QBLOCK)
    qp = pad_axis1(qr, Tp)
    kvp = jnp.pad(kvw_ext, [(0, 0), (W, Tp - T), (0, 0), (0, 0), (0, 0)])
    n_kp = W + Wb + Tp
    idx = jnp.arange(n_kp)
    kpos = (pos0 - Wb - W) + idx
    kval = (idx >= W) & (idx < W + Wb + T)

    def blk(i):
        start = i * qb
        q_b = lax.dynamic_slice_in_dim(qp, start, qb, 1)
        kv_b = lax.dynamic_slice_in_dim(kvp, start + Wb, W + qb, 1)
        kp_b = lax.dynamic_slice_in_dim(kpos, start + Wb, W + qb, 0)
        ok_b = lax.dynamic_slice_in_dim(kval, start + Wb, W + qb, 0)
        qpos = pos0 + start + jnp.arange(qb)
        mask = ok_b[None, :] & (kp_b[None, :] <= qpos[:, None]) & (qpos[:, None] - kp_b[None, :] < W)
        s = jnp.einsum('bqgjd,bkgd->bqgjk', q_b, kv_b[:, :, 0]) * ATTN_SCALE
        p = masked_softmax(s, mask[None, :, None, None, :])
        return jnp.einsum('bqgjk,bkgd->bqgjd', p, kv_b[:, :, 1])

    return run_blocks(blk, nb, qb, T)


def nsa_attention(q, kvc_all, kvs_all, kvw_ext, gates, pos0, cmp_pos, cmp_w1, cmp_b1, cmp_w2, cmp_b2):
    B, T = q.shape[:2]
    pos_q = pos0 + jnp.arange(T, dtype=jnp.int32)
    qg = q.reshape(B, T, NSA_KV_HEADS, NSA_GQ, HEAD_DIM)
    qr = partial_rope(q, pos_q).reshape(B, T, NSA_KV_HEADS, NSA_GQ, HEAD_DIM)
    cmp = nsa_compress(kvc_all, cmp_pos, cmp_w1, cmp_b1, cmp_w2, cmp_b2)
    n_cmp = cmp.shape[1]
    cmp_start = NSA_CMP_STRIDE * jnp.arange(n_cmp, dtype=jnp.int32)
    cmp_mask = (cmp_start + NSA_CMP_BLOCK - 1)[None, :] <= pos_q[:, None]
    s = jnp.einsum('btgjd,bngd->btgjn', qg, cmp[:, :, 0]) * ATTN_SCALE
    p_cmp = masked_softmax(s, cmp_mask[None, :, None, None, :])
    o_cmp = jnp.einsum('btgjn,bngd->btgjd', p_cmp, cmp[:, :, 1])
    n_sel = -(-kvs_all.shape[1] // NSA_SEL_BLOCK)
    sel_start = NSA_SEL_BLOCK * jnp.arange(n_sel, dtype=jnp.int32)
    overlap = ((cmp_start[:, None] < sel_start[None, :] + NSA_SEL_BLOCK)
               & (cmp_start[:, None] + NSA_CMP_BLOCK > sel_start[None, :])).astype(jnp.float32)
    imp = jnp.einsum('btgjn,ns->btgs', p_cmp, overlap)
    blk_t = pos_q // NSA_SEL_BLOCK
    sidx = jnp.arange(n_sel, dtype=jnp.int32)
    valid = sidx[None, :] <= blk_t[:, None]
    forced = valid & ((sidx[None, :] == 0) | (sidx[None, :] > blk_t[:, None] - NSA_LOCAL_BLOCKS))
    imp = jnp.where(forced[None, :, None, :], NSA_FORCE, imp)
    imp = jnp.where(valid[None, :, None, :], imp, NEG_INF)
    sel_score, sel_idx = lax.top_k(imp, min(NSA_TOPN, n_sel))
    o_sel = nsa_selected(qr, kvs_all, sel_idx, sel_score > 0.5 * NEG_INF, pos0)
    o_win = nsa_window(qr, kvw_ext, pos0)
    g = jax.nn.sigmoid(gates.astype(jnp.float32)).reshape(B, T, NSA_KV_HEADS, NSA_GQ, 3)
    o = g[..., 0:1] * o_cmp + g[..., 1:2] * o_sel + g[..., 2:3] * o_win
    return o.reshape(B, T, NSA_HEADS * HEAD_DIM).astype(q.dtype)


def moba_attention(qr, kv_all, pos0):
    B, T = qr.shape[:2]
    Lk = kv_all.shape[1]
    nblk = -(-Lk // MOBA_BLOCK)
    kvb = pad_axis1(kv_all, nblk * MOBA_BLOCK).reshape(B, nblk, MOBA_BLOCK, 2, MOBA_KV_HEADS, HEAD_DIM)
    kvb = jnp.transpose(kvb, (0, 4, 1, 2, 3, 5))
    kmean = jnp.mean(kvb[..., 0, :].astype(jnp.float32), axis=3)
    pos_q = pos0 + jnp.arange(T, dtype=jnp.int32)
    qg = qr.reshape(B, T, MOBA_KV_HEADS, MOBA_GQ, HEAD_DIM)
    gate = jnp.einsum('btgjd,bgnd->btgjn', qg.astype(jnp.float32), kmean)
    past_ok = jnp.arange(nblk)[None, :] < (pos_q // MOBA_BLOCK)[:, None]
    gate = jnp.where(past_ok[None, :, None, None, :], gate, NEG_INF)
    k = min(MOBA_TOPK, nblk)
    sc, idx = lax.top_k(gate, k)
    ok = sc > 0.5 * NEG_INF
    qb, nb, Tp = qblocks(T, MOBA_QBLOCK)
    qp, ip, okp = pad_axis1(qg, Tp), pad_axis1(idx, Tp), pad_axis1(ok, Tp)
    bi = jnp.arange(B)[:, None, None, None, None]
    gi = jnp.arange(MOBA_KV_HEADS)[None, None, :, None, None]
    bo = jnp.arange(B)[:, None, None]
    go = jnp.arange(MOBA_KV_HEADS)[None, None, :]
    n_sel = k * MOBA_BLOCK

    def blk(i):
        start = i * qb
        q_b = lax.dynamic_slice_in_dim(qp, start, qb, 1)
        i_b = lax.dynamic_slice_in_dim(ip, start, qb, 1)
        ok_b = lax.dynamic_slice_in_dim(okp, start, qb, 1)
        qpos = pos0 + start + jnp.arange(qb)
        own_blk = jnp.minimum(qpos // MOBA_BLOCK, nblk - 1)
        g_sel = kvb[bi, gi, i_b]
        g_own = kvb[bo, go, own_blk[None, :, None]]
        s_sel = jnp.einsum('bqgjd,bqgjkld->bqgjkl', q_b, g_sel[..., 0, :]).reshape(B, qb, MOBA_KV_HEADS, MOBA_GQ, n_sel)
        s_own = jnp.einsum('bqgjd,bqgld->bqgjl', q_b, g_own[..., 0, :])
        m_sel = jnp.broadcast_to(ok_b[..., None], ok_b.shape + (MOBA_BLOCK,)).reshape(B, qb, MOBA_KV_HEADS, MOBA_GQ, n_sel)
        own_pos = own_blk[:, None] * MOBA_BLOCK + jnp.arange(MOBA_BLOCK)
        m_own = jnp.broadcast_to((own_pos <= qpos[:, None])[None, :, None, None, :], (B, qb, MOBA_KV_HEADS, MOBA_GQ, MOBA_BLOCK))
        p = masked_softmax(jnp.concatenate([s_sel, s_own], axis=-1) * ATTN_SCALE,
                           jnp.concatenate([m_sel, m_own], axis=-1))
        v_sel = g_sel[..., 1, :].reshape(B, qb, MOBA_KV_HEADS, MOBA_GQ, n_sel, HEAD_DIM)
        return (jnp.einsum('bqgjm,bqgjmd->bqgjd', p[..., :n_sel], v_sel)
                + jnp.einsum('bqgjl,bqgld->bqgjd', p[..., n_sel:], g_own[..., 1, :]))

    o = run_blocks(blk, nb, qb, T)
    return o.reshape(B, T, MOBA_HEADS * HEAD_DIM).astype(qr.dtype)


def peer_ffn(x_res, g_norm, wq, subkeys, u, v):
    B, T, D = x_res.shape
    n = B * T
    q_all = norm_matmul(x_res.reshape(n, D), wq, g_norm)
    x = rmsnorm(x_res, g_norm)
    tb = min(PEER_TBLOCK, n)
    nb = -(-n // tb)
    xt = jnp.pad(x.reshape(n, D), ((0, nb * tb - n), (0, 0)))
    qt = jnp.pad(q_all, ((0, nb * tb - n), (0, 0)))
    half = PEER_KEY_DIM // 2
    kk = PEER_TOPK * PEER_TOPK

    def block(i):
        xb = lax.dynamic_slice_in_dim(xt, i * tb, tb, 0)
        q = lax.dynamic_slice_in_dim(qt, i * tb, tb, 0).reshape(tb, PEER_HEADS, 2, half)
        s = jnp.einsum('thcd,ckd->thck', q, subkeys).astype(jnp.float32)
        s1, i1 = lax.top_k(s[:, :, 0], PEER_TOPK)
        s2, i2 = lax.top_k(s[:, :, 1], PEER_TOPK)
        cand_s = (s1[..., :, None] + s2[..., None, :]).reshape(tb, PEER_HEADS, kk)
        cand_i = (i1[..., :, None] * PEER_N_KEYS + i2[..., None, :]).reshape(tb, PEER_HEADS, kk)
        top_s, pick = lax.top_k(cand_s, PEER_TOPK)
        expert = jnp.take_along_axis(cand_i, pick, axis=-1)
        gate = jax.nn.softmax(top_s, axis=-1)
        act = jax.nn.gelu(jnp.einsum('thkd,td->thk', u[expert], xb).astype(jnp.float32))
        return jnp.einsum('thk,thkd->td', (gate * act).astype(v.dtype), v[expert])

    y = lax.map(block, jnp.arange(nb))
    return y.reshape(nb * tb, D)[:n].reshape(B, T, D).astype(x.dtype)


def kernel(x_prompt, x_sample, cache_nsa_cmp_kv, cache_nsa_sel_kv, state_nsa_win_kv, state_ssm, state_conv,
           cache_moba_kv, page_table, norm_mix, norm_ffn, norm_final, w_in_even, w_out_even,
           ssm_conv_w, ssm_conv_b, ssm_dt_bias, ssm_a_log, ssm_d_skip, ssm_norm,
           nsa_cmp_pos, nsa_cmp_w1, nsa_cmp_b1, nsa_cmp_w2, nsa_cmp_b2, nsa_gate_b,
           w_in_odd, w_out_odd, peer_wq, peer_subkeys, peer_u, peer_v):

    def trunk(x, pos0, get_past):
        B, T, _ = x.shape
        pos_q = pos0 + jnp.arange(T, dtype=jnp.int32)
        new_cmp, new_sel, new_win, new_ssm, new_conv, new_moba = [], [], [], [], [], []
        for l in range(DEPTH):
            x2 = x.reshape(B * T, D_MODEL)
            if l % 2 == 0:
                e = l // 2
                proj = norm_matmul(x2, w_in_even[e], norm_mix[l]).reshape(B, T, -1)
                z, xbc, dt_raw, q, kvc, kvs, kvw, gts = split_cols(proj, EVEN_WIDTHS)
                y_ssm, conv_new, ssm_new = mamba2_ssd(z, xbc, dt_raw, get_past('conv', e), get_past('ssm', e),
                                                      ssm_conv_w[e], ssm_conv_b[e], ssm_dt_bias[e], ssm_a_log[e],
                                                      ssm_d_skip[e], ssm_norm[e])
                kv_shape = (B, T, 2, NSA_KV_HEADS, HEAD_DIM)
                kvc = kvc.reshape(kv_shape)
                kvs = rope_keys(kvs.reshape(kv_shape), pos_q)
                kvw = rope_keys(kvw.reshape(kv_shape), pos_q)
                kvw_ext = jnp.concatenate([get_past('win', e), kvw], axis=1)
                y_nsa = nsa_attention(q.reshape(B, T, NSA_HEADS, HEAD_DIM),
                                      jnp.concatenate([get_past('cmp', e), kvc], axis=1),
                                      jnp.concatenate([get_past('sel', e), kvs], axis=1),
                                      kvw_ext, gts + nsa_gate_b[e], pos0,
                                      nsa_cmp_pos[e], nsa_cmp_w1[e], nsa_cmp_b1[e], nsa_cmp_w2[e], nsa_cmp_b2[e])
                cat = jnp.concatenate([y_ssm, y_nsa], axis=-1).reshape(B * T, -1)
                mix = norm_matmul(cat, w_out_even[e]).reshape(B, T, D_MODEL)
                keep = min(NSA_WINDOW, kvw_ext.shape[1])
                new_cmp.append(kvc)
                new_sel.append(kvs)
                new_win.append(kvw_ext[:, kvw_ext.shape[1] - keep:])
                new_ssm.append(ssm_new)
                new_conv.append(conv_new)
            else:
                o = l // 2
                proj = norm_matmul(x2, w_in_odd[o], norm_mix[l]).reshape(B, T, -1)
                q, kv = split_cols(proj, ODD_WIDTHS)
                q = partial_rope(q.reshape(B, T, MOBA_HEADS, HEAD_DIM), pos_q)
                kv = rope_keys(kv.reshape(B, T, 2, MOBA_KV_HEADS, HEAD_DIM), pos_q)
                y_moba = moba_attention(q, jnp.concatenate([get_past('moba', o), kv], axis=1), pos0)
                mix = norm_matmul(y_moba.reshape(B * T, -1), w_out_odd[o]).reshape(B, T, D_MODEL)
                new_moba.append(kv)
            x = x + mix
            x = x + peer_ffn(x, norm_ffn[l], peer_wq[l], peer_subkeys[l], peer_u[l], peer_v[l])
        return (rmsnorm(x, norm_final), jnp.stack(new_cmp), jnp.stack(new_sel), jnp.stack(new_win),
                jnp.stack(new_ssm), jnp.stack(new_conv), jnp.stack(new_moba))

    bp = x_prompt.shape[0]
    dtp = x_prompt.dtype

    def prompt_past(kind, i):
        if kind == 'ssm':
            return jnp.zeros((bp, SSM_HEADS, SSM_HEAD_DIM, SSM_STATE), state_ssm.dtype)
        if kind == 'conv':
            return jnp.zeros((bp, SSM_CONV - 1, SSM_CONV_DIM), dtp)
        if kind == 'moba':
            return jnp.zeros((bp, 0, 2, MOBA_KV_HEADS, HEAD_DIM), dtp)
        return jnp.zeros((bp, 0, 2, NSA_KV_HEADS, HEAD_DIM), dtp)

    def sample_past(kind, i):
        if kind == 'ssm':
            return state_ssm[i]
        if kind == 'conv':
            return state_conv[i]
        if kind == 'win':
            return state_nsa_win_kv[i]
        if kind == 'cmp':
            return gather_pages(cache_nsa_cmp_kv[i], page_table)
        if kind == 'sel':
            return gather_pages(cache_nsa_sel_kv[i], page_table)
        return gather_pages(cache_moba_kv[i], page_table)

    past_len = page_table.shape[1] * cache_nsa_cmp_kv.shape[2]
    y_prompt, p_cmp, p_sel, p_win, p_ssm, p_conv, p_moba = trunk(x_prompt, 0, prompt_past)
    y_sample, s_cmp, s_sel, s_win, s_ssm, s_conv, s_moba = trunk(x_sample, past_len, sample_past)
    return (y_prompt, y_sample, p_cmp, p_sel, p_win, p_ssm, p_conv, p_moba,
            s_cmp, s_sel, s_win, s_ssm, s_conv, s_moba)
```

```python
import functools
import math

import jax
import jax.numpy as jnp
from jax import lax
from jax.experimental import pallas as pl
from jax.experimental.pallas import tpu as pltpu

D_MODEL = 1024
DEPTH = 4
HEAD_DIM = 64
ROT_DIM = HEAD_DIM // 4
ROPE_THETA = 500000.0
ATTN_SCALE = HEAD_DIM ** -0.5
NORM_EPS = 1e-6
NEG_INF = -1e30

SSM_HEADS = 16
SSM_HEAD_DIM = 64
SSM_D_INNER = SSM_HEADS * SSM_HEAD_DIM
SSM_GROUPS = 2
SSM_STATE = 128
SSM_CONV = 4
SSM_CONV_DIM = SSM_D_INNER + 2 * SSM_GROUPS * SSM_STATE
SSD_CHUNK = 128

NSA_HEADS = 16
NSA_KV_HEADS = 4
NSA_GQ = NSA_HEADS // NSA_KV_HEADS
NSA_CMP_BLOCK = 32
NSA_CMP_STRIDE = 16
NSA_SEL_BLOCK = 64
NSA_TOPN = 8
NSA_LOCAL_BLOCKS = 2
NSA_FORCE = 1e6
NSA_WINDOW = 512

MOBA_HEADS = 16
MOBA_KV_HEADS = 4
MOBA_GQ = MOBA_HEADS // MOBA_KV_HEADS
MOBA_BLOCK = 256
MOBA_TOPK = 3

PEER_HEADS = 8
PEER_N_KEYS = 128
PEER_KEY_DIM = 256
PEER_TOPK = 16

EVEN_WIDTHS = (SSM_D_INNER, SSM_CONV_DIM, SSM_HEADS, NSA_HEADS * HEAD_DIM,
               2 * NSA_KV_HEADS * HEAD_DIM, 2 * NSA_KV_HEADS * HEAD_DIM, 2 * NSA_KV_HEADS * HEAD_DIM,
               3 * NSA_HEADS)
ODD_WIDTHS = (MOBA_HEADS * HEAD_DIM, 2 * MOBA_KV_HEADS * HEAD_DIM)

WIN_QBLOCK = 128
SEL_QBLOCK = 16
MOBA_QBLOCK = 4
PEER_TBLOCK = 128

VMEM_LIMIT_BYTES = 48 * 1024 * 1024


def _mm_kernel(x_ref, g_ref, w_ref, o_ref, *, normalize):
    x = x_ref[...]
    if normalize:
        x = x * lax.rsqrt(jnp.mean(x * x, axis=-1, keepdims=True) + NORM_EPS) * g_ref[...]
    o_ref[...] = jnp.dot(x.astype(jnp.bfloat16), w_ref[...], preferred_element_type=jnp.float32)


def norm_matmul(x, w, g=None):
    m, k = x.shape
    n = w.shape[1]
    tm = min(m, 256)
    assert m % tm == 0
    gg = jnp.ones((1, k), jnp.float32) if g is None else g.reshape(1, k).astype(jnp.float32)
    return pl.pallas_call(
        functools.partial(_mm_kernel, normalize=g is not None),
        grid=(m // tm,),
        in_specs=[pl.BlockSpec((tm, k), lambda i: (i, 0)),
                  pl.BlockSpec((1, k), lambda i: (0, 0)),
                  pl.BlockSpec((k, n), lambda i: (0, 0))],
        out_specs=pl.BlockSpec((tm, n), lambda i: (i, 0)),
        out_shape=jax.ShapeDtypeStruct((m, n), jnp.float32),
        compiler_params=pltpu.CompilerParams(dimension_semantics=("arbitrary",),
                                             vmem_limit_bytes=VMEM_LIMIT_BYTES),
        name="norm_matmul",
    )(x, gg, w.astype(jnp.bfloat16))


ATTN_TILE = 256


def _flash_kernel(*refs, mode, n_kblocks):
    if mode == 'sel':
        q_ref, k_ref, v_ref, sel_ref, o_ref, m_scr, l_scr, acc_scr = refs
    else:
        q_ref, k_ref, v_ref, o_ref, m_scr, l_scr, acc_scr = refs
    f32 = jnp.float32
    tq = tk = ATTN_TILE
    i = pl.program_id(2)
    rows = q_ref.shape[2]
    q = q_ref[0, 0]
    qb = (q * ATTN_SCALE).astype(jnp.bfloat16)
    m_scr[...] = jnp.full(m_scr.shape, NEG_INF, f32)
    l_scr[...] = jnp.zeros(l_scr.shape, f32)
    acc_scr[...] = jnp.zeros(acc_scr.shape, f32)
    tpos = i * tq + lax.rem(lax.broadcasted_iota(jnp.int32, (rows, 1), 0), tq)

    if mode == 'moba':
        kmean = jnp.mean(k_ref[0, 0].reshape(n_kblocks, tk, HEAD_DIM), axis=1)
        gate = lax.dot_general(q, kmean, (((1,), (1,)), ((), ())),
                               precision=lax.Precision.HIGHEST, preferred_element_type=f32)
        blk = lax.broadcasted_iota(jnp.int32, (rows, n_kblocks), 1)
        valid = blk < i
        gate = jnp.where(valid, gate, NEG_INF)
        sel = jnp.zeros((rows, n_kblocks), f32)
        for _ in range(min(MOBA_TOPK, n_kblocks)):
            mx = jnp.max(gate, axis=-1, keepdims=True)
            first = jnp.min(jnp.where(gate == mx, blk, n_kblocks), axis=-1, keepdims=True)
            hit = blk == first
            sel = jnp.where(hit, 1.0, sel)
            gate = jnp.where(hit, -3e38, gate)
        sel = jnp.where(valid, sel, 0.0)
    elif mode == 'sel':
        sel = sel_ref[0, 0]

    for n in range(n_kblocks):
        cond = (n <= i) if mode != 'win' else ((n <= i) & (n * tk + tk - 1 >= i * tq - NSA_WINDOW + 1))

        @pl.when(cond)
        def _(n=n):
            kb = k_ref[0, 0, n * tk:(n + 1) * tk, :].astype(jnp.bfloat16)
            vb = v_ref[0, 0, n * tk:(n + 1) * tk, :].astype(jnp.bfloat16)
            s = lax.dot_general(qb, kb, (((1,), (1,)), ((), ())), preferred_element_type=f32)
            kpos = n * tk + lax.broadcasted_iota(jnp.int32, (1, tk), 1)
            mask = kpos <= tpos
            if mode == 'moba':
                own = jnp.where(n == i, 1.0, 0.0)
                mask = mask & ((sel[:, n:n + 1] + own) > 0.0)
            elif mode == 'sel':
                per = tk // NSA_SEL_BLOCK
                cb = lax.broadcasted_iota(jnp.int32, (1, tk), 1) // NSA_SEL_BLOCK
                se = sel[:, n * per + per - 1:n * per + per]
                for c in range(per - 2, -1, -1):
                    se = jnp.where(cb == c, sel[:, n * per + c:n * per + c + 1], se)
                mask = mask & (se > 0.0)
            else:
                mask = mask & (tpos - kpos < NSA_WINDOW)
            s = jnp.where(mask, s, NEG_INF)
            m_old = m_scr[...]
            m_new = jnp.maximum(m_old, jnp.max(s, axis=-1, keepdims=True))
            p = jnp.where(mask, jnp.exp(s - m_new), 0.0)
            alpha = jnp.exp(m_old - m_new)
            l_scr[...] = alpha * l_scr[...] + jnp.sum(p, axis=-1, keepdims=True)
            acc_scr[...] = alpha * acc_scr[...] + jnp.dot(p.astype(jnp.bfloat16), vb, preferred_element_type=f32)
            m_scr[...] = m_new

    o_ref[0, 0] = acc_scr[...] / jnp.maximum(l_scr[...], 1e-30)


def flash_attention(q, k, v, mode, sel=None):
    B, T, KV, GQ, D = q.shape
    tq = ATTN_TILE
    assert T % tq == 0 and k.shape[1] == T
    nq = T // tq
    rows = GQ * tq
    qr = jnp.transpose(q.reshape(B, nq, tq, KV, GQ, D), (0, 3, 1, 4, 2, 5)).reshape(B, KV, nq * rows, D)
    kr = jnp.transpose(k, (0, 2, 1, 3))
    vr = jnp.transpose(v, (0, 2, 1, 3))
    args = [qr, kr, vr]
    in_specs = [pl.BlockSpec((1, 1, rows, D), lambda b, g, i: (b, g, i, 0)),
                pl.BlockSpec((1, 1, T, D), lambda b, g, i: (b, g, 0, 0)),
                pl.BlockSpec((1, 1, T, D), lambda b, g, i: (b, g, 0, 0))]
    if mode == 'sel':
        ns = sel.shape[-1]
        sr = jnp.transpose(sel.reshape(B, nq, tq, KV, ns), (0, 3, 1, 2, 4))
        sr = jnp.broadcast_to(sr[:, :, :, None], (B, KV, nq, GQ, tq, ns)).reshape(B, KV, nq * rows, ns)
        args.append(sr.astype(jnp.float32))
        in_specs.append(pl.BlockSpec((1, 1, rows, ns), lambda b, g, i: (b, g, i, 0)))
    o = pl.pallas_call(
        functools.partial(_flash_kernel, mode=mode, n_kblocks=T // ATTN_TILE),
        grid=(B, KV, nq),
        in_specs=in_specs,
        out_specs=pl.BlockSpec((1, 1, rows, D), lambda b, g, i: (b, g, i, 0)),
        out_shape=jax.ShapeDtypeStruct((B, KV, nq * rows, D), jnp.float32),
        scratch_shapes=[pltpu.VMEM((rows, 1), jnp.float32), pltpu.VMEM((rows, 1), jnp.float32),
                        pltpu.VMEM((rows, D), jnp.float32)],
        compiler_params=pltpu.CompilerParams(dimension_semantics=("arbitrary", "arbitrary", "arbitrary"),
                                             vmem_limit_bytes=VMEM_LIMIT_BYTES),
        name="flash_" + mode,
    )(*args)
    return jnp.transpose(o.reshape(B, KV, nq, GQ, tq, D), (0, 2, 4, 1, 3, 5)).reshape(B, T, KV, GQ, D)


def rmsnorm(x, g):
    xf = x.astype(jnp.float32)
    y = xf * lax.rsqrt(jnp.mean(xf * xf, axis=-1, keepdims=True) + NORM_EPS)
    return (y * g.astype(jnp.float32)).astype(x.dtype)


def split_cols(a, widths):
    outs, off = [], 0
    for w in widths:
        outs.append(a[..., off:off + w])
        off += w
    return outs


def pad_axis1(a, n):
    return jnp.pad(a, [(0, 0), (0, n - a.shape[1])] + [(0, 0)] * (a.ndim - 2))


def qblocks(T, qmax):
    qb = min(qmax, T)
    nb = -(-T // qb)
    return qb, nb, nb * qb


def run_blocks(fn, nb, qb, T):
    out = lax.map(fn, jnp.arange(nb))
    out = jnp.moveaxis(out, 0, 1)
    return out.reshape(out.shape[:1] + (nb * qb,) + out.shape[3:])[:, :T]


def masked_softmax(s, mask):
    s = jnp.where(mask, s.astype(jnp.float32), NEG_INF)
    m = jnp.max(s, axis=-1, keepdims=True)
    p = jnp.exp(s - m) * mask
    return p / jnp.maximum(jnp.sum(p, axis=-1, keepdims=True), 1e-30)


def partial_rope(x, pos):
    half = ROT_DIM // 2
    inv = ROPE_THETA ** (-jnp.arange(0, ROT_DIM, 2, dtype=jnp.float32) / ROT_DIM)
    ang = pos.astype(jnp.float32)[:, None] * inv[None, :]
    cos = jnp.cos(ang)[:, None, :]
    sin = jnp.sin(ang)[:, None, :]
    x1 = x[..., :half].astype(jnp.float32)
    x2 = x[..., half:ROT_DIM].astype(jnp.float32)
    rot = jnp.concatenate([x1 * cos - x2 * sin, x2 * cos + x1 * sin], axis=-1).astype(x.dtype)
    return jnp.concatenate([rot, x[..., ROT_DIM:]], axis=-1)


def rope_keys(kv, pos):
    return jnp.stack([partial_rope(kv[:, :, 0], pos), kv[:, :, 1]], axis=2)


def gather_pages(pool, page_table):
    g = pool[page_table]
    return g.reshape((g.shape[0], g.shape[1] * g.shape[2]) + g.shape[3:])


def ssd_chunked(x, dt, a, b_h, c_h, h0):
    B, L, H, P = x.shape
    N = b_h.shape[-1]
    f32 = jnp.float32
    Q = min(SSD_CHUNK, L)
    nc = -(-L // Q)
    Lp = nc * Q
    xdt = pad_axis1(x.astype(f32) * dt[..., None], Lp).reshape(B, nc, Q, H, P)
    da = pad_axis1(dt * a, Lp).reshape(B, nc, Q, H)
    bc = pad_axis1(b_h.astype(f32), Lp).reshape(B, nc, Q, H, N)
    cc = pad_axis1(c_h.astype(f32), Lp).reshape(B, nc, Q, H, N)
    acum = jnp.cumsum(da, axis=2)
    causal = jnp.tril(jnp.ones((Q, Q), bool))
    seg = acum[:, :, :, None, :] - acum[:, :, None, :, :]
    decay_in = jnp.exp(jnp.where(causal[None, None, :, :, None], seg, NEG_INF))
    scores = jnp.einsum('bclhn,bcshn->bclsh', cc, bc) * decay_in
    y_diag = jnp.einsum('bclsh,bcshp->bclhp', scores, xdt)
    decay_out = jnp.exp(acum[:, :, -1:] - acum)
    chunk_states = jnp.einsum('bcshn,bcshp->bchpn', bc * decay_out[..., None], xdt)
    chunk_decay = jnp.exp(acum[:, :, -1])

    def step(h, inp):
        st, dec = inp
        return h * dec[:, :, None, None] + st, h

    h_last, h_enter = lax.scan(step, h0.astype(f32),
                               (jnp.moveaxis(chunk_states, 1, 0), jnp.moveaxis(chunk_decay, 1, 0)))
    h_enter = jnp.moveaxis(h_enter, 0, 1)
    y_off = jnp.einsum('bclhn,bchpn->bclhp', cc * jnp.exp(acum)[..., None], h_enter)
    y = (y_diag + y_off).reshape(B, Lp, H, P)[:, :L]
    return y, h_last


def mamba2_ssd(z, xbc, dt_raw, conv_state, ssm_state, conv_w, conv_b, dt_bias, a_log, d_skip, norm_g):
    B, T, _ = xbc.shape
    f32 = jnp.float32
    xpad = jnp.concatenate([conv_state.astype(xbc.dtype), xbc], axis=1)
    new_conv = xpad[:, T:]
    conv = conv_b
    for k in range(SSM_CONV):
        conv = conv + xpad[:, k:k + T] * conv_w[k]
    xbc_c = jax.nn.silu(conv)
    n_bc = SSM_GROUPS * SSM_STATE
    rep = SSM_HEADS // SSM_GROUPS
    xs = xbc_c[..., :SSM_D_INNER].reshape(B, T, SSM_HEADS, SSM_HEAD_DIM)
    b_h = jnp.repeat(xbc_c[..., SSM_D_INNER:SSM_D_INNER + n_bc].reshape(B, T, SSM_GROUPS, SSM_STATE), rep, axis=2)
    c_h = jnp.repeat(xbc_c[..., SSM_D_INNER + n_bc:].reshape(B, T, SSM_GROUPS, SSM_STATE), rep, axis=2)
    dt = jax.nn.softplus(dt_raw.astype(f32) + dt_bias.astype(f32))
    a = -jnp.exp(a_log.astype(f32))
    y, h_last = ssd_chunked(xs, dt, a, b_h, c_h, ssm_state)
    y = y + d_skip.astype(f32)[:, None] * xs.astype(f32)
    y = y.reshape(B, T, SSM_D_INNER) * jax.nn.silu(z.astype(f32))
    yg = y.reshape(B, T, SSM_GROUPS, SSM_D_INNER // SSM_GROUPS)
    yg = yg * lax.rsqrt(jnp.mean(yg * yg, axis=-1, keepdims=True) + NORM_EPS)
    y = yg.reshape(B, T, SSM_D_INNER) * norm_g.astype(f32)
    return y.astype(xbc.dtype), new_conv, h_last.astype(ssm_state.dtype)


def nsa_compress(kv_all, pos_emb, w1, b1, w2, b2):
    B, L = kv_all.shape[:2]
    S = NSA_CMP_STRIDE
    r = NSA_CMP_BLOCK // S
    n_seg = L // S
    n_cmp = n_seg - r + 1
    seg = kv_all[:, :n_seg * S].reshape(B, n_seg, S, 2, NSA_KV_HEADS, HEAD_DIM)
    h = b1[None, None, :, None, :]
    for j in range(r):
        pe = jnp.transpose(pos_emb[:, j * S:(j + 1) * S], (1, 0, 2))[:, :, None, :]
        h = h + jnp.einsum('bnlcgd,cldh->bncgh', seg[:, j:j + n_cmp] + pe, w1[:, j * S:(j + 1) * S])
    h = jax.nn.gelu(h)
    return jnp.einsum('bncgh,chd->bncgd', h, w2) + b2[None, None, :, None, :]


def nsa_selected(qr, kvs_all, sel_idx, sel_ok, pos0):
    B, T = qr.shape[:2]
    Lk = kvs_all.shape[1]
    n_sel = -(-Lk // NSA_SEL_BLOCK)
    topn = sel_idx.shape[-1]
    kvb = pad_axis1(kvs_all, n_sel * NSA_SEL_BLOCK).reshape(B, n_sel, NSA_SEL_BLOCK, 2, NSA_KV_HEADS, HEAD_DIM)
    kvb = jnp.transpose(kvb, (0, 4, 1, 2, 3, 5))
    qb, nb, Tp = qblocks(T, SEL_QBLOCK)
    qp, ip, okp = pad_axis1(qr, Tp), pad_axis1(sel_idx, Tp), pad_axis1(sel_ok, Tp)
    bi = jnp.arange(B)[:, None, None, None]
    gi = jnp.arange(NSA_KV_HEADS)[None, None, :, None]
    n_keys = topn * NSA_SEL_BLOCK

    def blk(i):
        start = i * qb
        q_b = lax.dynamic_slice_in_dim(qp, start, qb, 1)
        i_b = lax.dynamic_slice_in_dim(ip, start, qb, 1)
        ok_b = lax.dynamic_slice_in_dim(okp, start, qb, 1)
        qpos = pos0 + start + jnp.arange(qb)
        g = kvb[bi, gi, i_b]
        kpos = i_b[..., None] * NSA_SEL_BLOCK + jnp.arange(NSA_SEL_BLOCK)
        mask = (ok_b[..., None] & (kpos <= qpos[None, :, None, None, None])).reshape(B, qb, NSA_KV_HEADS, 1, n_keys)
        kk = g[..., 0, :].reshape(B, qb, NSA_KV_HEADS, n_keys, HEAD_DIM)
        vv = g[..., 1, :].reshape(B, qb, NSA_KV_HEADS, n_keys, HEAD_DIM)
        p = masked_softmax(jnp.einsum('bqgjd,bqgkd->bqgjk', q_b, kk) * ATTN_SCALE, mask)
        return jnp.einsum('bqgjk,bqgkd->bqgjd', p, vv)

    return run_blocks(blk, nb, qb, T)


def nsa_window(qr, kvw_ext, pos0):
    B, T = qr.shape[:2]
    Wb = kvw_ext.shape[1] - T
    W = NSA_WINDOW
    qb, nb, Tp = qblocks(T, WIN_QBLOCK)
    qp = pad_axis1(qr, Tp)
    kvp = jnp.pad(kvw_ext, [(0, 0), (W, Tp - T), (0, 0), (0, 0), (0, 0)])
    n_kp = W + Wb + Tp
    idx = jnp.arange(n_kp)
    kpos = (pos0 - Wb - W) + idx
    kval = (idx >= W) & (idx < W + Wb + T)

    def blk(i):
        start = i * qb
        q_b = lax.dynamic_slice_in_dim(qp, start, qb, 1)
        kv_b = lax.dynamic_slice_in_dim(kvp, start + Wb, W + qb, 1)
        kp_b = lax.dynamic_slice_in_dim(kpos, start + Wb, W + qb, 0)
        ok_b = lax.dynamic_slice_in_dim(kval, start + Wb, W + qb, 0)
        qpos = pos0 + start + jnp.arange(qb)
        mask = ok_b[None, :] & (kp_b[None, :] <= qpos[:, None]) & (qpos[:, None] - kp_b[None, :] < W)
        s = jnp.einsum('bqgjd,bkgd->bqgjk', q_b, kv_b[:, :, 0]) * ATTN_SCALE
        p = masked_softmax(s, mask[None, :, None, None, :])
        return jnp.einsum('bqgjk,bkgd->bqgjd', p, kv_b[:, :, 1])

    return run_blocks(blk, nb, qb, T)


def nsa_attention(q, kvc_all, kvs_all, kvw_ext, gates, pos0, cmp_pos, cmp_w1, cmp_b1, cmp_w2, cmp_b2, prompt=False):
    B, T = q.shape[:2]
    pos_q = pos0 + jnp.arange(T, dtype=jnp.int32)
    qg = q.reshape(B, T, NSA_KV_HEADS, NSA_GQ, HEAD_DIM)
    qr = partial_rope(q, pos_q).reshape(B, T, NSA_KV_HEADS, NSA_GQ, HEAD_DIM)
    cmp = nsa_compress(kvc_all, cmp_pos, cmp_w1, cmp_b1, cmp_w2, cmp_b2)
    n_cmp = cmp.shape[1]
    cmp_start = NSA_CMP_STRIDE * jnp.arange(n_cmp, dtype=jnp.int32)
    cmp_mask = (cmp_start + NSA_CMP_BLOCK - 1)[None, :] <= pos_q[:, None]
    s = jnp.einsum('btgjd,bngd->btgjn', qg, cmp[:, :, 0]) * ATTN_SCALE
    p_cmp = masked_softmax(s, cmp_mask[None, :, None, None, :])
    o_cmp = jnp.einsum('btgjn,bngd->btgjd', p_cmp, cmp[:, :, 1])
    n_sel = -(-kvs_all.shape[1] // NSA_SEL_BLOCK)
    sel_start = NSA_SEL_BLOCK * jnp.arange(n_sel, dtype=jnp.int32)
    overlap = ((cmp_start[:, None] < sel_start[None, :] + NSA_SEL_BLOCK)
               & (cmp_start[:, None] + NSA_CMP_BLOCK > sel_start[None, :])).astype(jnp.float32)
    imp = jnp.einsum('btgjn,ns->btgs', p_cmp, overlap)
    blk_t = pos_q // NSA_SEL_BLOCK
    sidx = jnp.arange(n_sel, dtype=jnp.int32)
    valid = sidx[None, :] <= blk_t[:, None]
    forced = valid & ((sidx[None, :] == 0) | (sidx[None, :] > blk_t[:, None] - NSA_LOCAL_BLOCKS))
    imp = jnp.where(forced[None, :, None, :], NSA_FORCE, imp)
    imp = jnp.where(valid[None, :, None, :], imp, NEG_INF)
    sel_score, sel_idx = lax.top_k(imp, min(NSA_TOPN, n_sel))
    if prompt:
        sel01 = jnp.sum(jax.nn.one_hot(sel_idx, n_sel, dtype=jnp.float32)
                        * (sel_score > 0.5 * NEG_INF)[..., None], axis=-2)
        o_sel = flash_attention(qr, kvs_all[:, :, 0], kvs_all[:, :, 1], 'sel', sel01)
        o_win = flash_attention(qr, kvw_ext[:, :, 0], kvw_ext[:, :, 1], 'win')
    else:
        o_sel = nsa_selected(qr, kvs_all, sel_idx, sel_score > 0.5 * NEG_INF, pos0)
        o_win = nsa_window(qr, kvw_ext, pos0)
    g = jax.nn.sigmoid(gates.astype(jnp.float32)).reshape(B, T, NSA_KV_HEADS, NSA_GQ, 3)
    o = g[..., 0:1] * o_cmp + g[..., 1:2] * o_sel + g[..., 2:3] * o_win
    return o.reshape(B, T, NSA_HEADS * HEAD_DIM).astype(q.dtype)


def moba_attention(qr, kv_all, pos0, prompt=False):
    B, T = qr.shape[:2]
    if prompt:
        o = flash_attention(qr.reshape(B, T, MOBA_KV_HEADS, MOBA_GQ, HEAD_DIM), kv_all[:, :, 0], kv_all[:, :, 1], 'moba')
        return o.reshape(B, T, MOBA_HEADS * HEAD_DIM)
    Lk = kv_all.shape[1]
    nblk = -(-Lk // MOBA_BLOCK)
    kvb = pad_axis1(kv_all, nblk * MOBA_BLOCK).reshape(B, nblk, MOBA_BLOCK, 2, MOBA_KV_HEADS, HEAD_DIM)
    kvb = jnp.transpose(kvb, (0, 4, 1, 2, 3, 5))
    kmean = jnp.mean(kvb[..., 0, :].astype(jnp.float32), axis=3)
    pos_q = pos0 + jnp.arange(T, dtype=jnp.int32)
    qg = qr.reshape(B, T, MOBA_KV_HEADS, MOBA_GQ, HEAD_DIM)
    gate = jnp.einsum('btgjd,bgnd->btgjn', qg.astype(jnp.float32), kmean)
    past_ok = jnp.arange(nblk)[None, :] < (pos_q // MOBA_BLOCK)[:, None]
    gate = jnp.where(past_ok[None, :, None, None, :], gate, NEG_INF)
    k = min(MOBA_TOPK, nblk)
    sc, idx = lax.top_k(gate, k)
    ok = sc > 0.5 * NEG_INF
    qb, nb, Tp = qblocks(T, MOBA_QBLOCK)
    qp, ip, okp = pad_axis1(qg, Tp), pad_axis1(idx, Tp), pad_axis1(ok, Tp)
    bi = jnp.arange(B)[:, None, None, None, None]
    gi = jnp.arange(MOBA_KV_HEADS)[None, None, :, None, None]
    bo = jnp.arange(B)[:, None, None]
    go = jnp.arange(MOBA_KV_HEADS)[None, None, :]
    n_sel = k * MOBA_BLOCK

    def blk(i):
        start = i * qb
        q_b = lax.dynamic_slice_in_dim(qp, start, qb, 1)
        i_b = lax.dynamic_slice_in_dim(ip, start, qb, 1)
        ok_b = lax.dynamic_slice_in_dim(okp, start, qb, 1)
        qpos = pos0 + start + jnp.arange(qb)
        own_blk = jnp.minimum(qpos // MOBA_BLOCK, nblk - 1)
        g_sel = kvb[bi, gi, i_b]
        g_own = kvb[bo, go, own_blk[None, :, None]]
        s_sel = jnp.einsum('bqgjd,bqgjkld->bqgjkl', q_b, g_sel[..., 0, :]).reshape(B, qb, MOBA_KV_HEADS, MOBA_GQ, n_sel)
        s_own = jnp.einsum('bqgjd,bqgld->bqgjl', q_b, g_own[..., 0, :])
        m_sel = jnp.broadcast_to(ok_b[..., None], ok_b.shape + (MOBA_BLOCK,)).reshape(B, qb, MOBA_KV_HEADS, MOBA_GQ, n_sel)
        own_pos = own_blk[:, None] * MOBA_BLOCK + jnp.arange(MOBA_BLOCK)
        m_own = jnp.broadcast_to((own_pos <= qpos[:, None])[None, :, None, None, :], (B, qb, MOBA_KV_HEADS, MOBA_GQ, MOBA_BLOCK))
        p = masked_softmax(jnp.concatenate([s_sel, s_own], axis=-1) * ATTN_SCALE,
                           jnp.concatenate([m_sel, m_own], axis=-1))
        v_sel = g_sel[..., 1, :].reshape(B, qb, MOBA_KV_HEADS, MOBA_GQ, n_sel, HEAD_DIM)
        return (jnp.einsum('bqgjm,bqgjmd->bqgjd', p[..., :n_sel], v_sel)
                + jnp.einsum('bqgjl,bqgld->bqgjd', p[..., n_sel:], g_own[..., 1, :]))

    o = run_blocks(blk, nb, qb, T)
    return o.reshape(B, T, MOBA_HEADS * HEAD_DIM).astype(qr.dtype)


def peer_ffn(x_res, g_norm, wq, subkeys, u, v):
    B, T, D = x_res.shape
    n = B * T
    q_all = norm_matmul(x_res.reshape(n, D), wq, g_norm)
    x = rmsnorm(x_res, g_norm)
    tb = min(PEER_TBLOCK, n)
    nb = -(-n // tb)
    xt = jnp.pad(x.reshape(n, D), ((0, nb * tb - n), (0, 0)))
    qt = jnp.pad(q_all, ((0, nb * tb - n), (0, 0)))
    half = PEER_KEY_DIM // 2
    kk = PEER_TOPK * PEER_TOPK

    def block(i):
        xb = lax.dynamic_slice_in_dim(xt, i * tb, tb, 0)
        q = lax.dynamic_slice_in_dim(qt, i * tb, tb, 0).reshape(tb, PEER_HEADS, 2, half)
        s = jnp.einsum('thcd,ckd->thck', q, subkeys).astype(jnp.float32)
        s1, i1 = lax.top_k(s[:, :, 0], PEER_TOPK)
        s2, i2 = lax.top_k(s[:, :, 1], PEER_TOPK)
        cand_s = (s1[..., :, None] + s2[..., None, :]).reshape(tb, PEER_HEADS, kk)
        cand_i = (i1[..., :, None] * PEER_N_KEYS + i2[..., None, :]).reshape(tb, PEER_HEADS, kk)
        top_s, pick = lax.top_k(cand_s, PEER_TOPK)
        expert = jnp.take_along_axis(cand_i, pick, axis=-1)
        gate = jax.nn.softmax(top_s, axis=-1)
        act = jax.nn.gelu(jnp.einsum('thkd,td->thk', u[expert], xb).astype(jnp.float32))
        return jnp.einsum('thk,thkd->td', (gate * act).astype(v.dtype), v[expert])

    y = lax.map(block, jnp.arange(nb))
    return y.reshape(nb * tb, D)[:n].reshape(B, T, D).astype(x.dtype)


def kernel(x_prompt, x_sample, cache_nsa_cmp_kv, cache_nsa_sel_kv, state_nsa_win_kv, state_ssm, state_conv,
           cache_moba_kv, page_table, norm_mix, norm_ffn, norm_final, w_in_even, w_out_even,
           ssm_conv_w, ssm_conv_b, ssm_dt_bias, ssm_a_log, ssm_d_skip, ssm_norm,
           nsa_cmp_pos, nsa_cmp_w1, nsa_cmp_b1, nsa_cmp_w2, nsa_cmp_b2, nsa_gate_b,
           w_in_odd, w_out_odd, peer_wq, peer_subkeys, peer_u, peer_v):

    def trunk(x, pos0, get_past, prompt):
        B, T, _ = x.shape
        pos_q = pos0 + jnp.arange(T, dtype=jnp.int32)
        new_cmp, new_sel, new_win, new_ssm, new_conv, new_moba = [], [], [], [], [], []
        for l in range(DEPTH):
            x2 = x.reshape(B * T, D_MODEL)
            if l % 2 == 0:
                e = l // 2
                proj = norm_matmul(x2, w_in_even[e], norm_mix[l]).reshape(B, T, -1)
                z, xbc, dt_raw, q, kvc, kvs, kvw, gts = split_cols(proj, EVEN_WIDTHS)
                y_ssm, conv_new, ssm_new = mamba2_ssd(z, xbc, dt_raw, get_past('conv', e), get_past('ssm', e),
                                                      ssm_conv_w[e], ssm_conv_b[e], ssm_dt_bias[e], ssm_a_log[e],
                                                      ssm_d_skip[e], ssm_norm[e])
                kv_shape = (B, T, 2, NSA_KV_HEADS, HEAD_DIM)
                kvc = kvc.reshape(kv_shape)
                kvs = rope_keys(kvs.reshape(kv_shape), pos_q)
                kvw = rope_keys(kvw.reshape(kv_shape), pos_q)
                kvw_ext = jnp.concatenate([get_past('win', e), kvw], axis=1)
                y_nsa = nsa_attention(q.reshape(B, T, NSA_HEADS, HEAD_DIM),
                                      jnp.concatenate([get_past('cmp', e), kvc], axis=1),
                                      jnp.concatenate([get_past('sel', e), kvs], axis=1),
                                      kvw_ext, gts + nsa_gate_b[e], pos0,
                                      nsa_cmp_pos[e], nsa_cmp_w1[e], nsa_cmp_b1[e], nsa_cmp_w2[e], nsa_cmp_b2[e],
                                      prompt=prompt)
                cat = jnp.concatenate([y_ssm, y_nsa], axis=-1).reshape(B * T, -1)
                mix = norm_matmul(cat, w_out_even[e]).reshape(B, T, D_MODEL)
                keep = min(NSA_WINDOW, kvw_ext.shape[1])
                new_cmp.append(kvc)
                new_sel.append(kvs)
                new_win.append(kvw_ext[:, kvw_ext.shape[1] - keep:])
                new_ssm.append(ssm_new)
                new_conv.append(conv_new)
            else:
                o = l // 2
                proj = norm_matmul(x2, w_in_odd[o], norm_mix[l]).reshape(B, T, -1)
                q, kv = split_cols(proj, ODD_WIDTHS)
                q = partial_rope(q.reshape(B, T, MOBA_HEADS, HEAD_DIM), pos_q)
                kv = rope_keys(kv.reshape(B, T, 2, MOBA_KV_HEADS, HEAD_DIM), pos_q)
                y_moba = moba_attention(q, jnp.concatenate([get_past('moba', o), kv], axis=1), pos0, prompt=prompt)
                mix = norm_matmul(y_moba.reshape(B * T, -1), w_out_odd[o]).reshape(B, T, D_MODEL)
                new_moba.append(kv)
            x = x + mix
            x = x + peer_ffn(x, norm_ffn[l], peer_wq[l], peer_subkeys[l], peer_u[l], peer_v[l])
        return (rmsnorm(x, norm_final), jnp.stack(new_cmp), jnp.stack(new_sel), jnp.stack(new_win),
                jnp.stack(new_ssm), jnp.stack(new_conv), jnp.stack(new_moba))

    bp = x_prompt.shape[0]
    dtp = x_prompt.dtype

    def prompt_past(kind, i):
        if kind == 'ssm':
            return jnp.zeros((bp, SSM_HEADS, SSM_HEAD_DIM, SSM_STATE), state_ssm.dtype)
        if kind == 'conv':
            return jnp.zeros((bp, SSM_CONV - 1, SSM_CONV_DIM), dtp)
        if kind == 'moba':
            return jnp.zeros((bp, 0, 2, MOBA_KV_HEADS, HEAD_DIM), dtp)
        return jnp.zeros((bp, 0, 2, NSA_KV_HEADS, HEAD_DIM), dtp)

    def sample_past(kind, i):
        if kind == 'ssm':
            return state_ssm[i]
        if kind == 'conv':
            return state_conv[i]
        if kind == 'win':
            return state_nsa_win_kv[i]
        if kind == 'cmp':
            return gather_pages(cache_nsa_cmp_kv[i], page_table)
        if kind == 'sel':
            return gather_pages(cache_nsa_sel_kv[i], page_table)
        return gather_pages(cache_moba_kv[i], page_table)

    past_len = page_table.shape[1] * cache_nsa_cmp_kv.shape[2]
    y_prompt, p_cmp, p_sel, p_win, p_ssm, p_conv, p_moba = trunk(x_prompt, 0, prompt_past, True)
    y_sample, s_cmp, s_sel, s_win, s_ssm, s_conv, s_moba = trunk(x_sample, past_len, sample_past, False)
    return (y_prompt, y_sample, p_cmp, p_sel, p_win, p_ssm, p_conv, p_moba,
            s_cmp, s_sel, s_win, s_ssm, s_conv, s_moba)
```

```python
import functools
import math

import jax
import jax.numpy as jnp
from jax import lax
from jax.experimental import pallas as pl
from jax.experimental.pallas import tpu as pltpu

D_MODEL = 1024
DEPTH = 4
HEAD_DIM = 64
ROT_DIM = HEAD_DIM // 4
ROPE_THETA = 500000.0
ATTN_SCALE = HEAD_DIM ** -0.5
NORM_EPS = 1e-6
NEG_INF = -1e30

SSM_HEADS = 16
SSM_HEAD_DIM = 64
SSM_D_INNER = SSM_HEADS * SSM_HEAD_DIM
SSM_GROUPS = 2
SSM_STATE = 128
SSM_CONV = 4
SSM_CONV_DIM = SSM_D_INNER + 2 * SSM_GROUPS * SSM_STATE
SSD_CHUNK = 128

NSA_HEADS = 16
NSA_KV_HEADS = 4
NSA_GQ = NSA_HEADS // NSA_KV_HEADS
NSA_CMP_BLOCK = 32
NSA_CMP_STRIDE = 16
NSA_SEL_BLOCK = 64
NSA_TOPN = 8
NSA_LOCAL_BLOCKS = 2
NSA_FORCE = 1e6
NSA_WINDOW = 512

MOBA_HEADS = 16
MOBA_KV_HEADS = 4
MOBA_GQ = MOBA_HEADS // MOBA_KV_HEADS
MOBA_BLOCK = 256
MOBA_TOPK = 3

PEER_HEADS = 8
PEER_N_KEYS = 128
PEER_KEY_DIM = 256
PEER_TOPK = 16

EVEN_WIDTHS = (SSM_D_INNER, SSM_CONV_DIM, SSM_HEADS, NSA_HEADS * HEAD_DIM,
               2 * NSA_KV_HEADS * HEAD_DIM, 2 * NSA_KV_HEADS * HEAD_DIM, 2 * NSA_KV_HEADS * HEAD_DIM,
               3 * NSA_HEADS)
ODD_WIDTHS = (MOBA_HEADS * HEAD_DIM, 2 * MOBA_KV_HEADS * HEAD_DIM)

WIN_QBLOCK = 128
SEL_QBLOCK = 16
MOBA_QBLOCK = 4
PEER_TBLOCK = 128

VMEM_LIMIT_BYTES = 48 * 1024 * 1024


def _mm_kernel(x_ref, g_ref, w_ref, o_ref, *, normalize):
    x = x_ref[...]
    if normalize:
        x = x * lax.rsqrt(jnp.mean(x * x, axis=-1, keepdims=True) + NORM_EPS) * g_ref[...]
    o_ref[...] = jnp.dot(x.astype(jnp.bfloat16), w_ref[...], preferred_element_type=jnp.float32)


def norm_matmul(x, w, g=None):
    m, k = x.shape
    n = w.shape[1]
    tm = min(m, 256)
    assert m % tm == 0
    gg = jnp.ones((1, k), jnp.float32) if g is None else g.reshape(1, k).astype(jnp.float32)
    return pl.pallas_call(
        functools.partial(_mm_kernel, normalize=g is not None),
        grid=(m // tm,),
        in_specs=[pl.BlockSpec((tm, k), lambda i: (i, 0)),
                  pl.BlockSpec((1, k), lambda i: (0, 0)),
                  pl.BlockSpec((k, n), lambda i: (0, 0))],
        out_specs=pl.BlockSpec((tm, n), lambda i: (i, 0)),
        out_shape=jax.ShapeDtypeStruct((m, n), jnp.float32),
        compiler_params=pltpu.CompilerParams(dimension_semantics=("arbitrary",),
                                             vmem_limit_bytes=VMEM_LIMIT_BYTES),
        name="norm_matmul",
    )(x, gg, w.astype(jnp.bfloat16))


ATTN_TILE = 256


def _flash_kernel(*refs, mode, n_kblocks):
    if mode == 'sel':
        q_ref, k_ref, v_ref, sel_ref, o_ref, m_scr, l_scr, acc_scr = refs
    else:
        q_ref, k_ref, v_ref, o_ref, m_scr, l_scr, acc_scr = refs
    f32 = jnp.float32
    tq = tk = ATTN_TILE
    i = pl.program_id(2)
    rows = q_ref.shape[2]
    q = q_ref[0, 0]
    qb = (q * ATTN_SCALE).astype(jnp.bfloat16)
    m_scr[...] = jnp.full(m_scr.shape, NEG_INF, f32)
    l_scr[...] = jnp.zeros(l_scr.shape, f32)
    acc_scr[...] = jnp.zeros(acc_scr.shape, f32)
    tpos = i * tq + lax.rem(lax.broadcasted_iota(jnp.int32, (rows, 1), 0), tq)

    if mode == 'moba':
        kmean = jnp.mean(k_ref[0, 0].reshape(n_kblocks, tk, HEAD_DIM), axis=1)
        gate = lax.dot_general(q, kmean, (((1,), (1,)), ((), ())),
                               precision=lax.Precision.HIGHEST, preferred_element_type=f32)
        blk = lax.broadcasted_iota(jnp.int32, (rows, n_kblocks), 1)
        valid = blk < i
        gate = jnp.where(valid, gate, NEG_INF)
        sel = jnp.zeros((rows, n_kblocks), f32)
        for _ in range(min(MOBA_TOPK, n_kblocks)):
            mx = jnp.max(gate, axis=-1, keepdims=True)
            first = jnp.min(jnp.where(gate == mx, blk, n_kblocks), axis=-1, keepdims=True)
            hit = blk == first
            sel = jnp.where(hit, 1.0, sel)
            gate = jnp.where(hit, -3e38, gate)
        sel = jnp.where(valid, sel, 0.0)
    elif mode == 'sel':
        sel = sel_ref[0, 0]

    for n in range(n_kblocks):
        cond = (n <= i) if mode != 'win' else ((n <= i) & (n * tk + tk - 1 >= i * tq - NSA_WINDOW + 1))

        @pl.when(cond)
        def _(n=n):
            kb = k_ref[0, 0, n * tk:(n + 1) * tk, :].astype(jnp.bfloat16)
            vb = v_ref[0, 0, n * tk:(n + 1) * tk, :].astype(jnp.bfloat16)
            s = lax.dot_general(qb, kb, (((1,), (1,)), ((), ())), preferred_element_type=f32)
            kpos = n * tk + lax.broadcasted_iota(jnp.int32, (1, tk), 1)
            mask = kpos <= tpos
            if mode == 'moba':
                own = jnp.where(n == i, 1.0, 0.0)
                mask = mask & ((sel[:, n:n + 1] + own) > 0.0)
            elif mode == 'sel':
                per = tk // NSA_SEL_BLOCK
                cb = lax.broadcasted_iota(jnp.int32, (1, tk), 1) // NSA_SEL_BLOCK
                se = sel[:, n * per + per - 1:n * per + per]
                for c in range(per - 2, -1, -1):
                    se = jnp.where(cb == c, sel[:, n * per + c:n * per + c + 1], se)
                mask = mask & (se > 0.0)
            else:
                mask = mask & (tpos - kpos < NSA_WINDOW)
            s = jnp.where(mask, s, NEG_INF)
            m_old = m_scr[...]
            m_new = jnp.maximum(m_old, jnp.max(s, axis=-1, keepdims=True))
            p = jnp.where(mask, jnp.exp(s - m_new), 0.0)
            alpha = jnp.exp(m_old - m_new)
            l_scr[...] = alpha * l_scr[...] + jnp.sum(p, axis=-1, keepdims=True)
            acc_scr[...] = alpha * acc_scr[...] + jnp.dot(p.astype(jnp.bfloat16), vb, preferred_element_type=f32)
            m_scr[...] = m_new

    o_ref[0, 0] = acc_scr[...] / jnp.maximum(l_scr[...], 1e-30)


def flash_attention(q, k, v, mode, sel=None):
    B, T, KV, GQ, D = q.shape
    tq = ATTN_TILE
    assert T % tq == 0 and k.shape[1] == T
    nq = T // tq
    rows = GQ * tq
    qr = jnp.transpose(q.reshape(B, nq, tq, KV, GQ, D), (0, 3, 1, 4, 2, 5)).reshape(B, KV, nq * rows, D)
    kr = jnp.transpose(k, (0, 2, 1, 3))
    vr = jnp.transpose(v, (0, 2, 1, 3))
    args = [qr, kr, vr]
    in_specs = [pl.BlockSpec((1, 1, rows, D), lambda b, g, i: (b, g, i, 0)),
                pl.BlockSpec((1, 1, T, D), lambda b, g, i: (b, g, 0, 0)),
                pl.BlockSpec((1, 1, T, D), lambda b, g, i: (b, g, 0, 0))]
    if mode == 'sel':
        ns = sel.shape[-1]
        sr = jnp.transpose(sel.reshape(B, nq, tq, KV, ns), (0, 3, 1, 2, 4))
        sr = jnp.broadcast_to(sr[:, :, :, None], (B, KV, nq, GQ, tq, ns)).reshape(B, KV, nq * rows, ns)
        args.append(sr.astype(jnp.float32))
        in_specs.append(pl.BlockSpec((1, 1, rows, ns), lambda b, g, i: (b, g, i, 0)))
    o = pl.pallas_call(
        functools.partial(_flash_kernel, mode=mode, n_kblocks=T // ATTN_TILE),
        grid=(B, KV, nq),
        in_specs=in_specs,
        out_specs=pl.BlockSpec((1, 1, rows, D), lambda b, g, i: (b, g, i, 0)),
        out_shape=jax.ShapeDtypeStruct((B, KV, nq * rows, D), jnp.float32),
        scratch_shapes=[pltpu.VMEM((rows, 1), jnp.float32), pltpu.VMEM((rows, 1), jnp.float32),
                        pltpu.VMEM((rows, D), jnp.float32)],
        compiler_params=pltpu.CompilerParams(dimension_semantics=("arbitrary", "arbitrary", "arbitrary"),
                                             vmem_limit_bytes=VMEM_LIMIT_BYTES),
        name="flash_" + mode,
    )(*args)
    return jnp.transpose(o.reshape(B, KV, nq, GQ, tq, D), (0, 2, 4, 1, 3, 5)).reshape(B, T, KV, GQ, D)


PEER_N_EXPERTS = PEER_N_KEYS * PEER_N_KEYS
PEER_SCORE_TILE = 256
PEER_TOKEN_TILE = 512
PEER_EXPERT_TILE = 1024
LANES = 128
REMOVED = -3e38


def _top_desc(s, k):
    outs = []
    for r in range(k):
        m = jnp.max(s, axis=0, keepdims=True)
        outs.append(m)
        if r + 1 < k:
            s = jnp.where(s == m, REMOVED, s)
    return jnp.concatenate(outs, axis=0)


def _peer_score_kernel(x_ref, g_ref, wq_ref, sk_ref, xt_ref, s1_ref, s2_ref, e1_ref, e2_ref, tau_ref):
    f32 = jnp.float32
    x = x_ref[...]
    xn = x * lax.rsqrt(jnp.mean(x * x, axis=-1, keepdims=True) + NORM_EPS) * g_ref[...]
    xt_ref[...] = xn.T.astype(jnp.bfloat16)
    q = jnp.dot(xn.astype(jnp.bfloat16), wq_ref[...], preferred_element_type=f32)
    half = PEER_KEY_DIM // 2
    taus = []
    for h in range(PEER_HEADS):
        st = []
        for c in range(2):
            qhc = q[:, (2 * h + c) * half:(2 * h + c + 1) * half]
            st.append(lax.dot_general(sk_ref[c], qhc, (((1,), (1,)), ((), ())),
                                      precision=lax.Precision.HIGHEST, preferred_element_type=f32))
        t1 = _top_desc(st[0], PEER_TOPK)
        t2 = _top_desc(st[1], PEER_TOPK)
        cand = jnp.concatenate([t1[i:i + 1] + t2 for i in range(PEER_TOPK)], axis=0)
        tops = _top_desc(cand, PEER_TOPK)
        z = jnp.sum(jnp.exp(tops - tops[0:1]), axis=0, keepdims=True)
        taus.append(tops[PEER_TOPK - 1:PEER_TOPK])
        s1_ref[h] = st[0]
        s2_ref[h] = st[1]
        e1_ref[h] = jnp.exp(st[0] - t1[0:1]) / z
        e2_ref[h] = jnp.exp(st[1] - t2[0:1])
    tau_ref[...] = jnp.concatenate(taus, axis=0)


def _peer_dense_kernel(xt_ref, s1_ref, s2_ref, e1_ref, e2_ref, tau_ref, u_ref, vt_ref, y_ref, act_scr, w_scr):
    f32 = jnp.float32
    j = pl.program_id(1)
    tm = xt_ref.shape[1]
    n_a = PEER_EXPERT_TILE // PEER_N_KEYS

    @pl.when(j == 0)
    def _():
        y_ref[...] = jnp.zeros(y_ref.shape, f32)

    act_scr[...] = jnp.dot(u_ref[...], xt_ref[...], preferred_element_type=f32)
    a_rows = pl.ds(pl.multiple_of(j * n_a, n_a), n_a)
    for aa in range(n_a):
        for tc in range(tm // LANES):
            lanes = slice(tc * LANES, (tc + 1) * LANES)
            acc = jnp.zeros((PEER_N_KEYS, LANES), f32)
            for h in range(PEER_HEADS):
                val = s2_ref[h, :, lanes] + s1_ref[h, a_rows, lanes][aa:aa + 1]
                gate = e2_ref[h, :, lanes] * e1_ref[h, a_rows, lanes][aa:aa + 1]
                acc = acc + jnp.where(val >= tau_ref[h:h + 1, lanes], gate, 0.0)
            act = act_scr[aa * PEER_N_KEYS:(aa + 1) * PEER_N_KEYS, lanes]
            w_scr[aa * PEER_N_KEYS:(aa + 1) * PEER_N_KEYS, lanes] = (acc * jax.nn.gelu(act)).astype(jnp.bfloat16)
    y_ref[...] += jnp.dot(vt_ref[...], w_scr[...], preferred_element_type=f32)


def peer_ffn_dense(x_res, g_norm, wq_bf16, subkeys, u_bf16, vt_bf16):
    n, d = x_res.shape
    tm = PEER_TOKEN_TILE if n % PEER_TOKEN_TILE == 0 else LANES
    ts = PEER_SCORE_TILE if n % PEER_SCORE_TILE == 0 else LANES
    n_pad = -(-n // tm) * tm
    xp = jnp.pad(x_res, ((0, n_pad - n), (0, 0)))
    hk = (PEER_HEADS, PEER_N_KEYS, n_pad)
    stat_spec = lambda t: pl.BlockSpec((PEER_HEADS, PEER_N_KEYS, t), lambda i, *_: (0, 0, i))
    xt, s1, s2, e1, e2, tau = pl.pallas_call(
        _peer_score_kernel,
        grid=(n_pad // ts,),
        in_specs=[pl.BlockSpec((ts, d), lambda i: (i, 0)),
                  pl.BlockSpec((1, d), lambda i: (0, 0)),
                  pl.BlockSpec(wq_bf16.shape, lambda i: (0, 0)),
                  pl.BlockSpec(subkeys.shape, lambda i: (0, 0, 0))],
        out_specs=[pl.BlockSpec((d, ts), lambda i: (0, i)), stat_spec(ts), stat_spec(ts), stat_spec(ts), stat_spec(ts),
                   pl.BlockSpec((PEER_HEADS, ts), lambda i: (0, i))],
        out_shape=[jax.ShapeDtypeStruct((d, n_pad), jnp.bfloat16)] + [jax.ShapeDtypeStruct(hk, jnp.float32)] * 4
                  + [jax.ShapeDtypeStruct((PEER_HEADS, n_pad), jnp.float32)],
        compiler_params=pltpu.CompilerParams(dimension_semantics=("arbitrary",),
                                             vmem_limit_bytes=VMEM_LIMIT_BYTES),
        name="peer_score",
    )(xp, g_norm.reshape(1, d).astype(jnp.float32), wq_bf16, subkeys.astype(jnp.float32))
    te = PEER_EXPERT_TILE
    yt = pl.pallas_call(
        _peer_dense_kernel,
        grid=(n_pad // tm, PEER_N_EXPERTS // te),
        in_specs=[pl.BlockSpec((d, tm), lambda i, j: (0, i)),
                  stat_spec(tm), stat_spec(tm), stat_spec(tm), stat_spec(tm),
                  pl.BlockSpec((PEER_HEADS, tm), lambda i, j: (0, i)),
                  pl.BlockSpec((te, d), lambda i, j: (j, 0)),
                  pl.BlockSpec((d, te), lambda i, j: (0, j))],
        out_specs=pl.BlockSpec((d, tm), lambda i, j: (0, i)),
        out_shape=jax.ShapeDtypeStruct((d, n_pad), jnp.float32),
        scratch_shapes=[pltpu.VMEM((te, tm), jnp.float32), pltpu.VMEM((te, tm), jnp.bfloat16)],
        compiler_params=pltpu.CompilerParams(dimension_semantics=("arbitrary", "arbitrary"),
                                             vmem_limit_bytes=VMEM_LIMIT_BYTES),
        name="peer_dense",
    )(xt, s1, s2, e1, e2, tau, u_bf16, vt_bf16)
    return yt.T[:n]


def rmsnorm(x, g):
    xf = x.astype(jnp.float32)
    y = xf * lax.rsqrt(jnp.mean(xf * xf, axis=-1, keepdims=True) + NORM_EPS)
    return (y * g.astype(jnp.float32)).astype(x.dtype)


def split_cols(a, widths):
    outs, off = [], 0
    for w in widths:
        outs.append(a[..., off:off + w])
        off += w
    return outs


def pad_axis1(a, n):
    return jnp.pad(a, [(0, 0), (0, n - a.shape[1])] + [(0, 0)] * (a.ndim - 2))


def qblocks(T, qmax):
    qb = min(qmax, T)
    nb = -(-T // qb)
    return qb, nb, nb * qb


def run_blocks(fn, nb, qb, T):
    out = lax.map(fn, jnp.arange(nb))
    out = jnp.moveaxis(out, 0, 1)
    return out.reshape(out.shape[:1] + (nb * qb,) + out.shape[3:])[:, :T]


def masked_softmax(s, mask):
    s = jnp.where(mask, s.astype(jnp.float32), NEG_INF)
    m = jnp.max(s, axis=-1, keepdims=True)
    p = jnp.exp(s - m) * mask
    return p / jnp.maximum(jnp.sum(p, axis=-1, keepdims=True), 1e-30)


def partial_rope(x, pos):
    half = ROT_DIM // 2
    inv = ROPE_THETA ** (-jnp.arange(0, ROT_DIM, 2, dtype=jnp.float32) / ROT_DIM)
    ang = pos.astype(jnp.float32)[:, None] * inv[None, :]
    cos = jnp.cos(ang)[:, None, :]
    sin = jnp.sin(ang)[:, None, :]
    x1 = x[..., :half].astype(jnp.float32)
    x2 = x[..., half:ROT_DIM].astype(jnp.float32)
    rot = jnp.concatenate([x1 * cos - x2 * sin, x2 * cos + x1 * sin], axis=-1).astype(x.dtype)
    return jnp.concatenate([rot, x[..., ROT_DIM:]], axis=-1)


def rope_keys(kv, pos):
    return jnp.stack([partial_rope(kv[:, :, 0], pos), kv[:, :, 1]], axis=2)


def gather_pages(pool, page_table):
    g = pool[page_table]
    return g.reshape((g.shape[0], g.shape[1] * g.shape[2]) + g.shape[3:])


def ssd_chunked(x, dt, a, b_h, c_h, h0):
    B, L, H, P = x.shape
    N = b_h.shape[-1]
    f32 = jnp.float32
    Q = min(SSD_CHUNK, L)
    nc = -(-L // Q)
    Lp = nc * Q
    xdt = pad_axis1(x.astype(f32) * dt[..., None], Lp).reshape(B, nc, Q, H, P)
    da = pad_axis1(dt * a, Lp).reshape(B, nc, Q, H)
    bc = pad_axis1(b_h.astype(f32), Lp).reshape(B, nc, Q, H, N)
    cc = pad_axis1(c_h.astype(f32), Lp).reshape(B, nc, Q, H, N)
    acum = jnp.cumsum(da, axis=2)
    causal = jnp.tril(jnp.ones((Q, Q), bool))
    seg = acum[:, :, :, None, :] - acum[:, :, None, :, :]
    decay_in = jnp.exp(jnp.where(causal[None, None, :, :, None], seg, NEG_INF))
    scores = jnp.einsum('bclhn,bcshn->bclsh', cc, bc) * decay_in
    y_diag = jnp.einsum('bclsh,bcshp->bclhp', scores, xdt)
    decay_out = jnp.exp(acum[:, :, -1:] - acum)
    chunk_states = jnp.einsum('bcshn,bcshp->bchpn', bc * decay_out[..., None], xdt)
    chunk_decay = jnp.exp(acum[:, :, -1])

    def step(h, inp):
        st, dec = inp
        return h * dec[:, :, None, None] + st, h

    h_last, h_enter = lax.scan(step, h0.astype(f32),
                               (jnp.moveaxis(chunk_states, 1, 0), jnp.moveaxis(chunk_decay, 1, 0)))
    h_enter = jnp.moveaxis(h_enter, 0, 1)
    y_off = jnp.einsum('bclhn,bchpn->bclhp', cc * jnp.exp(acum)[..., None], h_enter)
    y = (y_diag + y_off).reshape(B, Lp, H, P)[:, :L]
    return y, h_last


def mamba2_ssd(z, xbc, dt_raw, conv_state, ssm_state, conv_w, conv_b, dt_bias, a_log, d_skip, norm_g):
    B, T, _ = xbc.shape
    f32 = jnp.float32
    xpad = jnp.concatenate([conv_state.astype(xbc.dtype), xbc], axis=1)
    new_conv = xpad[:, T:]
    conv = conv_b
    for k in range(SSM_CONV):
        conv = conv + xpad[:, k:k + T] * conv_w[k]
    xbc_c = jax.nn.silu(conv)
    n_bc = SSM_GROUPS * SSM_STATE
    rep = SSM_HEADS // SSM_GROUPS
    xs = xbc_c[..., :SSM_D_INNER].reshape(B, T, SSM_HEADS, SSM_HEAD_DIM)
    b_h = jnp.repeat(xbc_c[..., SSM_D_INNER:SSM_D_INNER + n_bc].reshape(B, T, SSM_GROUPS, SSM_STATE), rep, axis=2)
    c_h = jnp.repeat(xbc_c[..., SSM_D_INNER + n_bc:].reshape(B, T, SSM_GROUPS, SSM_STATE), rep, axis=2)
    dt = jax.nn.softplus(dt_raw.astype(f32) + dt_bias.astype(f32))
    a = -jnp.exp(a_log.astype(f32))
    y, h_last = ssd_chunked(xs, dt, a, b_h, c_h, ssm_state)
    y = y + d_skip.astype(f32)[:, None] * xs.astype(f32)
    y = y.reshape(B, T, SSM_D_INNER) * jax.nn.silu(z.astype(f32))
    yg = y.reshape(B, T, SSM_GROUPS, SSM_D_INNER // SSM_GROUPS)
    yg = yg * lax.rsqrt(jnp.mean(yg * yg, axis=-1, keepdims=True) + NORM_EPS)
    y = yg.reshape(B, T, SSM_D_INNER) * norm_g.astype(f32)
    return y.astype(xbc.dtype), new_conv, h_last.astype(ssm_state.dtype)


def nsa_compress(kv_all, pos_emb, w1, b1, w2, b2):
    B, L = kv_all.shape[:2]
    S = NSA_CMP_STRIDE
    r = NSA_CMP_BLOCK // S
    n_seg = L // S
    n_cmp = n_seg - r + 1
    seg = kv_all[:, :n_seg * S].reshape(B, n_seg, S, 2, NSA_KV_HEADS, HEAD_DIM)
    h = b1[None, None, :, None, :]
    for j in range(r):
        pe = jnp.transpose(pos_emb[:, j * S:(j + 1) * S], (1, 0, 2))[:, :, None, :]
        h = h + jnp.einsum('bnlcgd,cldh->bncgh', seg[:, j:j + n_cmp] + pe, w1[:, j * S:(j + 1) * S])
    h = jax.nn.gelu(h)
    return jnp.einsum('bncgh,chd->bncgd', h, w2) + b2[None, None, :, None, :]


def nsa_selected(qr, kvs_all, sel_idx, sel_ok, pos0):
    B, T = qr.shape[:2]
    Lk = kvs_all.shape[1]
    n_sel = -(-Lk // NSA_SEL_BLOCK)
    topn = sel_idx.shape[-1]
    kvb = pad_axis1(kvs_all, n_sel * NSA_SEL_BLOCK).reshape(B, n_sel, NSA_SEL_BLOCK, 2, NSA_KV_HEADS, HEAD_DIM)
    kvb = jnp.transpose(kvb, (0, 4, 1, 2, 3, 5))
    qb, nb, Tp = qblocks(T, SEL_QBLOCK)
    qp, ip, okp = pad_axis1(qr, Tp), pad_axis1(sel_idx, Tp), pad_axis1(sel_ok, Tp)
    bi = jnp.arange(B)[:, None, None, None]
    gi = jnp.arange(NSA_KV_HEADS)[None, None, :, None]
    n_keys = topn * NSA_SEL_BLOCK

    def blk(i):
        start = i * qb
        q_b = lax.dynamic_slice_in_dim(qp, start, qb, 1)
        i_b = lax.dynamic_slice_in_dim(ip, start, qb, 1)
        ok_b = lax.dynamic_slice_in_dim(okp, start, qb, 1)
        qpos = pos0 + start + jnp.arange(qb)
        g = kvb[bi, gi, i_b]
        kpos = i_b[..., None] * NSA_SEL_BLOCK + jnp.arange(NSA_SEL_BLOCK)
        mask = (ok_b[..., None] & (kpos <= qpos[None, :, None, None, None])).reshape(B, qb, NSA_KV_HEADS, 1, n_keys)
        kk = g[..., 0, :].reshape(B, qb, NSA_KV_HEADS, n_keys, HEAD_DIM)
        vv = g[..., 1, :].reshape(B, qb, NSA_KV_HEADS, n_keys, HEAD_DIM)
        p = masked_softmax(jnp.einsum('bqgjd,bqgkd->bqgjk', q_b, kk) * ATTN_SCALE, mask)
        return jnp.einsum('bqgjk,bqgkd->bqgjd', p, vv)

    return run_blocks(blk, nb, qb, T)


def nsa_window(qr, kvw_ext, pos0):
    B, T = qr.shape[:2]
    Wb = kvw_ext.shape[1] - T
    W = NSA_WINDOW
    qb, nb, Tp = qblocks(T, WIN_QBLOCK)
    qp = pad_axis1(qr, Tp)
    kvp = jnp.pad(kvw_ext, [(0, 0), (W, Tp - T), (0, 0), (0, 0), (0, 0)])
    n_kp = W + Wb + Tp
    idx = jnp.arange(n_kp)
    kpos = (pos0 - Wb - W) + idx
    kval = (idx >= W) & (idx < W + Wb + T)

    def blk(i):
        start = i * qb
        q_b = lax.dynamic_slice_in_dim(qp, start, qb, 1)
        kv_b = lax.dynamic_slice_in_dim(kvp, start + Wb, W + qb, 1)
        kp_b = lax.dynamic_slice_in_dim(kpos, start + Wb, W + qb, 0)
        ok_b = lax.dynamic_slice_in_dim(kval, start + Wb, W + qb, 0)
        qpos = pos0 + start + jnp.arange(qb)
        mask = ok_b[None, :] & (kp_b[None, :] <= qpos[:, None]) & (qpos[:, None] - kp_b[None, :] < W)
        s = jnp.einsum('bqgjd,bkgd->bqgjk', q_b, kv_b[:, :, 0]) * ATTN_SCALE
        p = masked_softmax(s, mask[None, :, None, None, :])
        return jnp.einsum('bqgjk,bkgd->bqgjd', p, kv_b[:, :, 1])

    return run_blocks(blk, nb, qb, T)


def nsa_attention(q, kvc_all, kvs_all, kvw_ext, gates, pos0, cmp_pos, cmp_w1, cmp_b1, cmp_w2, cmp_b2, prompt=False):
    B, T = q.shape[:2]
    pos_q = pos0 + jnp.arange(T, dtype=jnp.int32)
    qg = q.reshape(B, T, NSA_KV_HEADS, NSA_GQ, HEAD_DIM)
    qr = partial_rope(q, pos_q).reshape(B, T, NSA_KV_HEADS, NSA_GQ, HEAD_DIM)
    cmp = nsa_compress(kvc_all, cmp_pos, cmp_w1, cmp_b1, cmp_w2, cmp_b2)
    n_cmp = cmp.shape[1]
    cmp_start = NSA_CMP_STRIDE * jnp.arange(n_cmp, dtype=jnp.int32)
    cmp_mask = (cmp_start + NSA_CMP_BLOCK - 1)[None, :] <= pos_q[:, None]
    s = jnp.einsum('btgjd,bngd->btgjn', qg, cmp[:, :, 0]) * ATTN_SCALE
    p_cmp = masked_softmax(s, cmp_mask[None, :, None, None, :])
    o_cmp = jnp.einsum('btgjn,bngd->btgjd', p_cmp, cmp[:, :, 1])
    n_sel = -(-kvs_all.shape[1] // NSA_SEL_BLOCK)
    sel_start = NSA_SEL_BLOCK * jnp.arange(n_sel, dtype=jnp.int32)
    overlap = ((cmp_start[:, None] < sel_start[None, :] + NSA_SEL_BLOCK)
               & (cmp_start[:, None] + NSA_CMP_BLOCK > sel_start[None, :])).astype(jnp.float32)
    imp = jnp.einsum('btgjn,ns->btgs', p_cmp, overlap)
    blk_t = pos_q // NSA_SEL_BLOCK
    sidx = jnp.arange(n_sel, dtype=jnp.int32)
    valid = sidx[None, :] <= blk_t[:, None]
    forced = valid & ((sidx[None, :] == 0) | (sidx[None, :] > blk_t[:, None] - NSA_LOCAL_BLOCKS))
    imp = jnp.where(forced[None, :, None, :], NSA_FORCE, imp)
    imp = jnp.where(valid[None, :, None, :], imp, NEG_INF)
    sel_score, sel_idx = lax.top_k(imp, min(NSA_TOPN, n_sel))
    if prompt:
        sel01 = jnp.sum(jax.nn.one_hot(sel_idx, n_sel, dtype=jnp.float32)
                        * (sel_score > 0.5 * NEG_INF)[..., None], axis=-2)
        o_sel = flash_attention(qr, kvs_all[:, :, 0], kvs_all[:, :, 1], 'sel', sel01)
        o_win = flash_attention(qr, kvw_ext[:, :, 0], kvw_ext[:, :, 1], 'win')
    else:
        o_sel = nsa_selected(qr, kvs_all, sel_idx, sel_score > 0.5 * NEG_INF, pos0)
        o_win = nsa_window(qr, kvw_ext, pos0)
    g = jax.nn.sigmoid(gates.astype(jnp.float32)).reshape(B, T, NSA_KV_HEADS, NSA_GQ, 3)
    o = g[..., 0:1] * o_cmp + g[..., 1:2] * o_sel + g[..., 2:3] * o_win
    return o.reshape(B, T, NSA_HEADS * HEAD_DIM).astype(q.dtype)


def moba_attention(qr, kv_all, pos0, prompt=False):
    B, T = qr.shape[:2]
    if prompt:
        o = flash_attention(qr.reshape(B, T, MOBA_KV_HEADS, MOBA_GQ, HEAD_DIM), kv_all[:, :, 0], kv_all[:, :, 1], 'moba')
        return o.reshape(B, T, MOBA_HEADS * HEAD_DIM)
    Lk = kv_all.shape[1]
    nblk = -(-Lk // MOBA_BLOCK)
    kvb = pad_axis1(kv_all, nblk * MOBA_BLOCK).reshape(B, nblk, MOBA_BLOCK, 2, MOBA_KV_HEADS, HEAD_DIM)
    kvb = jnp.transpose(kvb, (0, 4, 1, 2, 3, 5))
    kmean = jnp.mean(kvb[..., 0, :].astype(jnp.float32), axis=3)
    pos_q = pos0 + jnp.arange(T, dtype=jnp.int32)
    qg = qr.reshape(B, T, MOBA_KV_HEADS, MOBA_GQ, HEAD_DIM)
    gate = jnp.einsum('btgjd,bgnd->btgjn', qg.astype(jnp.float32), kmean)
    past_ok = jnp.arange(nblk)[None, :] < (pos_q // MOBA_BLOCK)[:, None]
    gate = jnp.where(past_ok[None, :, None, None, :], gate, NEG_INF)
    k = min(MOBA_TOPK, nblk)
    sc, idx = lax.top_k(gate, k)
    ok = sc > 0.5 * NEG_INF
    qb, nb, Tp = qblocks(T, MOBA_QBLOCK)
    qp, ip, okp = pad_axis1(qg, Tp), pad_axis1(idx, Tp), pad_axis1(ok, Tp)
    bi = jnp.arange(B)[:, None, None, None, None]
    gi = jnp.arange(MOBA_KV_HEADS)[None, None, :, None, None]
    bo = jnp.arange(B)[:, None, None]
    go = jnp.arange(MOBA_KV_HEADS)[None, None, :]
    n_sel = k * MOBA_BLOCK

    def blk(i):
        start = i * qb
        q_b = lax.dynamic_slice_in_dim(qp, start, qb, 1)
        i_b = lax.dynamic_slice_in_dim(ip, start, qb, 1)
        ok_b = lax.dynamic_slice_in_dim(okp, start, qb, 1)
        qpos = pos0 + start + jnp.arange(qb)
        own_blk = jnp.minimum(qpos // MOBA_BLOCK, nblk - 1)
        g_sel = kvb[bi, gi, i_b]
        g_own = kvb[bo, go, own_blk[None, :, None]]
        s_sel = jnp.einsum('bqgjd,bqgjkld->bqgjkl', q_b, g_sel[..., 0, :]).reshape(B, qb, MOBA_KV_HEADS, MOBA_GQ, n_sel)
        s_own = jnp.einsum('bqgjd,bqgld->bqgjl', q_b, g_own[..., 0, :])
        m_sel = jnp.broadcast_to(ok_b[..., None], ok_b.shape + (MOBA_BLOCK,)).reshape(B, qb, MOBA_KV_HEADS, MOBA_GQ, n_sel)
        own_pos = own_blk[:, None] * MOBA_BLOCK + jnp.arange(MOBA_BLOCK)
        m_own = jnp.broadcast_to((own_pos <= qpos[:, None])[None, :, None, None, :], (B, qb, MOBA_KV_HEADS, MOBA_GQ, MOBA_BLOCK))
        p = masked_softmax(jnp.concatenate([s_sel, s_own], axis=-1) * ATTN_SCALE,
                           jnp.concatenate([m_sel, m_own], axis=-1))
        v_sel = g_sel[..., 1, :].reshape(B, qb, MOBA_KV_HEADS, MOBA_GQ, n_sel, HEAD_DIM)
        return (jnp.einsum('bqgjm,bqgjmd->bqgjd', p[..., :n_sel], v_sel)
                + jnp.einsum('bqgjl,bqgld->bqgjd', p[..., n_sel:], g_own[..., 1, :]))

    o = run_blocks(blk, nb, qb, T)
    return o.reshape(B, T, MOBA_HEADS * HEAD_DIM).astype(qr.dtype)


def kernel(x_prompt, x_sample, cache_nsa_cmp_kv, cache_nsa_sel_kv, state_nsa_win_kv, state_ssm, state_conv,
           cache_moba_kv, page_table, norm_mix, norm_ffn, norm_final, w_in_even, w_out_even,
           ssm_conv_w, ssm_conv_b, ssm_dt_bias, ssm_a_log, ssm_d_skip, ssm_norm,
           nsa_cmp_pos, nsa_cmp_w1, nsa_cmp_b1, nsa_cmp_w2, nsa_cmp_b2, nsa_gate_b,
           w_in_odd, w_out_odd, peer_wq, peer_subkeys, peer_u, peer_v):

    peer_wq_bf16 = peer_wq.astype(jnp.bfloat16)
    peer_u_bf16 = peer_u.astype(jnp.bfloat16)
    peer_vt_bf16 = jnp.transpose(peer_v, (0, 2, 1)).astype(jnp.bfloat16)

    def trunk(x, pos0, get_past, prompt):
        B, T, _ = x.shape
        pos_q = pos0 + jnp.arange(T, dtype=jnp.int32)
        new_cmp, new_sel, new_win, new_ssm, new_conv, new_moba = [], [], [], [], [], []
        for l in range(DEPTH):
            x2 = x.reshape(B * T, D_MODEL)
            if l % 2 == 0:
                e = l // 2
                proj = norm_matmul(x2, w_in_even[e], norm_mix[l]).reshape(B, T, -1)
                z, xbc, dt_raw, q, kvc, kvs, kvw, gts = split_cols(proj, EVEN_WIDTHS)
                y_ssm, conv_new, ssm_new = mamba2_ssd(z, xbc, dt_raw, get_past('conv', e), get_past('ssm', e),
                                                      ssm_conv_w[e], ssm_conv_b[e], ssm_dt_bias[e], ssm_a_log[e],
                                                      ssm_d_skip[e], ssm_norm[e])
                kv_shape = (B, T, 2, NSA_KV_HEADS, HEAD_DIM)
                kvc = kvc.reshape(kv_shape)
                kvs = rope_keys(kvs.reshape(kv_shape), pos_q)
                kvw = rope_keys(kvw.reshape(kv_shape), pos_q)
                kvw_ext = jnp.concatenate([get_past('win', e), kvw], axis=1)
                y_nsa = nsa_attention(q.reshape(B, T, NSA_HEADS, HEAD_DIM),
                                      jnp.concatenate([get_past('cmp', e), kvc], axis=1),
                                      jnp.concatenate([get_past('sel', e), kvs], axis=1),
                                      kvw_ext, gts + nsa_gate_b[e], pos0,
                                      nsa_cmp_pos[e], nsa_cmp_w1[e], nsa_cmp_b1[e], nsa_cmp_w2[e], nsa_cmp_b2[e],
                                      prompt=prompt)
                cat = jnp.concatenate([y_ssm, y_nsa], axis=-1).reshape(B * T, -1)
                mix = norm_matmul(cat, w_out_even[e]).reshape(B, T, D_MODEL)
                keep = min(NSA_WINDOW, kvw_ext.shape[1])
                new_cmp.append(kvc)
                new_sel.append(kvs)
                new_win.append(kvw_ext[:, kvw_ext.shape[1] - keep:])
                new_ssm.append(ssm_new)
                new_conv.append(conv_new)
            else:
                o = l // 2
                proj = norm_matmul(x2, w_in_odd[o], norm_mix[l]).reshape(B, T, -1)
                q, kv = split_cols(proj, ODD_WIDTHS)
                q = partial_rope(q.reshape(B, T, MOBA_HEADS, HEAD_DIM), pos_q)
                kv = rope_keys(kv.reshape(B, T, 2, MOBA_KV_HEADS, HEAD_DIM), pos_q)
                y_moba = moba_attention(q, jnp.concatenate([get_past('moba', o), kv], axis=1), pos0, prompt=prompt)
                mix = norm_matmul(y_moba.reshape(B * T, -1), w_out_odd[o]).reshape(B, T, D_MODEL)
                new_moba.append(kv)
            x = x + mix
            y_peer = peer_ffn_dense(x.reshape(B * T, D_MODEL), norm_ffn[l], peer_wq_bf16[l], peer_subkeys[l],
                                    peer_u_bf16[l], peer_vt_bf16[l])
            x = x + y_peer.reshape(B, T, D_MODEL)
        return (rmsnorm(x, norm_final), jnp.stack(new_cmp), jnp.stack(new_sel), jnp.stack(new_win),
                jnp.stack(new_ssm), jnp.stack(new_conv), jnp.stack(new_moba))

    bp = x_prompt.shape[0]
    dtp = x_prompt.dtype

    def prompt_past(kind, i):
        if kind == 'ssm':
            return jnp.zeros((bp, SSM_HEADS, SSM_HEAD_DIM, SSM_STATE), state_ssm.dtype)
        if kind == 'conv':
            return jnp.zeros((bp, SSM_CONV - 1, SSM_CONV_DIM), dtp)
        if kind == 'moba':
            return jnp.zeros((bp, 0, 2, MOBA_KV_HEADS, HEAD_DIM), dtp)
        return jnp.zeros((bp, 0, 2, NSA_KV_HEADS, HEAD_DIM), dtp)

    def sample_past(kind, i):
        if kind == 'ssm':
            return state_ssm[i]
        if kind == 'conv':
            return state_conv[i]
        if kind == 'win':
            return state_nsa_win_kv[i]
        if kind == 'cmp':
            return gather_pages(cache_nsa_cmp_kv[i], page_table)
        if kind == 'sel':
            return gather_pages(cache_nsa_sel_kv[i], page_table)
        return gather_pages(cache_moba_kv[i], page_table)

    past_len = page_table.shape[1] * cache_nsa_cmp_kv.shape[2]
    y_prompt, p_cmp, p_sel, p_win, p_ssm, p_conv, p_moba = trunk(x_prompt, 0, prompt_past, True)
    y_sample, s_cmp, s_sel, s_win, s_ssm, s_conv, s_moba = trunk(x_sample, past_len, sample_past, False)
    return (y_prompt, y_sample, p_cmp, p_sel, p_win, p_ssm, p_conv, p_moba,
            s_cmp, s_sel, s_win, s_ssm, s_conv, s_moba)
```

```python
import functools
import math

import jax
import jax.numpy as jnp
from jax import lax
from jax.experimental import pallas as pl
from jax.experimental.pallas import tpu as pltpu

D_MODEL = 1024
DEPTH = 4
HEAD_DIM = 64
ROT_DIM = HEAD_DIM // 4
ROPE_THETA = 500000.0
ATTN_SCALE = HEAD_DIM ** -0.5
NORM_EPS = 1e-6
NEG_INF = -1e30

SSM_HEADS = 16
SSM_HEAD_DIM = 64
SSM_D_INNER = SSM_HEADS * SSM_HEAD_DIM
SSM_GROUPS = 2
SSM_STATE = 128
SSM_CONV = 4
SSM_CONV_DIM = SSM_D_INNER + 2 * SSM_GROUPS * SSM_STATE
SSD_CHUNK = 128

NSA_HEADS = 16
NSA_KV_HEADS = 4
NSA_GQ = NSA_HEADS // NSA_KV_HEADS
NSA_CMP_BLOCK = 32
NSA_CMP_STRIDE = 16
NSA_SEL_BLOCK = 64
NSA_TOPN = 8
NSA_LOCAL_BLOCKS = 2
NSA_FORCE = 1e6
NSA_WINDOW = 512

MOBA_HEADS = 16
MOBA_KV_HEADS = 4
MOBA_GQ = MOBA_HEADS // MOBA_KV_HEADS
MOBA_BLOCK = 256
MOBA_TOPK = 3

PEER_HEADS = 8
PEER_N_KEYS = 128
PEER_KEY_DIM = 256
PEER_TOPK = 16

EVEN_WIDTHS = (SSM_D_INNER, SSM_CONV_DIM, SSM_HEADS, NSA_HEADS * HEAD_DIM,
               2 * NSA_KV_HEADS * HEAD_DIM, 2 * NSA_KV_HEADS * HEAD_DIM, 2 * NSA_KV_HEADS * HEAD_DIM,
               3 * NSA_HEADS)
ODD_WIDTHS = (MOBA_HEADS * HEAD_DIM, 2 * MOBA_KV_HEADS * HEAD_DIM)

WIN_QBLOCK = 128
SEL_QBLOCK = 16
MOBA_QBLOCK = 4
PEER_TBLOCK = 128

VMEM_LIMIT_BYTES = 48 * 1024 * 1024


def _mm_kernel(x_ref, g_ref, w_ref, o_ref, *, normalize):
    x = x_ref[...]
    if normalize:
        x = x * lax.rsqrt(jnp.mean(x * x, axis=-1, keepdims=True) + NORM_EPS) * g_ref[...]
    o_ref[...] = jnp.dot(x.astype(jnp.bfloat16), w_ref[...], preferred_element_type=jnp.float32)


def norm_matmul(x, w, g=None):
    m, k = x.shape
    n = w.shape[1]
    tm = min(m, 256)
    assert m % tm == 0
    gg = jnp.ones((1, k), jnp.float32) if g is None else g.reshape(1, k).astype(jnp.float32)
    return pl.pallas_call(
        functools.partial(_mm_kernel, normalize=g is not None),
        grid=(m // tm,),
        in_specs=[pl.BlockSpec((tm, k), lambda i: (i, 0)),
                  pl.BlockSpec((1, k), lambda i: (0, 0)),
                  pl.BlockSpec((k, n), lambda i: (0, 0))],
        out_specs=pl.BlockSpec((tm, n), lambda i: (i, 0)),
        out_shape=jax.ShapeDtypeStruct((m, n), jnp.float32),
        compiler_params=pltpu.CompilerParams(dimension_semantics=("arbitrary",),
                                             vmem_limit_bytes=VMEM_LIMIT_BYTES),
        name="norm_matmul",
    )(x, gg, w.astype(jnp.bfloat16))


LANES = 128
ROT_HALF = ROT_DIM // 2
PROJ_VMEM_LIMIT_BYTES = 56 * 1024 * 1024
N_Q = NSA_HEADS * HEAD_DIM
N_KV = 2 * NSA_KV_HEADS * HEAD_DIM
EVEN_OFF_XBC = SSM_D_INNER
EVEN_OFF_Q = EVEN_OFF_XBC + SSM_CONV_DIM
EVEN_OFF_KVC = EVEN_OFF_Q + N_Q
EVEN_OFF_KVS = EVEN_OFF_KVC + N_KV
EVEN_OFF_KVW = EVEN_OFF_KVS + N_KV
EVEN_OFF_GD = EVEN_OFF_KVW + N_KV
N_GD = 3 * NSA_HEADS + SSM_HEADS
EVEN_TOTAL = EVEN_OFF_GD + N_GD


def rope_tables(pos):
    f32 = jnp.float32
    inv = ROPE_THETA ** (-jnp.arange(0, ROT_DIM, 2, dtype=f32) / ROT_DIM)
    ang = pos.astype(f32)[:, None] * inv[None, :]
    cos, sin = jnp.cos(ang), jnp.sin(ang)
    r = pos.shape[0]
    zeros = lambda n: jnp.zeros((r, n), f32)
    c = jnp.concatenate([cos, cos, jnp.ones((r, HEAD_DIM - ROT_DIM), f32)], axis=1)
    sm = jnp.concatenate([-sin, zeros(HEAD_DIM - ROT_HALF)], axis=1)
    sp = jnp.concatenate([zeros(ROT_HALF), sin, zeros(HEAD_DIM - ROT_DIM)], axis=1)
    rep = LANES // HEAD_DIM
    return tuple(jnp.tile(a, (1, rep)) for a in (c, sm, sp))


def _rope_lanes(x, c, sm, sp):
    w = x.shape[1]
    reps = w // LANES
    tile = lambda a: jnp.concatenate([a] * reps, axis=1) if reps > 1 else a
    return (x * tile(c) + pltpu.roll(x, w - ROT_HALF, axis=1) * tile(sm)
            + pltpu.roll(x, ROT_HALF, axis=1) * tile(sp))


def _rope_keys_lanes(kv, c, sm, sp):
    half = kv.shape[1] // 2
    return jnp.concatenate([_rope_lanes(kv[:, :half], c, sm, sp), kv[:, half:]], axis=1)


def _normed_dot(x_ref, g_ref, w_ref):
    x = x_ref[...]
    xn = x * lax.rsqrt(jnp.mean(x * x, axis=-1, keepdims=True) + NORM_EPS) * g_ref[...]
    return jnp.dot(xn.astype(jnp.bfloat16), w_ref[...], preferred_element_type=jnp.float32)


def _even_proj_kernel(x_ref, g_ref, w_ref, c_ref, sm_ref, sp_ref,
                      z_ref, xbc_ref, q_ref, qr_ref, kvc_ref, kvs_ref, kvw_ref, gd_ref):
    o = _normed_dot(x_ref, g_ref, w_ref)
    rope = (c_ref[...], sm_ref[...], sp_ref[...])
    z_ref[...] = o[:, :EVEN_OFF_XBC]
    xbc_ref[...] = o[:, EVEN_OFF_XBC:EVEN_OFF_Q]
    q = o[:, EVEN_OFF_Q:EVEN_OFF_KVC]
    q_ref[...] = q
    qr_ref[...] = _rope_lanes(q, *rope)
    kvc_ref[...] = o[:, EVEN_OFF_KVC:EVEN_OFF_KVS]
    kvs_ref[...] = _rope_keys_lanes(o[:, EVEN_OFF_KVS:EVEN_OFF_KVW], *rope)
    kvw_ref[...] = _rope_keys_lanes(o[:, EVEN_OFF_KVW:EVEN_OFF_GD], *rope)
    gd_ref[...] = o[:, EVEN_OFF_GD:EVEN_TOTAL]


def _odd_proj_kernel(x_ref, g_ref, w_ref, c_ref, sm_ref, sp_ref, qr_ref, kv_ref):
    o = _normed_dot(x_ref, g_ref, w_ref)
    rope = (c_ref[...], sm_ref[...], sp_ref[...])
    qr_ref[...] = _rope_lanes(o[:, :N_Q], *rope)
    kv_ref[...] = _rope_keys_lanes(o[:, N_Q:], *rope)


def relayout_even_weight(w):
    dt0 = SSM_D_INNER + SSM_CONV_DIM
    return jnp.concatenate([w[:, :dt0], w[:, dt0 + SSM_HEADS:], w[:, dt0:dt0 + SSM_HEADS]], axis=1).astype(jnp.bfloat16)


def layer_projection(x2, g, w_bf16, rope_tabs, widths, body):
    m, k = x2.shape
    r = rope_tabs[0].shape[0]
    tm = min(m, r, 256)
    assert m % tm == 0 and r % tm == 0
    nr = r // tm
    row = lambda i: (i, 0)
    fixed = lambda i: (0, 0)
    tab = pl.BlockSpec((tm, LANES), lambda i: (i % nr, 0))
    return pl.pallas_call(
        body,
        grid=(m // tm,),
        in_specs=[pl.BlockSpec((tm, k), row), pl.BlockSpec((1, k), fixed), pl.BlockSpec(w_bf16.shape, fixed),
                  tab, tab, tab],
        out_specs=[pl.BlockSpec((tm, n), row) for n in widths],
        out_shape=[jax.ShapeDtypeStruct((m, n), jnp.float32) for n in widths],
        compiler_params=pltpu.CompilerParams(dimension_semantics=("arbitrary",),
                                             vmem_limit_bytes=PROJ_VMEM_LIMIT_BYTES),
        name=body.__name__.strip("_"),
    )(x2, g.reshape(1, k).astype(jnp.float32), w_bf16, *rope_tabs)


EVEN_OUT_WIDTHS = (SSM_D_INNER, SSM_CONV_DIM, N_Q, N_Q, N_KV, N_KV, N_KV, N_GD)
ODD_OUT_WIDTHS = (N_Q, N_KV)


def _ssd_kernel(z_ref, xbc_ref, gd_ref, dtt_ref, conv0_ref, h0_ref, cw_ref, cb_ref, dtb_ref, dtbt_ref,
                alog_ref, alogt_ref, dskip_ref, ng_ref, y_ref, hlast_ref, h_scr, xw_scr):
    f32, bf16 = jnp.float32, jnp.bfloat16
    hi = lax.Precision.HIGHEST
    Q = SSD_CHUNK
    c_idx = pl.program_id(1)
    tail = 8

    @pl.when(c_idx == 0)
    def _():
        h_scr[...] = h0_ref[0]
        xw_scr[0:tail, :] = conv0_ref[0]

    xw_scr[tail:tail + Q, :] = xbc_ref[...]
    conv = cb_ref[...]
    for k in range(SSM_CONV):
        start = tail - (SSM_CONV - 1) + k
        conv = conv + xw_scr[start:start + Q, :] * cw_ref[k:k + 1, :]
    xw_scr[0:tail, :] = xw_scr[Q:Q + tail, :]
    xc = conv * jax.nn.sigmoid(conv)
    xs = xc[:, :SSM_D_INNER]
    n_bc = SSM_GROUPS * SSM_STATE
    bm = xc[:, SSM_D_INNER:SSM_D_INNER + n_bc]
    cm = xc[:, SSM_D_INNER + n_bc:]

    dt = jax.nn.softplus(gd_ref[:, 3 * NSA_HEADS:] + dtb_ref[...])
    dtt = jax.nn.softplus(dtt_ref[0] + dtbt_ref[...])
    da = dt * (-jnp.exp(alog_ref[...]))
    dat = dtt * (-jnp.exp(alogt_ref[...]))
    ri = lax.broadcasted_iota(jnp.int32, (Q, Q), 0)
    ci = lax.broadcasted_iota(jnp.int32, (Q, Q), 1)
    causal = ci <= ri
    acum = jnp.dot(jnp.where(causal, 1.0, 0.0), da, precision=hi, preferred_element_type=f32)
    acumt = jnp.dot(dat, jnp.where(ri <= ci, 1.0, 0.0), precision=hi, preferred_element_type=f32)
    tot_t = acumt[:, Q - 1:Q]
    hh = lax.broadcasted_iota(jnp.int32, (SSM_HEADS, SSM_D_INNER), 0)
    ch = lax.broadcasted_iota(jnp.int32, (SSM_HEADS, SSM_D_INNER), 1) // SSM_HEAD_DIM
    expand = jnp.where(hh == ch, 1.0, 0.0)
    xdt = xs * jnp.dot(dt, expand, precision=hi, preferred_element_type=f32)
    xst = xs.T
    rep = SSM_HEADS // SSM_GROUPS
    ys = []
    for g in range(SSM_GROUPS):
        b_g = bm[:, g * SSM_STATE:(g + 1) * SSM_STATE]
        c_g = cm[:, g * SSM_STATE:(g + 1) * SSM_STATE]
        b_bf, c_bf = b_g.astype(bf16), c_g.astype(bf16)
        cb = lax.dot_general(c_bf, b_bf, (((1,), (1,)), ((), ())), preferred_element_type=f32)
        for h in range(g * rep, (g + 1) * rep):
            p0 = h * SSM_HEAD_DIM
            acol = acum[:, h:h + 1]
            arow = acumt[h:h + 1, :]
            decay = jnp.exp(jnp.where(causal, acol - arow, NEG_INF))
            y_h = jnp.dot((cb * decay).astype(bf16), xdt[:, p0:p0 + SSM_HEAD_DIM].astype(bf16),
                          preferred_element_type=f32)
            h_old = h_scr[h]
            y_h = y_h + jnp.exp(acol) * lax.dot_general(c_bf, h_old.astype(bf16), (((1,), (1,)), ((), ())),
                                                        preferred_element_type=f32)
            ys.append(y_h)
            w_row = dtt[h:h + 1, :] * jnp.exp(tot_t[h:h + 1, :] - arow)
            xdt_t = (xst[p0:p0 + SSM_HEAD_DIM, :] * w_row).astype(bf16)
            h_scr[h] = h_old * jnp.exp(tot_t[h:h + 1, :]) + jnp.dot(xdt_t, b_bf, preferred_element_type=f32)
    y = jnp.concatenate(ys, axis=1) + dskip_ref[...] * xs
    zz = z_ref[...]
    y = y * (zz * jax.nn.sigmoid(zz))
    gw = SSM_D_INNER // SSM_GROUPS
    outs = []
    for g in range(SSM_GROUPS):
        yg = y[:, g * gw:(g + 1) * gw]
        outs.append(yg * lax.rsqrt(jnp.mean(yg * yg, axis=-1, keepdims=True) + NORM_EPS))
    y_ref[...] = jnp.concatenate(outs, axis=1) * ng_ref[...]

    @pl.when(c_idx == pl.num_programs(1) - 1)
    def _():
        hlast_ref[0] = h_scr[...]


def ssd_mixer(z, xbc, gd, conv_state, ssm_state, conv_w, conv_b, dt_bias, a_log, d_skip, norm_g, batch):
    m = z.shape[0]
    t = m // batch
    assert t % SSD_CHUNK == 0
    nc = t // SSD_CHUNK
    f32 = jnp.float32
    dtt = jnp.transpose(gd[:, 3 * NSA_HEADS:].reshape(batch, t, SSM_HEADS), (0, 2, 1))
    conv0 = jnp.pad(conv_state.astype(f32), ((0, 0), (8 - (SSM_CONV - 1), 0), (0, 0)))
    row = lambda b, c: (b * nc + c, 0)
    fixed2 = lambda b, c: (0, 0)
    vec = lambda a: a.reshape(1, -1).astype(f32)
    col = lambda a: a.reshape(-1, 1).astype(f32)
    y, h_last = pl.pallas_call(
        _ssd_kernel,
        grid=(batch, nc),
        in_specs=[pl.BlockSpec((SSD_CHUNK, SSM_D_INNER), row),
                  pl.BlockSpec((SSD_CHUNK, SSM_CONV_DIM), row),
                  pl.BlockSpec((SSD_CHUNK, N_GD), row),
                  pl.BlockSpec((1, SSM_HEADS, SSD_CHUNK), lambda b, c: (b, 0, c)),
                  pl.BlockSpec((1, 8, SSM_CONV_DIM), lambda b, c: (b, 0, 0)),
                  pl.BlockSpec((1, SSM_HEADS, SSM_HEAD_DIM, SSM_STATE), lambda b, c: (b, 0, 0, 0)),
                  pl.BlockSpec((SSM_CONV, SSM_CONV_DIM), fixed2),
                  pl.BlockSpec((1, SSM_CONV_DIM), fixed2),
                  pl.BlockSpec((1, SSM_HEADS), fixed2), pl.BlockSpec((SSM_HEADS, 1), fixed2),
                  pl.BlockSpec((1, SSM_HEADS), fixed2), pl.BlockSpec((SSM_HEADS, 1), fixed2),
                  pl.BlockSpec((1, SSM_D_INNER), fixed2), pl.BlockSpec((1, SSM_D_INNER), fixed2)],
        out_specs=[pl.BlockSpec((SSD_CHUNK, SSM_D_INNER), row),
                   pl.BlockSpec((1, SSM_HEADS, SSM_HEAD_DIM, SSM_STATE), lambda b, c: (b, 0, 0, 0))],
        out_shape=[jax.ShapeDtypeStruct((m, SSM_D_INNER), f32),
                   jax.ShapeDtypeStruct((batch, SSM_HEADS, SSM_HEAD_DIM, SSM_STATE), f32)],
        scratch_shapes=[pltpu.VMEM((SSM_HEADS, SSM_HEAD_DIM, SSM_STATE), f32),
                        pltpu.VMEM((SSD_CHUNK + 8, SSM_CONV_DIM), f32)],
        compiler_params=pltpu.CompilerParams(dimension_semantics=("arbitrary", "arbitrary"),
                                             vmem_limit_bytes=VMEM_LIMIT_BYTES),
        name="ssd_mixer",
    )(z, xbc, gd, dtt, conv0, ssm_state.astype(f32), conv_w.astype(f32), vec(conv_b),
      vec(dt_bias), col(dt_bias), vec(a_log), col(a_log),
      vec(jnp.repeat(d_skip, SSM_HEAD_DIM)), vec(norm_g))
    return y, h_last


def _nsa_cmp_kernel(q_ref, x_ref, pe_ref, w1_ref, b1_ref, w2_ref, b2_ref, ov_ref, o_ref, sel_ref, cmp_scr):
    f32, bf16 = jnp.float32, jnp.bfloat16
    i = pl.program_id(1)
    tq = q_ref.shape[1]
    n_seg = x_ref.shape[2]
    n_sel = ov_ref.shape[1]
    n_cg = 2 * NSA_KV_HEADS

    @pl.when(i == 0)
    def _():
        for cg in range(n_cg):
            c = cg // NSA_KV_HEADS
            x = x_ref[0, cg]
            a0 = jnp.dot((x + pe_ref[c, 0]).astype(bf16), w1_ref[c, 0], preferred_element_type=f32)
            a1 = jnp.dot((x + pe_ref[c, 1]).astype(bf16), w1_ref[c, 1], preferred_element_type=f32)
            hid = jax.nn.gelu(b1_ref[c] + a0 + pltpu.roll(a1, n_seg - 1, axis=0))
            cmp_scr[cg] = jnp.dot(hid.astype(bf16), w2_ref[c], preferred_element_type=f32) + b2_ref[c]

    q = q_ref[0]
    tpos = i * tq + lax.broadcasted_iota(jnp.int32, (tq, 1), 0)
    n_io = lax.broadcasted_iota(jnp.int32, (1, n_seg), 1)
    cmp_ok = (n_io * NSA_CMP_STRIDE + NSA_CMP_BLOCK - 1 <= tpos) & (n_io < n_seg - 1)
    sidx = lax.broadcasted_iota(jnp.int32, (1, n_sel), 1)
    blk_t = tpos // NSA_SEL_BLOCK
    valid = sidx <= blk_t
    forced = valid & ((sidx == 0) | (sidx > blk_t - NSA_LOCAL_BLOCKS))
    outs = []
    for g in range(NSA_KV_HEADS):
        kc = cmp_scr[g].astype(bf16)
        vc = cmp_scr[NSA_KV_HEADS + g].astype(bf16)
        psum = jnp.zeros((tq, n_seg), f32)
        for j in range(NSA_GQ):
            h0 = (g * NSA_GQ + j) * HEAD_DIM
            hq = (q[:, h0:h0 + HEAD_DIM] * ATTN_SCALE).astype(bf16)
            s = lax.dot_general(hq, kc, (((1,), (1,)), ((), ())), preferred_element_type=f32)
            s = jnp.where(cmp_ok, s, NEG_INF)
            p = jnp.where(cmp_ok, jnp.exp(s - jnp.max(s, axis=-1, keepdims=True)), 0.0)
            p = p / jnp.maximum(jnp.sum(p, axis=-1, keepdims=True), 1e-30)
            outs.append(jnp.dot(p.astype(bf16), vc, preferred_element_type=f32))
            psum = psum + p
        imp = jnp.dot(psum, ov_ref[...], precision=lax.Precision.HIGHEST, preferred_element_type=f32)
        imp = jnp.where(forced, NSA_FORCE, imp)
        imp = jnp.where(valid, imp, NEG_INF)
        sel = jnp.zeros((tq, n_sel), f32)
        for _ in range(min(NSA_TOPN, n_sel)):
            mx = jnp.max(imp, axis=-1, keepdims=True)
            first = jnp.min(jnp.where(imp == mx, sidx, n_sel), axis=-1, keepdims=True)
            hit = sidx == first
            sel = jnp.where(hit & (mx > 0.5 * NEG_INF), 1.0, sel)
            imp = jnp.where(hit, REMOVED, imp)
        sel_ref[0, g] = sel
    o_ref[0] = jnp.concatenate(outs, axis=1)


def nsa_compressed_branch(q, kvc, batch, cmp_pos, w1, b1, w2, b2):
    f32, bf16 = jnp.float32, jnp.bfloat16
    t = q.shape[0] // batch
    tq = ATTN_TILE
    S = NSA_CMP_STRIDE
    r = NSA_CMP_BLOCK // S
    assert t % tq == 0 and r == 2
    n_seg, n_sel = t // S, t // NSA_SEL_BLOCK
    n_cg = 2 * NSA_KV_HEADS
    x = jnp.transpose(kvc.reshape(batch, n_seg, S, n_cg, HEAD_DIM), (0, 3, 1, 2, 4)).reshape(batch, n_cg, n_seg, S * HEAD_DIM)
    pe = cmp_pos.reshape(2, r, 1, S * HEAD_DIM).astype(f32)
    w1r = w1.reshape(2, r, S * HEAD_DIM, -1).astype(bf16)
    hid = w1r.shape[-1]
    cmp_start = S * jnp.arange(n_seg, dtype=jnp.int32)
    sel_start = NSA_SEL_BLOCK * jnp.arange(n_sel, dtype=jnp.int32)
    overlap = ((cmp_start[:, None] < sel_start[None, :] + NSA_SEL_BLOCK)
               & (cmp_start[:, None] + NSA_CMP_BLOCK > sel_start[None, :])).astype(f32)
    full = lambda a: pl.BlockSpec(a.shape, lambda b, i: (0,) * a.ndim)
    b1r, b2r, w2r = b1.reshape(2, 1, hid).astype(f32), b2.reshape(2, 1, HEAD_DIM).astype(f32), w2.astype(bf16)
    o_cmp, sel = pl.pallas_call(
        _nsa_cmp_kernel,
        grid=(batch, t // tq),
        in_specs=[pl.BlockSpec((1, tq, N_Q), lambda b, i: (b, i, 0)),
                  pl.BlockSpec((1, n_cg, n_seg, S * HEAD_DIM), lambda b, i: (b, 0, 0, 0)),
                  full(pe), full(w1r), full(b1r), full(w2r), full(b2r), full(overlap)],
        out_specs=[pl.BlockSpec((1, tq, N_Q), lambda b, i: (b, i, 0)),
                   pl.BlockSpec((1, NSA_KV_HEADS, tq, n_sel), lambda b, i: (b, 0, i, 0))],
        out_shape=[jax.ShapeDtypeStruct((batch, t, N_Q), f32),
                   jax.ShapeDtypeStruct((batch, NSA_KV_HEADS, t, n_sel), f32)],
        scratch_shapes=[pltpu.VMEM((n_cg, n_seg, HEAD_DIM), f32)],
        compiler_params=pltpu.CompilerParams(dimension_semantics=("arbitrary", "arbitrary"),
                                             vmem_limit_bytes=VMEM_LIMIT_BYTES),
        name="nsa_cmp",
    )(q.reshape(batch, t, N_Q), x, pe, w1r, b1r, w2r, b2r, overlap)
    return o_cmp, sel


ATTN_TILE = 256


def _flash_kernel(*refs, mode, n_kblocks):
    if mode == 'sel':
        q_ref, k_ref, v_ref, sel_ref, o_ref, m_scr, l_scr, acc_scr = refs
    else:
        q_ref, k_ref, v_ref, o_ref, m_scr, l_scr, acc_scr = refs
    f32 = jnp.float32
    tq = tk = ATTN_TILE
    i = pl.program_id(2)
    rows = q_ref.shape[2]
    q = q_ref[0, 0]
    qb = (q * ATTN_SCALE).astype(jnp.bfloat16)
    m_scr[...] = jnp.full(m_scr.shape, NEG_INF, f32)
    l_scr[...] = jnp.zeros(l_scr.shape, f32)
    acc_scr[...] = jnp.zeros(acc_scr.shape, f32)
    tpos = i * tq + lax.rem(lax.broadcasted_iota(jnp.int32, (rows, 1), 0), tq)

    if mode == 'moba':
        kmean = jnp.mean(k_ref[0, 0].reshape(n_kblocks, tk, HEAD_DIM), axis=1)
        gate = lax.dot_general(q, kmean, (((1,), (1,)), ((), ())),
                               precision=lax.Precision.HIGHEST, preferred_element_type=f32)
        blk = lax.broadcasted_iota(jnp.int32, (rows, n_kblocks), 1)
        valid = blk < i
        gate = jnp.where(valid, gate, NEG_INF)
        sel = jnp.zeros((rows, n_kblocks), f32)
        for _ in range(min(MOBA_TOPK, n_kblocks)):
            mx = jnp.max(gate, axis=-1, keepdims=True)
            first = jnp.min(jnp.where(gate == mx, blk, n_kblocks), axis=-1, keepdims=True)
            hit = blk == first
            sel = jnp.where(hit, 1.0, sel)
            gate = jnp.where(hit, -3e38, gate)
        sel = jnp.where(valid, sel, 0.0)
    elif mode == 'sel':
        sel = sel_ref[0, 0]

    for n in range(n_kblocks):
        cond = (n <= i) if mode != 'win' else ((n <= i) & (n * tk + tk - 1 >= i * tq - NSA_WINDOW + 1))

        @pl.when(cond)
        def _(n=n):
            kb = k_ref[0, 0, n * tk:(n + 1) * tk, :].astype(jnp.bfloat16)
            vb = v_ref[0, 0, n * tk:(n + 1) * tk, :].astype(jnp.bfloat16)
            s = lax.dot_general(qb, kb, (((1,), (1,)), ((), ())), preferred_element_type=f32)
            kpos = n * tk + lax.broadcasted_iota(jnp.int32, (1, tk), 1)
            mask = kpos <= tpos
            if mode == 'moba':
                own = jnp.where(n == i, 1.0, 0.0)
                mask = mask & ((sel[:, n:n + 1] + own) > 0.0)
            elif mode == 'sel':
                per = tk // NSA_SEL_BLOCK
                cb = lax.broadcasted_iota(jnp.int32, (1, tk), 1) // NSA_SEL_BLOCK
                se = sel[:, n * per + per - 1:n * per + per]
                for c in range(per - 2, -1, -1):
                    se = jnp.where(cb == c, sel[:, n * per + c:n * per + c + 1], se)
                mask = mask & (se > 0.0)
            else:
                mask = mask & (tpos - kpos < NSA_WINDOW)
            s = jnp.where(mask, s, NEG_INF)
            m_old = m_scr[...]
            m_new = jnp.maximum(m_old, jnp.max(s, axis=-1, keepdims=True))
            p = jnp.where(mask, jnp.exp(s - m_new), 0.0)
            alpha = jnp.exp(m_old - m_new)
            l_scr[...] = alpha * l_scr[...] + jnp.sum(p, axis=-1, keepdims=True)
            acc_scr[...] = alpha * acc_scr[...] + jnp.dot(p.astype(jnp.bfloat16), vb, preferred_element_type=f32)
            m_scr[...] = m_new

    o_ref[0, 0] = acc_scr[...] / jnp.maximum(l_scr[...], 1e-30)


def flash_attention(q, k, v, mode, sel=None):
    B, T, KV, GQ, D = q.shape
    tq = ATTN_TILE
    assert T % tq == 0 and k.shape[1] == T
    nq = T // tq
    rows = GQ * tq
    qr = jnp.transpose(q.reshape(B, nq, tq, KV, GQ, D), (0, 3, 1, 4, 2, 5)).reshape(B, KV, nq * rows, D)
    kr = jnp.transpose(k, (0, 2, 1, 3))
    vr = jnp.transpose(v, (0, 2, 1, 3))
    args = [qr, kr, vr]
    in_specs = [pl.BlockSpec((1, 1, rows, D), lambda b, g, i: (b, g, i, 0)),
                pl.BlockSpec((1, 1, T, D), lambda b, g, i: (b, g, 0, 0)),
                pl.BlockSpec((1, 1, T, D), lambda b, g, i: (b, g, 0, 0))]
    if mode == 'sel':
        ns = sel.shape[-1]
        sr = jnp.transpose(sel.reshape(B, nq, tq, KV, ns), (0, 3, 1, 2, 4))
        sr = jnp.broadcast_to(sr[:, :, :, None], (B, KV, nq, GQ, tq, ns)).reshape(B, KV, nq * rows, ns)
        args.append(sr.astype(jnp.float32))
        in_specs.append(pl.BlockSpec((1, 1, rows, ns), lambda b, g, i: (b, g, i, 0)))
    o = pl.pallas_call(
        functools.partial(_flash_kernel, mode=mode, n_kblocks=T // ATTN_TILE),
        grid=(B, KV, nq),
        in_specs=in_specs,
        out_specs=pl.BlockSpec((1, 1, rows, D), lambda b, g, i: (b, g, i, 0)),
        out_shape=jax.ShapeDtypeStruct((B, KV, nq * rows, D), jnp.float32),
        scratch_shapes=[pltpu.VMEM((rows, 1), jnp.float32), pltpu.VMEM((rows, 1), jnp.float32),
                        pltpu.VMEM((rows, D), jnp.float32)],
        compiler_params=pltpu.CompilerParams(dimension_semantics=("arbitrary", "arbitrary", "arbitrary"),
                                             vmem_limit_bytes=VMEM_LIMIT_BYTES),
        name="flash_" + mode,
    )(*args)
    return jnp.transpose(o.reshape(B, KV, nq, GQ, tq, D), (0, 2, 4, 1, 3, 5)).reshape(B, T, KV, GQ, D)


PEER_N_EXPERTS = PEER_N_KEYS * PEER_N_KEYS
PEER_SCORE_TILE = 256
PEER_TOKEN_TILE = 512
PEER_EXPERT_TILE = 1024
LANES = 128
REMOVED = -3e38


def _top_desc(s, k):
    outs = []
    for r in range(k):
        m = jnp.max(s, axis=0, keepdims=True)
        outs.append(m)
        if r + 1 < k:
            s = jnp.where(s == m, REMOVED, s)
    return jnp.concatenate(outs, axis=0)


def _peer_score_kernel(x_ref, g_ref, wq_ref, sk_ref, xt_ref, s1_ref, s2_ref, e1_ref, e2_ref, tau_ref):
    f32 = jnp.float32
    x = x_ref[...]
    xn = x * lax.rsqrt(jnp.mean(x * x, axis=-1, keepdims=True) + NORM_EPS) * g_ref[...]
    xt_ref[...] = xn.T.astype(jnp.bfloat16)
    q = jnp.dot(xn.astype(jnp.bfloat16), wq_ref[...], preferred_element_type=f32)
    half = PEER_KEY_DIM // 2
    taus = []
    for h in range(PEER_HEADS):
        st = []
        for c in range(2):
            qhc = q[:, (2 * h + c) * half:(2 * h + c + 1) * half]
            st.append(lax.dot_general(sk_ref[c], qhc, (((1,), (1,)), ((), ())),
                                      precision=lax.Precision.HIGHEST, preferred_element_type=f32))
        t1 = _top_desc(st[0], PEER_TOPK)
        t2 = _top_desc(st[1], PEER_TOPK)
        cand = jnp.concatenate([t1[i:i + 1] + t2 for i in range(PEER_TOPK)], axis=0)
        tops = _top_desc(cand, PEER_TOPK)
        z = jnp.sum(jnp.exp(tops - tops[0:1]), axis=0, keepdims=True)
        taus.append(tops[PEER_TOPK - 1:PEER_TOPK])
        s1_ref[h] = st[0]
        s2_ref[h] = st[1]
        e1_ref[h] = jnp.exp(st[0] - t1[0:1]) / z
        e2_ref[h] = jnp.exp(st[1] - t2[0:1])
    tau_ref[...] = jnp.concatenate(taus, axis=0)


def _peer_dense_kernel(xt_ref, s1_ref, s2_ref, e1_ref, e2_ref, tau_ref, u_ref, vt_ref, y_ref, act_scr, w_scr):
    f32 = jnp.float32
    j = pl.program_id(1)
    tm = xt_ref.shape[1]
    n_a = PEER_EXPERT_TILE // PEER_N_KEYS

    @pl.when(j == 0)
    def _():
        y_ref[...] = jnp.zeros(y_ref.shape, f32)

    act_scr[...] = jnp.dot(u_ref[...], xt_ref[...], preferred_element_type=f32)
    a_rows = pl.ds(pl.multiple_of(j * n_a, n_a), n_a)
    for aa in range(n_a):
        for tc in range(tm // LANES):
            lanes = slice(tc * LANES, (tc + 1) * LANES)
            acc = jnp.zeros((PEER_N_KEYS, LANES), f32)
            for h in range(PEER_HEADS):
                val = s2_ref[h, :, lanes] + s1_ref[h, a_rows, lanes][aa:aa + 1]
                gate = e2_ref[h, :, lanes] * e1_ref[h, a_rows, lanes][aa:aa + 1]
                acc = acc + jnp.where(val >= tau_ref[h:h + 1, lanes], gate, 0.0)
            act = act_scr[aa * PEER_N_KEYS:(aa + 1) * PEER_N_KEYS, lanes]
            w_scr[aa * PEER_N_KEYS:(aa + 1) * PEER_N_KEYS, lanes] = (acc * jax.nn.gelu(act)).astype(jnp.bfloat16)
    y_ref[...] += jnp.dot(vt_ref[...], w_scr[...], preferred_element_type=f32)


def peer_ffn_dense(x_res, g_norm, wq_bf16, subkeys, u_bf16, vt_bf16):
    n, d = x_res.shape
    tm = PEER_TOKEN_TILE if n % PEER_TOKEN_TILE == 0 else LANES
    ts = PEER_SCORE_TILE if n % PEER_SCORE_TILE == 0 else LANES
    n_pad = -(-n // tm) * tm
    xp = jnp.pad(x_res, ((0, n_pad - n), (0, 0)))
    hk = (PEER_HEADS, PEER_N_KEYS, n_pad)
    stat_spec = lambda t: pl.BlockSpec((PEER_HEADS, PEER_N_KEYS, t), lambda i, *_: (0, 0, i))
    xt, s1, s2, e1, e2, tau = pl.pallas_call(
        _peer_score_kernel,
        grid=(n_pad // ts,),
        in_specs=[pl.BlockSpec((ts, d), lambda i: (i, 0)),
                  pl.BlockSpec((1, d), lambda i: (0, 0)),
                  pl.BlockSpec(wq_bf16.shape, lambda i: (0, 0)),
                  pl.BlockSpec(subkeys.shape, lambda i: (0, 0, 0))],
        out_specs=[pl.BlockSpec((d, ts), lambda i: (0, i)), stat_spec(ts), stat_spec(ts), stat_spec(ts), stat_spec(ts),
                   pl.BlockSpec((PEER_HEADS, ts), lambda i: (0, i))],
        out_shape=[jax.ShapeDtypeStruct((d, n_pad), jnp.bfloat16)] + [jax.ShapeDtypeStruct(hk, jnp.float32)] * 4
                  + [jax.ShapeDtypeStruct((PEER_HEADS, n_pad), jnp.float32)],
        compiler_params=pltpu.CompilerParams(dimension_semantics=("arbitrary",),
                                             vmem_limit_bytes=VMEM_LIMIT_BYTES),
        name="peer_score",
    )(xp, g_norm.reshape(1, d).astype(jnp.float32), wq_bf16, subkeys.astype(jnp.float32))
    te = PEER_EXPERT_TILE
    yt = pl.pallas_call(
        _peer_dense_kernel,
        grid=(n_pad // tm, PEER_N_EXPERTS // te),
        in_specs=[pl.BlockSpec((d, tm), lambda i, j: (0, i)),
                  stat_spec(tm), stat_spec(tm), stat_spec(tm), stat_spec(tm),
                  pl.BlockSpec((PEER_HEADS, tm), lambda i, j: (0, i)),
                  pl.BlockSpec((te, d), lambda i, j: (j, 0)),
                  pl.BlockSpec((d, te), lambda i, j: (0, j))],
        out_specs=pl.BlockSpec((d, tm), lambda i, j: (0, i)),
        out_shape=jax.ShapeDtypeStruct((d, n_pad), jnp.float32),
        scratch_shapes=[pltpu.VMEM((te, tm), jnp.float32), pltpu.VMEM((te, tm), jnp.bfloat16)],
        compiler_params=pltpu.CompilerParams(dimension_semantics=("arbitrary", "arbitrary"),
                                             vmem_limit_bytes=VMEM_LIMIT_BYTES),
        name="peer_dense",
    )(xt, s1, s2, e1, e2, tau, u_bf16, vt_bf16)
    return yt.T[:n]


def rmsnorm(x, g):
    xf = x.astype(jnp.float32)
    y = xf * lax.rsqrt(jnp.mean(xf * xf, axis=-1, keepdims=True) + NORM_EPS)
    return (y * g.astype(jnp.float32)).astype(x.dtype)


def split_cols(a, widths):
    outs, off = [], 0
    for w in widths:
        outs.append(a[..., off:off + w])
        off += w
    return outs


def pad_axis1(a, n):
    return jnp.pad(a, [(0, 0), (0, n - a.shape[1])] + [(0, 0)] * (a.ndim - 2))


def qblocks(T, qmax):
    qb = min(qmax, T)
    nb = -(-T // qb)
    return qb, nb, nb * qb


def run_blocks(fn, nb, qb, T):
    out = lax.map(fn, jnp.arange(nb))
    out = jnp.moveaxis(out, 0, 1)
    return out.reshape(out.shape[:1] + (nb * qb,) + out.shape[3:])[:, :T]


def masked_softmax(s, mask):
    s = jnp.where(mask, s.astype(jnp.float32), NEG_INF)
    m = jnp.max(s, axis=-1, keepdims=True)
    p = jnp.exp(s - m) * mask
    return p / jnp.maximum(jnp.sum(p, axis=-1, keepdims=True), 1e-30)


def partial_rope(x, pos):
    half = ROT_DIM // 2
    inv = ROPE_THETA ** (-jnp.arange(0, ROT_DIM, 2, dtype=jnp.float32) / ROT_DIM)
    ang = pos.astype(jnp.float32)[:, None] * inv[None, :]
    cos = jnp.cos(ang)[:, None, :]
    sin = jnp.sin(ang)[:, None, :]
    x1 = x[..., :half].astype(jnp.float32)
    x2 = x[..., half:ROT_DIM].astype(jnp.float32)
    rot = jnp.concatenate([x1 * cos - x2 * sin, x2 * cos + x1 * sin], axis=-1).astype(x.dtype)
    return jnp.concatenate([rot, x[..., ROT_DIM:]], axis=-1)


def rope_keys(kv, pos):
    return jnp.stack([partial_rope(kv[:, :, 0], pos), kv[:, :, 1]], axis=2)


def gather_pages(pool, page_table):
    g = pool[page_table]
    return g.reshape((g.shape[0], g.shape[1] * g.shape[2]) + g.shape[3:])


def ssd_chunked(x, dt, a, b_h, c_h, h0):
    B, L, H, P = x.shape
    N = b_h.shape[-1]
    f32 = jnp.float32
    Q = min(SSD_CHUNK, L)
    nc = -(-L // Q)
    Lp = nc * Q
    xdt = pad_axis1(x.astype(f32) * dt[..., None], Lp).reshape(B, nc, Q, H, P)
    da = pad_axis1(dt * a, Lp).reshape(B, nc, Q, H)
    bc = pad_axis1(b_h.astype(f32), Lp).reshape(B, nc, Q, H, N)
    cc = pad_axis1(c_h.astype(f32), Lp).reshape(B, nc, Q, H, N)
    acum = jnp.cumsum(da, axis=2)
    causal = jnp.tril(jnp.ones((Q, Q), bool))
    seg = acum[:, :, :, None, :] - acum[:, :, None, :, :]
    decay_in = jnp.exp(jnp.where(causal[None, None, :, :, None], seg, NEG_INF))
    scores = jnp.einsum('bclhn,bcshn->bclsh', cc, bc) * decay_in
    y_diag = jnp.einsum('bclsh,bcshp->bclhp', scores, xdt)
    decay_out = jnp.exp(acum[:, :, -1:] - acum)
    chunk_states = jnp.einsum('bcshn,bcshp->bchpn', bc * decay_out[..., None], xdt)
    chunk_decay = jnp.exp(acum[:, :, -1])

    def step(h, inp):
        st, dec = inp
        return h * dec[:, :, None, None] + st, h

    h_last, h_enter = lax.scan(step, h0.astype(f32),
                               (jnp.moveaxis(chunk_states, 1, 0), jnp.moveaxis(chunk_decay, 1, 0)))
    h_enter = jnp.moveaxis(h_enter, 0, 1)
    y_off = jnp.einsum('bclhn,bchpn->bclhp', cc * jnp.exp(acum)[..., None], h_enter)
    y = (y_diag + y_off).reshape(B, Lp, H, P)[:, :L]
    return y, h_last


def mamba2_ssd(z, xbc, dt_raw, conv_state, ssm_state, conv_w, conv_b, dt_bias, a_log, d_skip, norm_g):
    B, T, _ = xbc.shape
    f32 = jnp.float32
    xpad = jnp.concatenate([conv_state.astype(xbc.dtype), xbc], axis=1)
    new_conv = xpad[:, T:]
    conv = conv_b
    for k in range(SSM_CONV):
        conv = conv + xpad[:, k:k + T] * conv_w[k]
    xbc_c = jax.nn.silu(conv)
    n_bc = SSM_GROUPS * SSM_STATE
    rep = SSM_HEADS // SSM_GROUPS
    xs = xbc_c[..., :SSM_D_INNER].reshape(B, T, SSM_HEADS, SSM_HEAD_DIM)
    b_h = jnp.repeat(xbc_c[..., SSM_D_INNER:SSM_D_INNER + n_bc].reshape(B, T, SSM_GROUPS, SSM_STATE), rep, axis=2)
    c_h = jnp.repeat(xbc_c[..., SSM_D_INNER + n_bc:].reshape(B, T, SSM_GROUPS, SSM_STATE), rep, axis=2)
    dt = jax.nn.softplus(dt_raw.astype(f32) + dt_bias.astype(f32))
    a = -jnp.exp(a_log.astype(f32))
    y, h_last = ssd_chunked(xs, dt, a, b_h, c_h, ssm_state)
    y = y + d_skip.astype(f32)[:, None] * xs.astype(f32)
    y = y.reshape(B, T, SSM_D_INNER) * jax.nn.silu(z.astype(f32))
    yg = y.reshape(B, T, SSM_GROUPS, SSM_D_INNER // SSM_GROUPS)
    yg = yg * lax.rsqrt(jnp.mean(yg * yg, axis=-1, keepdims=True) + NORM_EPS)
    y = yg.reshape(B, T, SSM_D_INNER) * norm_g.astype(f32)
    return y.astype(xbc.dtype), new_conv, h_last.astype(ssm_state.dtype)


def nsa_compress(kv_all, pos_emb, w1, b1, w2, b2):
    B, L = kv_all.shape[:2]
    S = NSA_CMP_STRIDE
    r = NSA_CMP_BLOCK // S
    n_seg = L // S
    n_cmp = n_seg - r + 1
    seg = kv_all[:, :n_seg * S].reshape(B, n_seg, S, 2, NSA_KV_HEADS, HEAD_DIM)
    h = b1[None, None, :, None, :]
    for j in range(r):
        pe = jnp.transpose(pos_emb[:, j * S:(j + 1) * S], (1, 0, 2))[:, :, None, :]
        h = h + jnp.einsum('bnlcgd,cldh->bncgh', seg[:, j:j + n_cmp] + pe, w1[:, j * S:(j + 1) * S])
    h = jax.nn.gelu(h)
    return jnp.einsum('bncgh,chd->bncgd', h, w2) + b2[None, None, :, None, :]


def nsa_selected(qr, kvs_all, sel_idx, sel_ok, pos0):
    B, T = qr.shape[:2]
    Lk = kvs_all.shape[1]
    n_sel = -(-Lk // NSA_SEL_BLOCK)
    topn = sel_idx.shape[-1]
    kvb = pad_axis1(kvs_all, n_sel * NSA_SEL_BLOCK).reshape(B, n_sel, NSA_SEL_BLOCK, 2, NSA_KV_HEADS, HEAD_DIM)
    kvb = jnp.transpose(kvb, (0, 4, 1, 2, 3, 5))
    qb, nb, Tp = qblocks(T, SEL_QBLOCK)
    qp, ip, okp = pad_axis1(qr, Tp), pad_axis1(sel_idx, Tp), pad_axis1(sel_ok, Tp)
    bi = jnp.arange(B)[:, None, None, None]
    gi = jnp.arange(NSA_KV_HEADS)[None, None, :, None]
    n_keys = topn * NSA_SEL_BLOCK

    def blk(i):
        start = i * qb
        q_b = lax.dynamic_slice_in_dim(qp, start, qb, 1)
        i_b = lax.dynamic_slice_in_dim(ip, start, qb, 1)
        ok_b = lax.dynamic_slice_in_dim(okp, start, qb, 1)
        qpos = pos0 + start + jnp.arange(qb)
        g = kvb[bi, gi, i_b]
        kpos = i_b[..., None] * NSA_SEL_BLOCK + jnp.arange(NSA_SEL_BLOCK)
        mask = (ok_b[..., None] & (kpos <= qpos[None, :, None, None, None])).reshape(B, qb, NSA_KV_HEADS, 1, n_keys)
        kk = g[..., 0, :].reshape(B, qb, NSA_KV_HEADS, n_keys, HEAD_DIM)
        vv = g[..., 1, :].reshape(B, qb, NSA_KV_HEADS, n_keys, HEAD_DIM)
        p = masked_softmax(jnp.einsum('bqgjd,bqgkd->bqgjk', q_b, kk) * ATTN_SCALE, mask)
        return jnp.einsum('bqgjk,bqgkd->bqgjd', p, vv)

    return run_blocks(blk, nb, qb, T)


def nsa_window(qr, kvw_ext, pos0):
    B, T = qr.shape[:2]
    Wb = kvw_ext.shape[1] - T
    W = NSA_WINDOW
    qb, nb, Tp = qblocks(T, WIN_QBLOCK)
    qp = pad_axis1(qr, Tp)
    kvp = jnp.pad(kvw_ext, [(0, 0), (W, Tp - T), (0, 0), (0, 0), (0, 0)])
    n_kp = W + Wb + Tp
    idx = jnp.arange(n_kp)
    kpos = (pos0 - Wb - W) + idx
    kval = (idx >= W) & (idx < W + Wb + T)

    def blk(i):
        start = i * qb
        q_b = lax.dynamic_slice_in_dim(qp, start, qb, 1)
        kv_b = lax.dynamic_slice_in_dim(kvp, start + Wb, W + qb, 1)
        kp_b = lax.dynamic_slice_in_dim(kpos, start + Wb, W + qb, 0)
        ok_b = lax.dynamic_slice_in_dim(kval, start + Wb, W + qb, 0)
        qpos = pos0 + start + jnp.arange(qb)
        mask = ok_b[None, :] & (kp_b[None, :] <= qpos[:, None]) & (qpos[:, None] - kp_b[None, :] < W)
        s = jnp.einsum('bqgjd,bkgd->bqgjk', q_b, kv_b[:, :, 0]) * ATTN_SCALE
        p = masked_softmax(s, mask[None, :, None, None, :])
        return jnp.einsum('bqgjk,bkgd->bqgjd', p, kv_b[:, :, 1])

    return run_blocks(blk, nb, qb, T)


def nsa_gate_combine(gates, o_cmp, o_sel, o_win):
    B, T = gates.shape[:2]
    g = jax.nn.sigmoid(gates.astype(jnp.float32)).reshape(B, T, NSA_HEADS, 3)
    heads = lambda o: o.reshape(B, T, NSA_HEADS, HEAD_DIM)
    o = g[..., 0:1] * heads(o_cmp) + g[..., 1:2] * heads(o_sel) + g[..., 2:3] * heads(o_win)
    return o.reshape(B, T, NSA_HEADS * HEAD_DIM)


def nsa_attention(q, qr, kvc_all, kvs_all, kvw_ext, gates, pos0, cmp_pos, cmp_w1, cmp_b1, cmp_w2, cmp_b2):
    B, T = q.shape[:2]
    pos_q = pos0 + jnp.arange(T, dtype=jnp.int32)
    qg = q.reshape(B, T, NSA_KV_HEADS, NSA_GQ, HEAD_DIM)
    qr = qr.reshape(B, T, NSA_KV_HEADS, NSA_GQ, HEAD_DIM)
    cmp = nsa_compress(kvc_all, cmp_pos, cmp_w1, cmp_b1, cmp_w2, cmp_b2)
    n_cmp = cmp.shape[1]
    cmp_start = NSA_CMP_STRIDE * jnp.arange(n_cmp, dtype=jnp.int32)
    cmp_mask = (cmp_start + NSA_CMP_BLOCK - 1)[None, :] <= pos_q[:, None]
    s = jnp.einsum('btgjd,bngd->btgjn', qg, cmp[:, :, 0]) * ATTN_SCALE
    p_cmp = masked_softmax(s, cmp_mask[None, :, None, None, :])
    o_cmp = jnp.einsum('btgjn,bngd->btgjd', p_cmp, cmp[:, :, 1])
    n_sel = -(-kvs_all.shape[1] // NSA_SEL_BLOCK)
    sel_start = NSA_SEL_BLOCK * jnp.arange(n_sel, dtype=jnp.int32)
    overlap = ((cmp_start[:, None] < sel_start[None, :] + NSA_SEL_BLOCK)
               & (cmp_start[:, None] + NSA_CMP_BLOCK > sel_start[None, :])).astype(jnp.float32)
    imp = jnp.einsum('btgjn,ns->btgs', p_cmp, overlap)
    blk_t = pos_q // NSA_SEL_BLOCK
    sidx = jnp.arange(n_sel, dtype=jnp.int32)
    valid = sidx[None, :] <= blk_t[:, None]
    forced = valid & ((sidx[None, :] == 0) | (sidx[None, :] > blk_t[:, None] - NSA_LOCAL_BLOCKS))
    imp = jnp.where(forced[None, :, None, :], NSA_FORCE, imp)
    imp = jnp.where(valid[None, :, None, :], imp, NEG_INF)
    sel_score, sel_idx = lax.top_k(imp, min(NSA_TOPN, n_sel))
    o_sel = nsa_selected(qr, kvs_all, sel_idx, sel_score > 0.5 * NEG_INF, pos0)
    o_win = nsa_window(qr, kvw_ext, pos0)
    flat = lambda o: o.reshape(B, T, NSA_HEADS * HEAD_DIM)
    return nsa_gate_combine(gates, flat(o_cmp), flat(o_sel), flat(o_win))


def moba_attention(qr, kv_all, pos0, prompt=False):
    B, T = qr.shape[:2]
    if prompt:
        o = flash_attention(qr.reshape(B, T, MOBA_KV_HEADS, MOBA_GQ, HEAD_DIM), kv_all[:, :, 0], kv_all[:, :, 1], 'moba')
        return o.reshape(B, T, MOBA_HEADS * HEAD_DIM)
    Lk = kv_all.shape[1]
    nblk = -(-Lk // MOBA_BLOCK)
    kvb = pad_axis1(kv_all, nblk * MOBA_BLOCK).reshape(B, nblk, MOBA_BLOCK, 2, MOBA_KV_HEADS, HEAD_DIM)
    kvb = jnp.transpose(kvb, (0, 4, 1, 2, 3, 5))
    kmean = jnp.mean(kvb[..., 0, :].astype(jnp.float32), axis=3)
    pos_q = pos0 + jnp.arange(T, dtype=jnp.int32)
    qg = qr.reshape(B, T, MOBA_KV_HEADS, MOBA_GQ, HEAD_DIM)
    gate = jnp.einsum('btgjd,bgnd->btgjn', qg.astype(jnp.float32), kmean)
    past_ok = jnp.arange(nblk)[None, :] < (pos_q // MOBA_BLOCK)[:, None]
    gate = jnp.where(past_ok[None, :, None, None, :], gate, NEG_INF)
    k = min(MOBA_TOPK, nblk)
    sc, idx = lax.top_k(gate, k)
    ok = sc > 0.5 * NEG_INF
    qb, nb, Tp = qblocks(T, MOBA_QBLOCK)
    qp, ip, okp = pad_axis1(qg, Tp), pad_axis1(idx, Tp), pad_axis1(ok, Tp)
    bi = jnp.arange(B)[:, None, None, None, None]
    gi = jnp.arange(MOBA_KV_HEADS)[None, None, :, None, None]
    bo = jnp.arange(B)[:, None, None]
    go = jnp.arange(MOBA_KV_HEADS)[None, None, :]
    n_sel = k * MOBA_BLOCK

    def blk(i):
        start = i * qb
        q_b = lax.dynamic_slice_in_dim(qp, start, qb, 1)
        i_b = lax.dynamic_slice_in_dim(ip, start, qb, 1)
        ok_b = lax.dynamic_slice_in_dim(okp, start, qb, 1)
        qpos = pos0 + start + jnp.arange(qb)
        own_blk = jnp.minimum(qpos // MOBA_BLOCK, nblk - 1)
        g_sel = kvb[bi, gi, i_b]
        g_own = kvb[bo, go, own_blk[None, :, None]]
        s_sel = jnp.einsum('bqgjd,bqgjkld->bqgjkl', q_b, g_sel[..., 0, :]).reshape(B, qb, MOBA_KV_HEADS, MOBA_GQ, n_sel)
        s_own = jnp.einsum('bqgjd,bqgld->bqgjl', q_b, g_own[..., 0, :])
        m_sel = jnp.broadcast_to(ok_b[..., None], ok_b.shape + (MOBA_BLOCK,)).reshape(B, qb, MOBA_KV_HEADS, MOBA_GQ, n_sel)
        own_pos = own_blk[:, None] * MOBA_BLOCK + jnp.arange(MOBA_BLOCK)
        m_own = jnp.broadcast_to((own_pos <= qpos[:, None])[None, :, None, None, :], (B, qb, MOBA_KV_HEADS, MOBA_GQ, MOBA_BLOCK))
        p = masked_softmax(jnp.concatenate([s_sel, s_own], axis=-1) * ATTN_SCALE,
                           jnp.concatenate([m_sel, m_own], axis=-1))
        v_sel = g_sel[..., 1, :].reshape(B, qb, MOBA_KV_HEADS, MOBA_GQ, n_sel, HEAD_DIM)
        return (jnp.einsum('bqgjm,bqgjmd->bqgjd', p[..., :n_sel], v_sel)
                + jnp.einsum('bqgjl,bqgld->bqgjd', p[..., n_sel:], g_own[..., 1, :]))

    o = run_blocks(blk, nb, qb, T)
    return o.reshape(B, T, MOBA_HEADS * HEAD_DIM).astype(qr.dtype)


def kernel(x_prompt, x_sample, cache_nsa_cmp_kv, cache_nsa_sel_kv, state_nsa_win_kv, state_ssm, state_conv,
           cache_moba_kv, page_table, norm_mix, norm_ffn, norm_final, w_in_even, w_out_even,
           ssm_conv_w, ssm_conv_b, ssm_dt_bias, ssm_a_log, ssm_d_skip, ssm_norm,
           nsa_cmp_pos, nsa_cmp_w1, nsa_cmp_b1, nsa_cmp_w2, nsa_cmp_b2, nsa_gate_b,
           w_in_odd, w_out_odd, peer_wq, peer_subkeys, peer_u, peer_v):

    peer_wq_bf16 = peer_wq.astype(jnp.bfloat16)
    peer_u_bf16 = peer_u.astype(jnp.bfloat16)
    peer_vt_bf16 = jnp.transpose(peer_v, (0, 2, 1)).astype(jnp.bfloat16)
    w_even_bf16 = [relayout_even_weight(w_in_even[e]) for e in range(w_in_even.shape[0])]
    w_odd_bf16 = w_in_odd.astype(jnp.bfloat16)

    def trunk(x, pos0, get_past, prompt):
        B, T, _ = x.shape
        pos_q = pos0 + jnp.arange(T, dtype=jnp.int32)
        new_cmp, new_sel, new_win, new_ssm, new_conv, new_moba = [], [], [], [], [], []
        tabs = rope_tables(pos_q)
        if T % ATTN_TILE != 0:
            tabs = tuple(jnp.tile(a, (B, 1)) for a in tabs)
        for l in range(DEPTH):
            x2 = x.reshape(B * T, D_MODEL)
            if l % 2 == 0:
                e = l // 2
                z, xbc, q, qr, kvc, kvs, kvw, gd = layer_projection(
                    x2, norm_mix[l], w_even_bf16[e], tabs, EVEN_OUT_WIDTHS, _even_proj_kernel)
                kv_shape = (B, T, 2, NSA_KV_HEADS, HEAD_DIM)
                kvc, kvs, kvw = kvc.reshape(kv_shape), kvs.reshape(kv_shape), kvw.reshape(kv_shape)
                gts = gd[:, :3 * NSA_HEADS].reshape(B, T, 3 * NSA_HEADS) + nsa_gate_b[e]
                kvw_ext = jnp.concatenate([get_past('win', e), kvw], axis=1)
                if prompt:
                    conv_past = get_past('conv', e)
                    y_ssm, ssm_new = ssd_mixer(z, xbc, gd, conv_past, get_past('ssm', e), ssm_conv_w[e], ssm_conv_b[e],
                                               ssm_dt_bias[e], ssm_a_log[e], ssm_d_skip[e], ssm_norm[e], B)
                    conv_new = jnp.concatenate([conv_past, xbc.reshape(B, T, SSM_CONV_DIM)], axis=1)[:, T:]
                    o_cmp, sel = nsa_compressed_branch(q, kvc.reshape(B * T, N_KV), B, nsa_cmp_pos[e], nsa_cmp_w1[e],
                                                       nsa_cmp_b1[e], nsa_cmp_w2[e], nsa_cmp_b2[e])
                    qr5 = qr.reshape(B, T, NSA_KV_HEADS, NSA_GQ, HEAD_DIM)
                    o_sel = flash_attention(qr5, kvs[:, :, 0], kvs[:, :, 1], 'sel', jnp.transpose(sel, (0, 2, 1, 3)))
                    o_win = flash_attention(qr5, kvw[:, :, 0], kvw[:, :, 1], 'win')
                    y_nsa = nsa_gate_combine(gts, o_cmp, o_sel.reshape(B, T, N_Q), o_win.reshape(B, T, N_Q))
                    y_ssm = y_ssm.reshape(B, T, SSM_D_INNER)
                else:
                    y_ssm, conv_new, ssm_new = mamba2_ssd(
                        z.reshape(B, T, -1), xbc.reshape(B, T, -1), gd[:, 3 * NSA_HEADS:].reshape(B, T, SSM_HEADS),
                        get_past('conv', e), get_past('ssm', e), ssm_conv_w[e], ssm_conv_b[e], ssm_dt_bias[e],
                        ssm_a_log[e], ssm_d_skip[e], ssm_norm[e])
                    y_nsa = nsa_attention(q.reshape(B, T, NSA_HEADS, HEAD_DIM), qr,
                                          jnp.concatenate([get_past('cmp', e), kvc], axis=1),
                                          jnp.concatenate([get_past('sel', e), kvs], axis=1),
                                          kvw_ext, gts, pos0,
                                          nsa_cmp_pos[e], nsa_cmp_w1[e], nsa_cmp_b1[e], nsa_cmp_w2[e], nsa_cmp_b2[e])
                cat = jnp.concatenate([y_ssm, y_nsa], axis=-1).reshape(B * T, -1)
                mix = norm_matmul(cat, w_out_even[e]).reshape(B, T, D_MODEL)
                keep = min(NSA_WINDOW, kvw_ext.shape[1])
                new_cmp.append(kvc)
                new_sel.append(kvs)
                new_win.append(kvw_ext[:, kvw_ext.shape[1] - keep:])
                new_ssm.append(ssm_new)
                new_conv.append(conv_new)
            else:
                o = l // 2
                q, kv = layer_projection(x2, norm_mix[l], w_odd_bf16[o], tabs, ODD_OUT_WIDTHS, _odd_proj_kernel)
                q = q.reshape(B, T, MOBA_HEADS, HEAD_DIM)
                kv = kv.reshape(B, T, 2, MOBA_KV_HEADS, HEAD_DIM)
                y_moba = moba_attention(q, jnp.concatenate([get_past('moba', o), kv], axis=1), pos0, prompt=prompt)
                mix = norm_matmul(y_moba.reshape(B * T, -1), w_out_odd[o]).reshape(B, T, D_MODEL)
                new_moba.append(kv)
            x = x + mix
            y_peer = peer_ffn_dense(x.reshape(B * T, D_MODEL), norm_ffn[l], peer_wq_bf16[l], peer_subkeys[l],
                                    peer_u_bf16[l], peer_vt_bf16[l])
            x = x + y_peer.reshape(B, T, D_MODEL)
        return (rmsnorm(x, norm_final), jnp.stack(new_cmp), jnp.stack(new_sel), jnp.stack(new_win),
                jnp.stack(new_ssm), jnp.stack(new_conv), jnp.stack(new_moba))

    bp = x_prompt.shape[0]
    dtp = x_prompt.dtype

    def prompt_past(kind, i):
        if kind == 'ssm':
            return jnp.zeros((bp, SSM_HEADS, SSM_HEAD_DIM, SSM_STATE), state_ssm.dtype)
        if kind == 'conv':
            return jnp.zeros((bp, SSM_CONV - 1, SSM_CONV_DIM), dtp)
        if kind == 'moba':
            return jnp.zeros((bp, 0, 2, MOBA_KV_HEADS, HEAD_DIM), dtp)
        return jnp.zeros((bp, 0, 2, NSA_KV_HEADS, HEAD_DIM), dtp)

    def sample_past(kind, i):
        if kind == 'ssm':
            return state_ssm[i]
        if kind == 'conv':
            return state_conv[i]
        if kind == 'win':
            return state_nsa_win_kv[i]
        if kind == 'cmp':
            return gather_pages(cache_nsa_cmp_kv[i], page_table)
        if kind == 'sel':
            return gather_pages(cache_nsa_sel_kv[i], page_table)
        return gather_pages(cache_moba_kv[i], page_table)

    past_len = page_table.shape[1] * cache_nsa_cmp_kv.shape[2]
    y_prompt, p_cmp, p_sel, p_win, p_ssm, p_conv, p_moba = trunk(x_prompt, 0, prompt_past, True)
    y_sample, s_cmp, s_sel, s_win, s_ssm, s_conv, s_moba = trunk(x_sample, past_len, sample_past, False)
    return (y_prompt, y_sample, p_cmp, p_sel, p_win, p_ssm, p_conv, p_moba,
            s_cmp, s_sel, s_win, s_ssm, s_conv, s_moba)
```

```python
import functools
import math

import jax
import jax.numpy as jnp
from jax import lax
from jax.experimental import pallas as pl
from jax.experimental.pallas import tpu as pltpu

D_MODEL = 1024
DEPTH = 4
HEAD_DIM = 64
ROT_DIM = HEAD_DIM // 4
ROPE_THETA = 500000.0
ATTN_SCALE = HEAD_DIM ** -0.5
NORM_EPS = 1e-6
NEG_INF = -1e30

SSM_HEADS = 16
SSM_HEAD_DIM = 64
SSM_D_INNER = SSM_HEADS * SSM_HEAD_DIM
SSM_GROUPS = 2
SSM_STATE = 128
SSM_CONV = 4
SSM_CONV_DIM = SSM_D_INNER + 2 * SSM_GROUPS * SSM_STATE
SSD_CHUNK = 128

NSA_HEADS = 16
NSA_KV_HEADS = 4
NSA_GQ = NSA_HEADS // NSA_KV_HEADS
NSA_CMP_BLOCK = 32
NSA_CMP_STRIDE = 16
NSA_SEL_BLOCK = 64
NSA_TOPN = 8
NSA_LOCAL_BLOCKS = 2
NSA_FORCE = 1e6
NSA_WINDOW = 512

MOBA_HEADS = 16
MOBA_KV_HEADS = 4
MOBA_GQ = MOBA_HEADS // MOBA_KV_HEADS
MOBA_BLOCK = 256
MOBA_TOPK = 3

PEER_HEADS = 8
PEER_N_KEYS = 128
PEER_KEY_DIM = 256
PEER_TOPK = 16

EVEN_WIDTHS = (SSM_D_INNER, SSM_CONV_DIM, SSM_HEADS, NSA_HEADS * HEAD_DIM,
               2 * NSA_KV_HEADS * HEAD_DIM, 2 * NSA_KV_HEADS * HEAD_DIM, 2 * NSA_KV_HEADS * HEAD_DIM,
               3 * NSA_HEADS)
ODD_WIDTHS = (MOBA_HEADS * HEAD_DIM, 2 * MOBA_KV_HEADS * HEAD_DIM)

WIN_QBLOCK = 128
SEL_QBLOCK = 16
MOBA_QBLOCK = 4
PEER_TBLOCK = 128

VMEM_LIMIT_BYTES = 48 * 1024 * 1024


def _mm_kernel(x_ref, g_ref, w_ref, o_ref, *, normalize):
    x = x_ref[...]
    if normalize:
        x = x * lax.rsqrt(jnp.mean(x * x, axis=-1, keepdims=True) + NORM_EPS) * g_ref[...]
    o_ref[...] = jnp.dot(x.astype(jnp.bfloat16), w_ref[...], preferred_element_type=jnp.float32)


def norm_matmul(x, w, g=None):
    m, k = x.shape
    n = w.shape[1]
    tm = min(m, 256)
    assert m % tm == 0
    gg = jnp.ones((1, k), jnp.float32) if g is None else g.reshape(1, k).astype(jnp.float32)
    return pl.pallas_call(
        functools.partial(_mm_kernel, normalize=g is not None),
        grid=(m // tm,),
        in_specs=[pl.BlockSpec((tm, k), lambda i: (i, 0)),
                  pl.BlockSpec((1, k), lambda i: (0, 0)),
                  pl.BlockSpec((k, n), lambda i: (0, 0))],
        out_specs=pl.BlockSpec((tm, n), lambda i: (i, 0)),
        out_shape=jax.ShapeDtypeStruct((m, n), jnp.float32),
        compiler_params=pltpu.CompilerParams(dimension_semantics=("arbitrary",),
                                             vmem_limit_bytes=VMEM_LIMIT_BYTES),
        name="norm_matmul",
    )(x, gg, w.astype(jnp.bfloat16))


LANES = 128
ROT_HALF = ROT_DIM // 2
PROJ_VMEM_LIMIT_BYTES = 56 * 1024 * 1024
N_Q = NSA_HEADS * HEAD_DIM
N_KV = 2 * NSA_KV_HEADS * HEAD_DIM
EVEN_OFF_XBC = SSM_D_INNER
EVEN_OFF_Q = EVEN_OFF_XBC + SSM_CONV_DIM
EVEN_OFF_KVC = EVEN_OFF_Q + N_Q
EVEN_OFF_KVS = EVEN_OFF_KVC + N_KV
EVEN_OFF_KVW = EVEN_OFF_KVS + N_KV
EVEN_OFF_GD = EVEN_OFF_KVW + N_KV
N_GD = 3 * NSA_HEADS + SSM_HEADS
EVEN_TOTAL = EVEN_OFF_GD + N_GD


def rope_tables(pos):
    f32 = jnp.float32
    inv = ROPE_THETA ** (-jnp.arange(0, ROT_DIM, 2, dtype=f32) / ROT_DIM)
    ang = pos.astype(f32)[:, None] * inv[None, :]
    cos, sin = jnp.cos(ang), jnp.sin(ang)
    r = pos.shape[0]
    zeros = lambda n: jnp.zeros((r, n), f32)
    c = jnp.concatenate([cos, cos, jnp.ones((r, HEAD_DIM - ROT_DIM), f32)], axis=1)
    sm = jnp.concatenate([-sin, zeros(HEAD_DIM - ROT_HALF)], axis=1)
    sp = jnp.concatenate([zeros(ROT_HALF), sin, zeros(HEAD_DIM - ROT_DIM)], axis=1)
    rep = LANES // HEAD_DIM
    return tuple(jnp.tile(a, (1, rep)) for a in (c, sm, sp))


def _rope_lanes(x, c, sm, sp):
    w = x.shape[1]
    reps = w // LANES
    tile = lambda a: jnp.concatenate([a] * reps, axis=1) if reps > 1 else a
    return (x * tile(c) + pltpu.roll(x, w - ROT_HALF, axis=1) * tile(sm)
            + pltpu.roll(x, ROT_HALF, axis=1) * tile(sp))


def _rope_keys_lanes(kv, c, sm, sp):
    half = kv.shape[1] // 2
    return jnp.concatenate([_rope_lanes(kv[:, :half], c, sm, sp), kv[:, half:]], axis=1)


def _normed_dot(x_ref, g_ref, w_ref):
    x = x_ref[...]
    xn = x * lax.rsqrt(jnp.mean(x * x, axis=-1, keepdims=True) + NORM_EPS) * g_ref[...]
    return jnp.dot(xn.astype(jnp.bfloat16), w_ref[...], preferred_element_type=jnp.float32)


def _even_proj_kernel(x_ref, g_ref, w_ref, c_ref, sm_ref, sp_ref,
                      z_ref, xbc_ref, q_ref, qr_ref, kvc_ref, kvs_ref, kvw_ref, gd_ref):
    o = _normed_dot(x_ref, g_ref, w_ref)
    rope = (c_ref[...], sm_ref[...], sp_ref[...])
    z_ref[...] = o[:, :EVEN_OFF_XBC]
    xbc_ref[...] = o[:, EVEN_OFF_XBC:EVEN_OFF_Q]
    q = o[:, EVEN_OFF_Q:EVEN_OFF_KVC]
    q_ref[...] = q
    qr_ref[...] = _rope_lanes(q, *rope)
    kvc_ref[...] = o[:, EVEN_OFF_KVC:EVEN_OFF_KVS]
    kvs_ref[...] = _rope_keys_lanes(o[:, EVEN_OFF_KVS:EVEN_OFF_KVW], *rope)
    kvw_ref[...] = _rope_keys_lanes(o[:, EVEN_OFF_KVW:EVEN_OFF_GD], *rope)
    gd_ref[...] = o[:, EVEN_OFF_GD:EVEN_TOTAL]


def _odd_proj_kernel(x_ref, g_ref, w_ref, c_ref, sm_ref, sp_ref, qr_ref, kv_ref):
    o = _normed_dot(x_ref, g_ref, w_ref)
    rope = (c_ref[...], sm_ref[...], sp_ref[...])
    qr_ref[...] = _rope_lanes(o[:, :N_Q], *rope)
    kv_ref[...] = _rope_keys_lanes(o[:, N_Q:], *rope)


def relayout_even_weight(w):
    dt0 = SSM_D_INNER + SSM_CONV_DIM
    return jnp.concatenate([w[:, :dt0], w[:, dt0 + SSM_HEADS:], w[:, dt0:dt0 + SSM_HEADS]], axis=1).astype(jnp.bfloat16)


def layer_projection(x2, g, w_bf16, rope_tabs, widths, body):
    m, k = x2.shape
    r = rope_tabs[0].shape[0]
    tm = min(m, r, 256)
    assert m % tm == 0 and r % tm == 0
    nr = r // tm
    row = lambda i: (i, 0)
    fixed = lambda i: (0, 0)
    tab = pl.BlockSpec((tm, LANES), lambda i: (i % nr, 0))
    return pl.pallas_call(
        body,
        grid=(m // tm,),
        in_specs=[pl.BlockSpec((tm, k), row), pl.BlockSpec((1, k), fixed), pl.BlockSpec(w_bf16.shape, fixed),
                  tab, tab, tab],
        out_specs=[pl.BlockSpec((tm, n), row) for n in widths],
        out_shape=[jax.ShapeDtypeStruct((m, n), jnp.float32) for n in widths],
        compiler_params=pltpu.CompilerParams(dimension_semantics=("arbitrary",),
                                             vmem_limit_bytes=PROJ_VMEM_LIMIT_BYTES),
        name=body.__name__.strip("_"),
    )(x2, g.reshape(1, k).astype(jnp.float32), w_bf16, *rope_tabs)


EVEN_OUT_WIDTHS = (SSM_D_INNER, SSM_CONV_DIM, N_Q, N_Q, N_KV, N_KV, N_KV, N_GD)
ODD_OUT_WIDTHS = (N_Q, N_KV)


def _ssd_kernel(z_ref, xbc_ref, gd_ref, dtt_ref, conv0_ref, h0_ref, cw_ref, cb_ref, dtb_ref, dtbt_ref,
                alog_ref, alogt_ref, dskip_ref, ng_ref, y_ref, hlast_ref, h_scr, xw_scr):
    f32, bf16 = jnp.float32, jnp.bfloat16
    hi = lax.Precision.HIGHEST
    Q = SSD_CHUNK
    c_idx = pl.program_id(1)
    tail = 8

    @pl.when(c_idx == 0)
    def _():
        h_scr[...] = h0_ref[0]
        xw_scr[0:tail, :] = conv0_ref[0]

    xw_scr[tail:tail + Q, :] = xbc_ref[...]
    conv = cb_ref[...]
    for k in range(SSM_CONV):
        start = tail - (SSM_CONV - 1) + k
        conv = conv + xw_scr[start:start + Q, :] * cw_ref[k:k + 1, :]
    xw_scr[0:tail, :] = xw_scr[Q:Q + tail, :]
    xc = conv * jax.nn.sigmoid(conv)
    xs = xc[:, :SSM_D_INNER]
    n_bc = SSM_GROUPS * SSM_STATE
    bm = xc[:, SSM_D_INNER:SSM_D_INNER + n_bc]
    cm = xc[:, SSM_D_INNER + n_bc:]

    dt = jax.nn.softplus(gd_ref[:, 3 * NSA_HEADS:] + dtb_ref[...])
    dtt = jax.nn.softplus(dtt_ref[0] + dtbt_ref[...])
    da = dt * (-jnp.exp(alog_ref[...]))
    dat = dtt * (-jnp.exp(alogt_ref[...]))
    ri = lax.broadcasted_iota(jnp.int32, (Q, Q), 0)
    ci = lax.broadcasted_iota(jnp.int32, (Q, Q), 1)
    causal = ci <= ri
    acum = jnp.dot(jnp.where(causal, 1.0, 0.0), da, precision=hi, preferred_element_type=f32)
    acumt = jnp.dot(dat, jnp.where(ri <= ci, 1.0, 0.0), precision=hi, preferred_element_type=f32)
    tot_t = acumt[:, Q - 1:Q]
    hh = lax.broadcasted_iota(jnp.int32, (SSM_HEADS, SSM_D_INNER), 0)
    ch = lax.broadcasted_iota(jnp.int32, (SSM_HEADS, SSM_D_INNER), 1) // SSM_HEAD_DIM
    expand = jnp.where(hh == ch, 1.0, 0.0)
    xdt = xs * jnp.dot(dt, expand, precision=hi, preferred_element_type=f32)
    xst = xs.T
    rep = SSM_HEADS // SSM_GROUPS
    ys = []
    for g in range(SSM_GROUPS):
        b_g = bm[:, g * SSM_STATE:(g + 1) * SSM_STATE]
        c_g = cm[:, g * SSM_STATE:(g + 1) * SSM_STATE]
        b_bf, c_bf = b_g.astype(bf16), c_g.astype(bf16)
        cb = lax.dot_general(c_bf, b_bf, (((1,), (1,)), ((), ())), preferred_element_type=f32)
        for h in range(g * rep, (g + 1) * rep):
            p0 = h * SSM_HEAD_DIM
            acol = acum[:, h:h + 1]
            arow = acumt[h:h + 1, :]
            decay = jnp.exp(jnp.where(causal, acol - arow, NEG_INF))
            y_h = jnp.dot((cb * decay).astype(bf16), xdt[:, p0:p0 + SSM_HEAD_DIM].astype(bf16),
                          preferred_element_type=f32)
            h_old = h_scr[h]
            y_h = y_h + jnp.exp(acol) * lax.dot_general(c_bf, h_old.astype(bf16), (((1,), (1,)), ((), ())),
                                                        preferred_element_type=f32)
            ys.append(y_h)
            w_row = dtt[h:h + 1, :] * jnp.exp(tot_t[h:h + 1, :] - arow)
            xdt_t = (xst[p0:p0 + SSM_HEAD_DIM, :] * w_row).astype(bf16)
            h_scr[h] = h_old * jnp.exp(tot_t[h:h + 1, :]) + jnp.dot(xdt_t, b_bf, preferred_element_type=f32)
    y = jnp.concatenate(ys, axis=1) + dskip_ref[...] * xs
    zz = z_ref[...]
    y = y * (zz * jax.nn.sigmoid(zz))
    gw = SSM_D_INNER // SSM_GROUPS
    outs = []
    for g in range(SSM_GROUPS):
        yg = y[:, g * gw:(g + 1) * gw]
        outs.append(yg * lax.rsqrt(jnp.mean(yg * yg, axis=-1, keepdims=True) + NORM_EPS))
    y_ref[...] = jnp.concatenate(outs, axis=1) * ng_ref[...]

    @pl.when(c_idx == pl.num_programs(1) - 1)
    def _():
        hlast_ref[0] = h_scr[...]


def ssd_mixer(z, xbc, gd, conv_state, ssm_state, conv_w, conv_b, dt_bias, a_log, d_skip, norm_g, batch):
    m = z.shape[0]
    t = m // batch
    assert t % SSD_CHUNK == 0
    nc = t // SSD_CHUNK
    f32 = jnp.float32
    dtt = jnp.transpose(gd[:, 3 * NSA_HEADS:].reshape(batch, t, SSM_HEADS), (0, 2, 1))
    conv0 = jnp.pad(conv_state.astype(f32), ((0, 0), (8 - (SSM_CONV - 1), 0), (0, 0)))
    row = lambda b, c: (b * nc + c, 0)
    fixed2 = lambda b, c: (0, 0)
    vec = lambda a: a.reshape(1, -1).astype(f32)
    col = lambda a: a.reshape(-1, 1).astype(f32)
    y, h_last = pl.pallas_call(
        _ssd_kernel,
        grid=(batch, nc),
        in_specs=[pl.BlockSpec((SSD_CHUNK, SSM_D_INNER), row),
                  pl.BlockSpec((SSD_CHUNK, SSM_CONV_DIM), row),
                  pl.BlockSpec((SSD_CHUNK, N_GD), row),
                  pl.BlockSpec((1, SSM_HEADS, SSD_CHUNK), lambda b, c: (b, 0, c)),
                  pl.BlockSpec((1, 8, SSM_CONV_DIM), lambda b, c: (b, 0, 0)),
                  pl.BlockSpec((1, SSM_HEADS, SSM_HEAD_DIM, SSM_STATE), lambda b, c: (b, 0, 0, 0)),
                  pl.BlockSpec((SSM_CONV, SSM_CONV_DIM), fixed2),
                  pl.BlockSpec((1, SSM_CONV_DIM), fixed2),
                  pl.BlockSpec((1, SSM_HEADS), fixed2), pl.BlockSpec((SSM_HEADS, 1), fixed2),
                  pl.BlockSpec((1, SSM_HEADS), fixed2), pl.BlockSpec((SSM_HEADS, 1), fixed2),
                  pl.BlockSpec((1, SSM_D_INNER), fixed2), pl.BlockSpec((1, SSM_D_INNER), fixed2)],
        out_specs=[pl.BlockSpec((SSD_CHUNK, SSM_D_INNER), row),
                   pl.BlockSpec((1, SSM_HEADS, SSM_HEAD_DIM, SSM_STATE), lambda b, c: (b, 0, 0, 0))],
        out_shape=[jax.ShapeDtypeStruct((m, SSM_D_INNER), f32),
                   jax.ShapeDtypeStruct((batch, SSM_HEADS, SSM_HEAD_DIM, SSM_STATE), f32)],
        scratch_shapes=[pltpu.VMEM((SSM_HEADS, SSM_HEAD_DIM, SSM_STATE), f32),
                        pltpu.VMEM((SSD_CHUNK + 8, SSM_CONV_DIM), f32)],
        compiler_params=pltpu.CompilerParams(dimension_semantics=("arbitrary", "arbitrary"),
                                             vmem_limit_bytes=VMEM_LIMIT_BYTES),
        name="ssd_mixer",
    )(z, xbc, gd, dtt, conv0, ssm_state.astype(f32), conv_w.astype(f32), vec(conv_b),
      vec(dt_bias), col(dt_bias), vec(a_log), col(a_log),
      vec(jnp.repeat(d_skip, SSM_HEAD_DIM)), vec(norm_g))
    return y, h_last


def _nsa_cmp_kernel(q_ref, x_ref, pe_ref, w1_ref, b1_ref, w2_ref, b2_ref, ov_ref, o_ref, sel_ref, cmp_scr):
    f32, bf16 = jnp.float32, jnp.bfloat16
    i = pl.program_id(1)
    tq = q_ref.shape[1]
    n_seg = x_ref.shape[2]
    n_sel = ov_ref.shape[1]
    n_cg = 2 * NSA_KV_HEADS

    @pl.when(i == 0)
    def _():
        for cg in range(n_cg):
            c = cg // NSA_KV_HEADS
            x = x_ref[0, cg]
            a0 = jnp.dot((x + pe_ref[c, 0]).astype(bf16), w1_ref[c, 0], preferred_element_type=f32)
            a1 = jnp.dot((x + pe_ref[c, 1]).astype(bf16), w1_ref[c, 1], preferred_element_type=f32)
            hid = jax.nn.gelu(b1_ref[c] + a0 + pltpu.roll(a1, n_seg - 1, axis=0))
            cmp_scr[cg] = jnp.dot(hid.astype(bf16), w2_ref[c], preferred_element_type=f32) + b2_ref[c]

    q = q_ref[0]
    tpos = i * tq + lax.broadcasted_iota(jnp.int32, (tq, 1), 0)
    n_io = lax.broadcasted_iota(jnp.int32, (1, n_seg), 1)
    cmp_ok = (n_io * NSA_CMP_STRIDE + NSA_CMP_BLOCK - 1 <= tpos) & (n_io < n_seg - 1)
    sidx = lax.broadcasted_iota(jnp.int32, (1, n_sel), 1)
    blk_t = tpos // NSA_SEL_BLOCK
    valid = sidx <= blk_t
    forced = valid & ((sidx == 0) | (sidx > blk_t - NSA_LOCAL_BLOCKS))
    eye = jnp.where(lax.broadcasted_iota(jnp.int32, (n_sel, n_sel), 0)
                    == lax.broadcasted_iota(jnp.int32, (n_sel, n_sel), 1), 1.0, 0.0)
    outs = []
    for g in range(NSA_KV_HEADS):
        kc = cmp_scr[g].astype(bf16)
        vc = cmp_scr[NSA_KV_HEADS + g].astype(bf16)
        psum = jnp.zeros((tq, n_seg), f32)
        for j in range(NSA_GQ):
            h0 = (g * NSA_GQ + j) * HEAD_DIM
            hq = (q[:, h0:h0 + HEAD_DIM] * ATTN_SCALE).astype(bf16)
            s = lax.dot_general(hq, kc, (((1,), (1,)), ((), ())), preferred_element_type=f32)
            s = jnp.where(cmp_ok, s, NEG_INF)
            p = jnp.where(cmp_ok, jnp.exp(s - jnp.max(s, axis=-1, keepdims=True)), 0.0)
            p = p / jnp.maximum(jnp.sum(p, axis=-1, keepdims=True), 1e-30)
            outs.append(jnp.dot(p.astype(bf16), vc, preferred_element_type=f32))
            psum = psum + p
        imp = jnp.dot(psum, ov_ref[...], precision=lax.Precision.HIGHEST, preferred_element_type=f32)
        imp = jnp.where(forced, NSA_FORCE, imp)
        imp = jnp.where(valid, imp, NEG_INF)
        sel = jnp.zeros((tq, n_sel), f32)
        for _ in range(min(NSA_TOPN, n_sel)):
            mx = jnp.max(imp, axis=-1, keepdims=True)
            first = jnp.min(jnp.where(imp == mx, sidx, n_sel), axis=-1, keepdims=True)
            hit = sidx == first
            sel = jnp.where(hit & (mx > 0.5 * NEG_INF), 1.0, sel)
            imp = jnp.where(hit, REMOVED, imp)
        sel_ref[0, g] = lax.dot_general(eye, sel, (((1,), (1,)), ((), ())), preferred_element_type=f32)
    o_ref[0] = jnp.concatenate(outs, axis=1)


def nsa_compressed_branch(q, kvc, batch, cmp_pos, w1, b1, w2, b2):
    f32, bf16 = jnp.float32, jnp.bfloat16
    t = q.shape[0] // batch
    tq = ATTN_TILE
    S = NSA_CMP_STRIDE
    r = NSA_CMP_BLOCK // S
    assert t % tq == 0 and r == 2
    n_seg, n_sel = t // S, t // NSA_SEL_BLOCK
    n_cg = 2 * NSA_KV_HEADS
    x = jnp.transpose(kvc.reshape(batch, n_seg, S, n_cg, HEAD_DIM), (0, 3, 1, 2, 4)).reshape(batch, n_cg, n_seg, S * HEAD_DIM)
    pe = cmp_pos.reshape(2, r, 1, S * HEAD_DIM).astype(f32)
    w1r = w1.reshape(2, r, S * HEAD_DIM, -1).astype(bf16)
    hid = w1r.shape[-1]
    cmp_start = S * jnp.arange(n_seg, dtype=jnp.int32)
    sel_start = NSA_SEL_BLOCK * jnp.arange(n_sel, dtype=jnp.int32)
    overlap = ((cmp_start[:, None] < sel_start[None, :] + NSA_SEL_BLOCK)
               & (cmp_start[:, None] + NSA_CMP_BLOCK > sel_start[None, :])).astype(f32)
    full = lambda a: pl.BlockSpec(a.shape, lambda b, i: (0,) * a.ndim)
    b1r, b2r, w2r = b1.reshape(2, 1, hid).astype(f32), b2.reshape(2, 1, HEAD_DIM).astype(f32), w2.astype(bf16)
    o_cmp, sel = pl.pallas_call(
        _nsa_cmp_kernel,
        grid=(batch, t // tq),
        in_specs=[pl.BlockSpec((1, tq, N_Q), lambda b, i: (b, i, 0)),
                  pl.BlockSpec((1, n_cg, n_seg, S * HEAD_DIM), lambda b, i: (b, 0, 0, 0)),
                  full(pe), full(w1r), full(b1r), full(w2r), full(b2r), full(overlap)],
        out_specs=[pl.BlockSpec((1, tq, N_Q), lambda b, i: (b, i, 0)),
                   pl.BlockSpec((1, NSA_KV_HEADS, n_sel, tq), lambda b, i: (b, 0, 0, i))],
        out_shape=[jax.ShapeDtypeStruct((batch, t, N_Q), f32),
                   jax.ShapeDtypeStruct((batch, NSA_KV_HEADS, n_sel, t), f32)],
        scratch_shapes=[pltpu.VMEM((n_cg, n_seg, HEAD_DIM), f32)],
        compiler_params=pltpu.CompilerParams(dimension_semantics=("arbitrary", "arbitrary"),
                                             vmem_limit_bytes=VMEM_LIMIT_BYTES),
        name="nsa_cmp",
    )(q.reshape(batch, t, N_Q), x, pe, w1r, b1r, w2r, b2r, overlap)
    return o_cmp, sel


ATTN_TILE = 256


def _flash_t_kernel(*refs, mode, n_kblocks):
    f32, bf16 = jnp.float32, jnp.bfloat16
    if mode == 'sel':
        q_ref, k_ref, v_ref, sel_ref, o_ref, qt_scr, m_scr, l_scr, acc_scr = refs
    elif mode == 'moba':
        q_ref, k_ref, v_ref, o_ref, qt_scr, m_scr, l_scr, acc_scr, selm_scr = refs
    else:
        q_ref, k_ref, v_ref, o_ref, qt_scr, m_scr, l_scr, acc_scr = refs
    tq = tk = ATTN_TILE
    D = HEAD_DIM
    i = pl.program_id(2)
    nh = q_ref.shape[2] // D
    gq = nh // 2
    qt = q_ref[0].T
    qt_scr[...] = (qt * ATTN_SCALE).astype(bf16)
    m_scr[...] = jnp.full(m_scr.shape, NEG_INF, f32)
    l_scr[...] = jnp.zeros(l_scr.shape, f32)
    acc_scr[...] = jnp.zeros(acc_scr.shape, f32)
    tpos = i * tq + lax.broadcasted_iota(jnp.int32, (1, tq), 1)
    krow = lax.broadcasted_iota(jnp.int32, (tk, 1), 0)
    brow = lax.broadcasted_iota(jnp.int32, (n_kblocks, tq), 0)

    if mode == 'moba':
        kmean = jnp.mean(k_ref[0].reshape(n_kblocks, tk, 2 * D), axis=1)
        valid = brow < i
        for jj in range(nh):
            gg = jj // gq
            gate = jnp.dot(kmean[:, gg * D:(gg + 1) * D], qt[jj * D:(jj + 1) * D, :],
                           precision=lax.Precision.HIGHEST, preferred_element_type=f32)
            gate = jnp.where(valid, gate, NEG_INF)
            sel = jnp.zeros((n_kblocks, tq), f32)
            for _ in range(min(MOBA_TOPK, n_kblocks)):
                mx = jnp.max(gate, axis=0, keepdims=True)
                first = jnp.min(jnp.where(gate == mx, brow, n_kblocks), axis=0, keepdims=True)
                hit = brow == first
                sel = jnp.where(hit, 1.0, sel)
                gate = jnp.where(hit, REMOVED, gate)
            selm_scr[jj] = jnp.where(valid, sel, 0.0)

    def body(n, carry):
        off = pl.multiple_of(n * tk, tk)
        kblk = k_ref[0, pl.ds(off, tk), :]
        vblk_t = v_ref[0, pl.ds(off, tk), :].T
        kpos = n * tk + krow
        base = kpos <= tpos
        if mode == 'win':
            base = base & (tpos - kpos < NSA_WINDOW)
        for gg in range(2):
            kb = kblk[:, gg * D:(gg + 1) * D].astype(bf16)
            vt = vblk_t[gg * D:(gg + 1) * D, :].astype(bf16)
            mask_g = base
            if mode == 'sel':
                st = sel_ref[0, gg]
                per = tk // NSA_SEL_BLOCK
                srow = lax.broadcasted_iota(jnp.int32, (st.shape[0], 1), 0)
                pieces = []
                for r in range(per):
                    row = jnp.sum(jnp.where(srow == n * per + r, st, 0.0), axis=0, keepdims=True)
                    pieces.append(jnp.broadcast_to(row, (NSA_SEL_BLOCK, tq)))
                mask_g = base & (jnp.concatenate(pieces, axis=0) > 0.0)
            for j in range(gq):
                jj = gg * gq + j
                mask = mask_g
                if mode == 'moba':
                    row = jnp.sum(jnp.where(brow == n, selm_scr[jj], 0.0), axis=0, keepdims=True)
                    own = jnp.where(n == i, 1.0, 0.0)
                    mask = base & ((row + own) > 0.0)
                s = jnp.dot(kb, qt_scr[jj * D:(jj + 1) * D, :], preferred_element_type=f32)
                s = jnp.where(mask, s, NEG_INF)
                m_old = m_scr[jj:jj + 1, :]
                m_new = jnp.maximum(m_old, jnp.max(s, axis=0, keepdims=True))
                p = jnp.where(mask, jnp.exp(s - m_new), 0.0)
                alpha = jnp.exp(m_old - m_new)
                l_scr[jj:jj + 1, :] = alpha * l_scr[jj:jj + 1, :] + jnp.sum(p, axis=0, keepdims=True)
                acc_scr[jj * D:(jj + 1) * D, :] = (alpha * acc_scr[jj * D:(jj + 1) * D, :]
                                                   + jnp.dot(vt, p.astype(bf16), preferred_element_type=f32))
                m_scr[jj:jj + 1, :] = m_new
        return carry

    if mode == 'win':
        lo = jnp.maximum(i - (NSA_WINDOW + tk - 1) // tk, 0)
    else:
        lo = 0
    lax.fori_loop(lo, i + 1, body, 0)
    outs = [acc_scr[jj * D:(jj + 1) * D, :] / jnp.maximum(l_scr[jj:jj + 1, :], 1e-30) for jj in range(nh)]
    o_ref[0] = jnp.concatenate(outs, axis=0).T


def flash_attention_t(q, kv, mode, batch, sel_t=None):
    f32 = jnp.float32
    t = q.shape[0] // batch
    tq = ATTN_TILE
    assert t % tq == 0
    n_pairs = NSA_KV_HEADS // 2
    pw = 2 * NSA_GQ * HEAD_DIM
    q3 = q.reshape(batch, t, N_Q)
    kv3 = kv.reshape(batch, t, N_KV)
    args = [q3, kv3, kv3]
    in_specs = [pl.BlockSpec((1, tq, pw), lambda b, g, i: (b, i, g)),
                pl.BlockSpec((1, t, 2 * HEAD_DIM), lambda b, g, i: (b, 0, g)),
                pl.BlockSpec((1, t, 2 * HEAD_DIM), lambda b, g, i: (b, 0, n_pairs + g))]
    scratch = [pltpu.VMEM((pw, tq), jnp.bfloat16), pltpu.VMEM((pw // HEAD_DIM, tq), f32),
               pltpu.VMEM((pw // HEAD_DIM, tq), f32), pltpu.VMEM((pw, tq), f32)]
    if mode == 'sel':
        args.append(sel_t)
        in_specs.append(pl.BlockSpec((1, 2, sel_t.shape[2], tq), lambda b, g, i: (b, g, 0, i)))
    if mode == 'moba':
        scratch.append(pltpu.VMEM((pw // HEAD_DIM, t // tq, tq), f32))
    return pl.pallas_call(
        functools.partial(_flash_t_kernel, mode=mode, n_kblocks=t // tq),
        grid=(batch, n_pairs, t // tq),
        in_specs=in_specs,
        out_specs=pl.BlockSpec((1, tq, pw), lambda b, g, i: (b, i, g)),
        out_shape=jax.ShapeDtypeStruct((batch, t, N_Q), f32),
        scratch_shapes=scratch,
        compiler_params=pltpu.CompilerParams(dimension_semantics=("arbitrary", "arbitrary", "arbitrary"),
                                             vmem_limit_bytes=VMEM_LIMIT_BYTES),
        name="flash_" + mode,
    )(*args)


PEER_N_EXPERTS = PEER_N_KEYS * PEER_N_KEYS
PEER_SCORE_TILE = 256
PEER_TOKEN_TILE = 512
PEER_EXPERT_TILE = 1024
LANES = 128
REMOVED = -3e38


def _top_desc(s, k):
    outs = []
    for r in range(k):
        m = jnp.max(s, axis=0, keepdims=True)
        outs.append(m)
        if r + 1 < k:
            s = jnp.where(s == m, REMOVED, s)
    return jnp.concatenate(outs, axis=0)


def _peer_score_kernel(x_ref, g_ref, wq_ref, sk_ref, xt_ref, s1_ref, s2_ref, e1_ref, e2_ref, tau_ref):
    f32 = jnp.float32
    x = x_ref[...]
    xn = x * lax.rsqrt(jnp.mean(x * x, axis=-1, keepdims=True) + NORM_EPS) * g_ref[...]
    xt_ref[...] = xn.T.astype(jnp.bfloat16)
    q = jnp.dot(xn.astype(jnp.bfloat16), wq_ref[...], preferred_element_type=f32)
    half = PEER_KEY_DIM // 2
    taus = []
    for h in range(PEER_HEADS):
        st = []
        for c in range(2):
            qhc = q[:, (2 * h + c) * half:(2 * h + c + 1) * half]
            st.append(lax.dot_general(sk_ref[c], qhc, (((1,), (1,)), ((), ())),
                                      precision=lax.Precision.HIGHEST, preferred_element_type=f32))
        t1 = _top_desc(st[0], PEER_TOPK)
        t2 = _top_desc(st[1], PEER_TOPK)
        cand = jnp.concatenate([t1[i:i + 1] + t2 for i in range(PEER_TOPK)], axis=0)
        tops = _top_desc(cand, PEER_TOPK)
        z = jnp.sum(jnp.exp(tops - tops[0:1]), axis=0, keepdims=True)
        taus.append(tops[PEER_TOPK - 1:PEER_TOPK])
        s1_ref[h] = st[0]
        s2_ref[h] = st[1]
        e1_ref[h] = jnp.exp(st[0] - t1[0:1]) / z
        e2_ref[h] = jnp.exp(st[1] - t2[0:1])
    tau_ref[...] = jnp.concatenate(taus, axis=0)


def _peer_dense_kernel(xres_ref, xt_ref, s1_ref, s2_ref, e1_ref, e2_ref, tau_ref, u_ref, vt_ref, o_ref,
                       act_scr, w_scr, yt_scr):
    f32 = jnp.float32
    j = pl.program_id(1)
    tm = xt_ref.shape[1]
    n_a = PEER_EXPERT_TILE // PEER_N_KEYS

    @pl.when(j == 0)
    def _():
        yt_scr[...] = jnp.zeros(yt_scr.shape, f32)

    act_scr[...] = jnp.dot(u_ref[...], xt_ref[...], preferred_element_type=f32)
    a_rows = pl.ds(pl.multiple_of(j * n_a, n_a), n_a)
    for aa in range(n_a):
        for tc in range(tm // LANES):
            lanes = slice(tc * LANES, (tc + 1) * LANES)
            acc = jnp.zeros((PEER_N_KEYS, LANES), f32)
            for h in range(PEER_HEADS):
                val = s2_ref[h, :, lanes] + s1_ref[h, a_rows, lanes][aa:aa + 1]
                gate = e2_ref[h, :, lanes] * e1_ref[h, a_rows, lanes][aa:aa + 1]
                acc = acc + jnp.where(val >= tau_ref[h:h + 1, lanes], gate, 0.0)
            act = act_scr[aa * PEER_N_KEYS:(aa + 1) * PEER_N_KEYS, lanes]
            w_scr[aa * PEER_N_KEYS:(aa + 1) * PEER_N_KEYS, lanes] = (acc * jax.nn.gelu(act)).astype(jnp.bfloat16)
    yt_scr[...] += jnp.dot(vt_ref[...], w_scr[...], preferred_element_type=f32)

    @pl.when(j == pl.num_programs(1) - 1)
    def _():
        o_ref[...] = xres_ref[...] + yt_scr[...].T


def peer_ffn_dense(x_res, g_norm, wq_bf16, subkeys, u_bf16, vt_bf16):
    n, d = x_res.shape
    tm = PEER_TOKEN_TILE if n % PEER_TOKEN_TILE == 0 else LANES
    ts = PEER_SCORE_TILE if n % PEER_SCORE_TILE == 0 else LANES
    n_pad = -(-n // tm) * tm
    xp = jnp.pad(x_res, ((0, n_pad - n), (0, 0)))
    hk = (PEER_HEADS, PEER_N_KEYS, n_pad)
    stat_spec = lambda t: pl.BlockSpec((PEER_HEADS, PEER_N_KEYS, t), lambda i, *_: (0, 0, i))
    xt, s1, s2, e1, e2, tau = pl.pallas_call(
        _peer_score_kernel,
        grid=(n_pad // ts,),
        in_specs=[pl.BlockSpec((ts, d), lambda i: (i, 0)),
                  pl.BlockSpec((1, d), lambda i: (0, 0)),
                  pl.BlockSpec(wq_bf16.shape, lambda i: (0, 0)),
                  pl.BlockSpec(subkeys.shape, lambda i: (0, 0, 0))],
        out_specs=[pl.BlockSpec((d, ts), lambda i: (0, i)), stat_spec(ts), stat_spec(ts), stat_spec(ts), stat_spec(ts),
                   pl.BlockSpec((PEER_HEADS, ts), lambda i: (0, i))],
        out_shape=[jax.ShapeDtypeStruct((d, n_pad), jnp.bfloat16)] + [jax.ShapeDtypeStruct(hk, jnp.float32)] * 4
                  + [jax.ShapeDtypeStruct((PEER_HEADS, n_pad), jnp.float32)],
        compiler_params=pltpu.CompilerParams(dimension_semantics=("arbitrary",),
                                             vmem_limit_bytes=VMEM_LIMIT_BYTES),
        name="peer_score",
    )(xp, g_norm.reshape(1, d).astype(jnp.float32), wq_bf16, subkeys.astype(jnp.float32))
    te = PEER_EXPERT_TILE
    out = pl.pallas_call(
        _peer_dense_kernel,
        grid=(n_pad // tm, PEER_N_EXPERTS // te),
        in_specs=[pl.BlockSpec((tm, d), lambda i, j: (i, 0)),
                  pl.BlockSpec((d, tm), lambda i, j: (0, i)),
                  stat_spec(tm), stat_spec(tm), stat_spec(tm), stat_spec(tm),
                  pl.BlockSpec((PEER_HEADS, tm), lambda i, j: (0, i)),
                  pl.BlockSpec((te, d), lambda i, j: (j, 0)),
                  pl.BlockSpec((d, te), lambda i, j: (0, j))],
        out_specs=pl.BlockSpec((tm, d), lambda i, j: (i, 0)),
        out_shape=jax.ShapeDtypeStruct((n_pad, d), jnp.float32),
        scratch_shapes=[pltpu.VMEM((te, tm), jnp.float32), pltpu.VMEM((te, tm), jnp.bfloat16),
                        pltpu.VMEM((d, tm), jnp.float32)],
        compiler_params=pltpu.CompilerParams(dimension_semantics=("arbitrary", "arbitrary"),
                                             vmem_limit_bytes=VMEM_LIMIT_BYTES),
        name="peer_dense",
    )(xp, xt, s1, s2, e1, e2, tau, u_bf16, vt_bf16)
    return out[:n]


def rmsnorm(x, g):
    xf = x.astype(jnp.float32)
    y = xf * lax.rsqrt(jnp.mean(xf * xf, axis=-1, keepdims=True) + NORM_EPS)
    return (y * g.astype(jnp.float32)).astype(x.dtype)


def split_cols(a, widths):
    outs, off = [], 0
    for w in widths:
        outs.append(a[..., off:off + w])
        off += w
    return outs


def pad_axis1(a, n):
    return jnp.pad(a, [(0, 0), (0, n - a.shape[1])] + [(0, 0)] * (a.ndim - 2))


def qblocks(T, qmax):
    qb = min(qmax, T)
    nb = -(-T // qb)
    return qb, nb, nb * qb


def run_blocks(fn, nb, qb, T):
    out = lax.map(fn, jnp.arange(nb))
    out = jnp.moveaxis(out, 0, 1)
    return out.reshape(out.shape[:1] + (nb * qb,) + out.shape[3:])[:, :T]


def masked_softmax(s, mask):
    s = jnp.where(mask, s.astype(jnp.float32), NEG_INF)
    m = jnp.max(s, axis=-1, keepdims=True)
    p = jnp.exp(s - m) * mask
    return p / jnp.maximum(jnp.sum(p, axis=-1, keepdims=True), 1e-30)


def partial_rope(x, pos):
    half = ROT_DIM // 2
    inv = ROPE_THETA ** (-jnp.arange(0, ROT_DIM, 2, dtype=jnp.float32) / ROT_DIM)
    ang = pos.astype(jnp.float32)[:, None] * inv[None, :]
    cos = jnp.cos(ang)[:, None, :]
    sin = jnp.sin(ang)[:, None, :]
    x1 = x[..., :half].astype(jnp.float32)
    x2 = x[..., half:ROT_DIM].astype(jnp.float32)
    rot = jnp.concatenate([x1 * cos - x2 * sin, x2 * cos + x1 * sin], axis=-1).astype(x.dtype)
    return jnp.concatenate([rot, x[..., ROT_DIM:]], axis=-1)


def rope_keys(kv, pos):
    return jnp.stack([partial_rope(kv[:, :, 0], pos), kv[:, :, 1]], axis=2)


def gather_pages(pool, page_table):
    g = pool[page_table]
    return g.reshape((g.shape[0], g.shape[1] * g.shape[2]) + g.shape[3:])


def ssd_chunked(x, dt, a, b_h, c_h, h0):
    B, L, H, P = x.shape
    N = b_h.shape[-1]
    f32 = jnp.float32
    Q = min(SSD_CHUNK, L)
    nc = -(-L // Q)
    Lp = nc * Q
    xdt = pad_axis1(x.astype(f32) * dt[..., None], Lp).reshape(B, nc, Q, H, P)
    da = pad_axis1(dt * a, Lp).reshape(B, nc, Q, H)
    bc = pad_axis1(b_h.astype(f32), Lp).reshape(B, nc, Q, H, N)
    cc = pad_axis1(c_h.astype(f32), Lp).reshape(B, nc, Q, H, N)
    acum = jnp.cumsum(da, axis=2)
    causal = jnp.tril(jnp.ones((Q, Q), bool))
    seg = acum[:, :, :, None, :] - acum[:, :, None, :, :]
    decay_in = jnp.exp(jnp.where(causal[None, None, :, :, None], seg, NEG_INF))
    scores = jnp.einsum('bclhn,bcshn->bclsh', cc, bc) * decay_in
    y_diag = jnp.einsum('bclsh,bcshp->bclhp', scores, xdt)
    decay_out = jnp.exp(acum[:, :, -1:] - acum)
    chunk_states = jnp.einsum('bcshn,bcshp->bchpn', bc * decay_out[..., None], xdt)
    chunk_decay = jnp.exp(acum[:, :, -1])

    def step(h, inp):
        st, dec = inp
        return h * dec[:, :, None, None] + st, h

    h_last, h_enter = lax.scan(step, h0.astype(f32),
                               (jnp.moveaxis(chunk_states, 1, 0), jnp.moveaxis(chunk_decay, 1, 0)))
    h_enter = jnp.moveaxis(h_enter, 0, 1)
    y_off = jnp.einsum('bclhn,bchpn->bclhp', cc * jnp.exp(acum)[..., None], h_enter)
    y = (y_diag + y_off).reshape(B, Lp, H, P)[:, :L]
    return y, h_last


def mamba2_ssd(z, xbc, dt_raw, conv_state, ssm_state, conv_w, conv_b, dt_bias, a_log, d_skip, norm_g):
    B, T, _ = xbc.shape
    f32 = jnp.float32
    xpad = jnp.concatenate([conv_state.astype(xbc.dtype), xbc], axis=1)
    new_conv = xpad[:, T:]
    conv = conv_b
    for k in range(SSM_CONV):
        conv = conv + xpad[:, k:k + T] * conv_w[k]
    xbc_c = jax.nn.silu(conv)
    n_bc = SSM_GROUPS * SSM_STATE
    rep = SSM_HEADS // SSM_GROUPS
    xs = xbc_c[..., :SSM_D_INNER].reshape(B, T, SSM_HEADS, SSM_HEAD_DIM)
    b_h = jnp.repeat(xbc_c[..., SSM_D_INNER:SSM_D_INNER + n_bc].reshape(B, T, SSM_GROUPS, SSM_STATE), rep, axis=2)
    c_h = jnp.repeat(xbc_c[..., SSM_D_INNER + n_bc:].reshape(B, T, SSM_GROUPS, SSM_STATE), rep, axis=2)
    dt = jax.nn.softplus(dt_raw.astype(f32) + dt_bias.astype(f32))
    a = -jnp.exp(a_log.astype(f32))
    y, h_last = ssd_chunked(xs, dt, a, b_h, c_h, ssm_state)
    y = y + d_skip.astype(f32)[:, None] * xs.astype(f32)
    y = y.reshape(B, T, SSM_D_INNER) * jax.nn.silu(z.astype(f32))
    yg = y.reshape(B, T, SSM_GROUPS, SSM_D_INNER // SSM_GROUPS)
    yg = yg * lax.rsqrt(jnp.mean(yg * yg, axis=-1, keepdims=True) + NORM_EPS)
    y = yg.reshape(B, T, SSM_D_INNER) * norm_g.astype(f32)
    return y.astype(xbc.dtype), new_conv, h_last.astype(ssm_state.dtype)


def nsa_compress(kv_all, pos_emb, w1, b1, w2, b2):
    B, L = kv_all.shape[:2]
    S = NSA_CMP_STRIDE
    r = NSA_CMP_BLOCK // S
    n_seg = L // S
    n_cmp = n_seg - r + 1
    seg = kv_all[:, :n_seg * S].reshape(B, n_seg, S, 2, NSA_KV_HEADS, HEAD_DIM)
    h = b1[None, None, :, None, :]
    for j in range(r):
        pe = jnp.transpose(pos_emb[:, j * S:(j + 1) * S], (1, 0, 2))[:, :, None, :]
        h = h + jnp.einsum('bnlcgd,cldh->bncgh', seg[:, j:j + n_cmp] + pe, w1[:, j * S:(j + 1) * S])
    h = jax.nn.gelu(h)
    return jnp.einsum('bncgh,chd->bncgd', h, w2) + b2[None, None, :, None, :]


def nsa_selected(qr, kvs_all, sel_idx, sel_ok, pos0):
    B, T = qr.shape[:2]
    Lk = kvs_all.shape[1]
    n_sel = -(-Lk // NSA_SEL_BLOCK)
    topn = sel_idx.shape[-1]
    kvb = pad_axis1(kvs_all, n_sel * NSA_SEL_BLOCK).reshape(B, n_sel, NSA_SEL_BLOCK, 2, NSA_KV_HEADS, HEAD_DIM)
    kvb = jnp.transpose(kvb, (0, 4, 1, 2, 3, 5))
    qb, nb, Tp = qblocks(T, SEL_QBLOCK)
    qp, ip, okp = pad_axis1(qr, Tp), pad_axis1(sel_idx, Tp), pad_axis1(sel_ok, Tp)
    bi = jnp.arange(B)[:, None, None, None]
    gi = jnp.arange(NSA_KV_HEADS)[None, None, :, None]
    n_keys = topn * NSA_SEL_BLOCK

    def blk(i):
        start = i * qb
        q_b = lax.dynamic_slice_in_dim(qp, start, qb, 1)
        i_b = lax.dynamic_slice_in_dim(ip, start, qb, 1)
        ok_b = lax.dynamic_slice_in_dim(okp, start, qb, 1)
        qpos = pos0 + start + jnp.arange(qb)
        g = kvb[bi, gi, i_b]
        kpos = i_b[..., None] * NSA_SEL_BLOCK + jnp.arange(NSA_SEL_BLOCK)
        mask = (ok_b[..., None] & (kpos <= qpos[None, :, None, None, None])).reshape(B, qb, NSA_KV_HEADS, 1, n_keys)
        kk = g[..., 0, :].reshape(B, qb, NSA_KV_HEADS, n_keys, HEAD_DIM)
        vv = g[..., 1, :].reshape(B, qb, NSA_KV_HEADS, n_keys, HEAD_DIM)
        p = masked_softmax(jnp.einsum('bqgjd,bqgkd->bqgjk', q_b, kk) * ATTN_SCALE, mask)
        return jnp.einsum('bqgjk,bqgkd->bqgjd', p, vv)

    return run_blocks(blk, nb, qb, T)


def nsa_window(qr, kvw_ext, pos0):
    B, T = qr.shape[:2]
    Wb = kvw_ext.shape[1] - T
    W = NSA_WINDOW
    qb, nb, Tp = qblocks(T, WIN_QBLOCK)
    qp = pad_axis1(qr, Tp)
    kvp = jnp.pad(kvw_ext, [(0, 0), (W, Tp - T), (0, 0), (0, 0), (0, 0)])
    n_kp = W + Wb + Tp
    idx = jnp.arange(n_kp)
    kpos = (pos0 - Wb - W) + idx
    kval = (idx >= W) & (idx < W + Wb + T)

    def blk(i):
        start = i * qb
        q_b = lax.dynamic_slice_in_dim(qp, start, qb, 1)
        kv_b = lax.dynamic_slice_in_dim(kvp, start + Wb, W + qb, 1)
        kp_b = lax.dynamic_slice_in_dim(kpos, start + Wb, W + qb, 0)
        ok_b = lax.dynamic_slice_in_dim(kval, start + Wb, W + qb, 0)
        qpos = pos0 + start + jnp.arange(qb)
        mask = ok_b[None, :] & (kp_b[None, :] <= qpos[:, None]) & (qpos[:, None] - kp_b[None, :] < W)
        s = jnp.einsum('bqgjd,bkgd->bqgjk', q_b, kv_b[:, :, 0]) * ATTN_SCALE
        p = masked_softmax(s, mask[None, :, None, None, :])
        return jnp.einsum('bqgjk,bkgd->bqgjd', p, kv_b[:, :, 1])

    return run_blocks(blk, nb, qb, T)


def nsa_gate_combine(gates, o_cmp, o_sel, o_win):
    B, T = gates.shape[:2]
    g = jax.nn.sigmoid(gates.astype(jnp.float32)).reshape(B, T, NSA_HEADS, 3)
    heads = lambda o: o.reshape(B, T, NSA_HEADS, HEAD_DIM)
    o = g[..., 0:1] * heads(o_cmp) + g[..., 1:2] * heads(o_sel) + g[..., 2:3] * heads(o_win)
    return o.reshape(B, T, NSA_HEADS * HEAD_DIM)


def nsa_attention(q, qr, kvc_all, kvs_all, kvw_ext, gates, pos0, cmp_pos, cmp_w1, cmp_b1, cmp_w2, cmp_b2):
    B, T = q.shape[:2]
    pos_q = pos0 + jnp.arange(T, dtype=jnp.int32)
    qg = q.reshape(B, T, NSA_KV_HEADS, NSA_GQ, HEAD_DIM)
    qr = qr.reshape(B, T, NSA_KV_HEADS, NSA_GQ, HEAD_DIM)
    cmp = nsa_compress(kvc_all, cmp_pos, cmp_w1, cmp_b1, cmp_w2, cmp_b2)
    n_cmp = cmp.shape[1]
    cmp_start = NSA_CMP_STRIDE * jnp.arange(n_cmp, dtype=jnp.int32)
    cmp_mask = (cmp_start + NSA_CMP_BLOCK - 1)[None, :] <= pos_q[:, None]
    s = jnp.einsum('btgjd,bngd->btgjn', qg, cmp[:, :, 0]) * ATTN_SCALE
    p_cmp = masked_softmax(s, cmp_mask[None, :, None, None, :])
    o_cmp = jnp.einsum('btgjn,bngd->btgjd', p_cmp, cmp[:, :, 1])
    n_sel = -(-kvs_all.shape[1] // NSA_SEL_BLOCK)
    sel_start = NSA_SEL_BLOCK * jnp.arange(n_sel, dtype=jnp.int32)
    overlap = ((cmp_start[:, None] < sel_start[None, :] + NSA_SEL_BLOCK)
               & (cmp_start[:, None] + NSA_CMP_BLOCK > sel_start[None, :])).astype(jnp.float32)
    imp = jnp.einsum('btgjn,ns->btgs', p_cmp, overlap)
    blk_t = pos_q // NSA_SEL_BLOCK
    sidx = jnp.arange(n_sel, dtype=jnp.int32)
    valid = sidx[None, :] <= blk_t[:, None]
    forced = valid & ((sidx[None, :] == 0) | (sidx[None, :] > blk_t[:, None] - NSA_LOCAL_BLOCKS))
    imp = jnp.where(forced[None, :, None, :], NSA_FORCE, imp)
    imp = jnp.where(valid[None, :, None, :], imp, NEG_INF)
    sel_score, sel_idx = lax.top_k(imp, min(NSA_TOPN, n_sel))
    o_sel = nsa_selected(qr, kvs_all, sel_idx, sel_score > 0.5 * NEG_INF, pos0)
    o_win = nsa_window(qr, kvw_ext, pos0)
    flat = lambda o: o.reshape(B, T, NSA_HEADS * HEAD_DIM)
    return nsa_gate_combine(gates, flat(o_cmp), flat(o_sel), flat(o_win))


def moba_attention(qr, kv_all, pos0):
    B, T = qr.shape[:2]
    Lk = kv_all.shape[1]
    nblk = -(-Lk // MOBA_BLOCK)
    kvb = pad_axis1(kv_all, nblk * MOBA_BLOCK).reshape(B, nblk, MOBA_BLOCK, 2, MOBA_KV_HEADS, HEAD_DIM)
    kvb = jnp.transpose(kvb, (0, 4, 1, 2, 3, 5))
    kmean = jnp.mean(kvb[..., 0, :].astype(jnp.float32), axis=3)
    pos_q = pos0 + jnp.arange(T, dtype=jnp.int32)
    qg = qr.reshape(B, T, MOBA_KV_HEADS, MOBA_GQ, HEAD_DIM)
    gate = jnp.einsum('btgjd,bgnd->btgjn', qg.astype(jnp.float32), kmean)
    past_ok = jnp.arange(nblk)[None, :] < (pos_q // MOBA_BLOCK)[:, None]
    gate = jnp.where(past_ok[None, :, None, None, :], gate, NEG_INF)
    k = min(MOBA_TOPK, nblk)
    sc, idx = lax.top_k(gate, k)
    ok = sc > 0.5 * NEG_INF
    qb, nb, Tp = qblocks(T, MOBA_QBLOCK)
    qp, ip, okp = pad_axis1(qg, Tp), pad_axis1(idx, Tp), pad_axis1(ok, Tp)
    bi = jnp.arange(B)[:, None, None, None, None]
    gi = jnp.arange(MOBA_KV_HEADS)[None, None, :, None, None]
    bo = jnp.arange(B)[:, None, None]
    go = jnp.arange(MOBA_KV_HEADS)[None, None, :]
    n_sel = k * MOBA_BLOCK

    def blk(i):
        start = i * qb
        q_b = lax.dynamic_slice_in_dim(qp, start, qb, 1)
        i_b = lax.dynamic_slice_in_dim(ip, start, qb, 1)
        ok_b = lax.dynamic_slice_in_dim(okp, start, qb, 1)
        qpos = pos0 + start + jnp.arange(qb)
        own_blk = jnp.minimum(qpos // MOBA_BLOCK, nblk - 1)
        g_sel = kvb[bi, gi, i_b]
        g_own = kvb[bo, go, own_blk[None, :, None]]
        s_sel = jnp.einsum('bqgjd,bqgjkld->bqgjkl', q_b, g_sel[..., 0, :]).reshape(B, qb, MOBA_KV_HEADS, MOBA_GQ, n_sel)
        s_own = jnp.einsum('bqgjd,bqgld->bqgjl', q_b, g_own[..., 0, :])
        m_sel = jnp.broadcast_to(ok_b[..., None], ok_b.shape + (MOBA_BLOCK,)).reshape(B, qb, MOBA_KV_HEADS, MOBA_GQ, n_sel)
        own_pos = own_blk[:, None] * MOBA_BLOCK + jnp.arange(MOBA_BLOCK)
        m_own = jnp.broadcast_to((own_pos <= qpos[:, None])[None, :, None, None, :], (B, qb, MOBA_KV_HEADS, MOBA_GQ, MOBA_BLOCK))
        p = masked_softmax(jnp.concatenate([s_sel, s_own], axis=-1) * ATTN_SCALE,
                           jnp.concatenate([m_sel, m_own], axis=-1))
        v_sel = g_sel[..., 1, :].reshape(B, qb, MOBA_KV_HEADS, MOBA_GQ, n_sel, HEAD_DIM)
        return (jnp.einsum('bqgjm,bqgjmd->bqgjd', p[..., :n_sel], v_sel)
                + jnp.einsum('bqgjl,bqgld->bqgjd', p[..., n_sel:], g_own[..., 1, :]))

    o = run_blocks(blk, nb, qb, T)
    return o.reshape(B, T, MOBA_HEADS * HEAD_DIM).astype(qr.dtype)


def kernel(x_prompt, x_sample, cache_nsa_cmp_kv, cache_nsa_sel_kv, state_nsa_win_kv, state_ssm, state_conv,
           cache_moba_kv, page_table, norm_mix, norm_ffn, norm_final, w_in_even, w_out_even,
           ssm_conv_w, ssm_conv_b, ssm_dt_bias, ssm_a_log, ssm_d_skip, ssm_norm,
           nsa_cmp_pos, nsa_cmp_w1, nsa_cmp_b1, nsa_cmp_w2, nsa_cmp_b2, nsa_gate_b,
           w_in_odd, w_out_odd, peer_wq, peer_subkeys, peer_u, peer_v):

    peer_wq_bf16 = peer_wq.astype(jnp.bfloat16)
    peer_u_bf16 = peer_u.astype(jnp.bfloat16)
    peer_vt_bf16 = jnp.transpose(peer_v, (0, 2, 1)).astype(jnp.bfloat16)
    w_even_bf16 = [relayout_even_weight(w_in_even[e]) for e in range(w_in_even.shape[0])]
    w_odd_bf16 = w_in_odd.astype(jnp.bfloat16)

    def trunk(x, pos0, get_past, prompt):
        B, T, _ = x.shape
        pos_q = pos0 + jnp.arange(T, dtype=jnp.int32)
        new_cmp, new_sel, new_win, new_ssm, new_conv, new_moba = [], [], [], [], [], []
        tabs = rope_tables(pos_q)
        if T % ATTN_TILE != 0:
            tabs = tuple(jnp.tile(a, (B, 1)) for a in tabs)
        for l in range(DEPTH):
            x2 = x.reshape(B * T, D_MODEL)
            if l % 2 == 0:
                e = l // 2
                z, xbc, q, qr, kvc, kvs, kvw, gd = layer_projection(
                    x2, norm_mix[l], w_even_bf16[e], tabs, EVEN_OUT_WIDTHS, _even_proj_kernel)
                kv_shape = (B, T, 2, NSA_KV_HEADS, HEAD_DIM)
                kvc, kvs, kvw = kvc.reshape(kv_shape), kvs.reshape(kv_shape), kvw.reshape(kv_shape)
                gts = gd[:, :3 * NSA_HEADS].reshape(B, T, 3 * NSA_HEADS) + nsa_gate_b[e]
                kvw_ext = jnp.concatenate([get_past('win', e), kvw], axis=1)
                if prompt:
                    conv_past = get_past('conv', e)
                    y_ssm, ssm_new = ssd_mixer(z, xbc, gd, conv_past, get_past('ssm', e), ssm_conv_w[e], ssm_conv_b[e],
                                               ssm_dt_bias[e], ssm_a_log[e], ssm_d_skip[e], ssm_norm[e], B)
                    conv_new = jnp.concatenate([conv_past, xbc.reshape(B, T, SSM_CONV_DIM)], axis=1)[:, T:]
                    o_cmp, sel = nsa_compressed_branch(q, kvc.reshape(B * T, N_KV), B, nsa_cmp_pos[e], nsa_cmp_w1[e],
                                                       nsa_cmp_b1[e], nsa_cmp_w2[e], nsa_cmp_b2[e])
                    o_sel = flash_attention_t(qr, kvs.reshape(B * T, N_KV), 'sel', B, sel)
                    o_win = flash_attention_t(qr, kvw.reshape(B * T, N_KV), 'win', B)
                    y_nsa = nsa_gate_combine(gts, o_cmp, o_sel, o_win)
                    y_ssm = y_ssm.reshape(B, T, SSM_D_INNER)
                else:
                    y_ssm, conv_new, ssm_new = mamba2_ssd(
                        z.reshape(B, T, -1), xbc.reshape(B, T, -1), gd[:, 3 * NSA_HEADS:].reshape(B, T, SSM_HEADS),
                        get_past('conv', e), get_past('ssm', e), ssm_conv_w[e], ssm_conv_b[e], ssm_dt_bias[e],
                        ssm_a_log[e], ssm_d_skip[e], ssm_norm[e])
                    y_nsa = nsa_attention(q.reshape(B, T, NSA_HEADS, HEAD_DIM), qr,
                                          jnp.concatenate([get_past('cmp', e), kvc], axis=1),
                                          jnp.concatenate([get_past('sel', e), kvs], axis=1),
                                          kvw_ext, gts, pos0,
                                          nsa_cmp_pos[e], nsa_cmp_w1[e], nsa_cmp_b1[e], nsa_cmp_w2[e], nsa_cmp_b2[e])
                cat = jnp.concatenate([y_ssm, y_nsa], axis=-1).reshape(B * T, -1)
                mix = norm_matmul(cat, w_out_even[e]).reshape(B, T, D_MODEL)
                keep = min(NSA_WINDOW, kvw_ext.shape[1])
                new_cmp.append(kvc)
                new_sel.append(kvs)
                new_win.append(kvw_ext[:, kvw_ext.shape[1] - keep:])
                new_ssm.append(ssm_new)
                new_conv.append(conv_new)
            else:
                o = l // 2
                q, kv2 = layer_projection(x2, norm_mix[l], w_odd_bf16[o], tabs, ODD_OUT_WIDTHS, _odd_proj_kernel)
                kv = kv2.reshape(B, T, 2, MOBA_KV_HEADS, HEAD_DIM)
                if prompt:
                    y_moba = flash_attention_t(q, kv2, 'moba', B)
                else:
                    y_moba = moba_attention(q.reshape(B, T, MOBA_HEADS, HEAD_DIM),
                                            jnp.concatenate([get_past('moba', o), kv], axis=1), pos0)
                mix = norm_matmul(y_moba.reshape(B * T, -1), w_out_odd[o]).reshape(B, T, D_MODEL)
                new_moba.append(kv)
            x = x + mix
            x = peer_ffn_dense(x.reshape(B * T, D_MODEL), norm_ffn[l], peer_wq_bf16[l], peer_subkeys[l],
                               peer_u_bf16[l], peer_vt_bf16[l]).reshape(B, T, D_MODEL)
        return (rmsnorm(x, norm_final), jnp.stack(new_cmp), jnp.stack(new_sel), jnp.stack(new_win),
                jnp.stack(new_ssm), jnp.stack(new_conv), jnp.stack(new_moba))

    bp = x_prompt.shape[0]
    dtp = x_prompt.dtype

    def prompt_past(kind, i):
        if kind == 'ssm':
            return jnp.zeros((bp, SSM_HEADS, SSM_HEAD_DIM, SSM_STATE), state_ssm.dtype)
        if kind == 'conv':
            return jnp.zeros((bp, SSM_CONV - 1, SSM_CONV_DIM), dtp)
        if kind == 'moba':
            return jnp.zeros((bp, 0, 2, MOBA_KV_HEADS, HEAD_DIM), dtp)
        return jnp.zeros((bp, 0, 2, NSA_KV_HEADS, HEAD_DIM), dtp)

    def sample_past(kind, i):
        if kind == 'ssm':
            return state_ssm[i]
        if kind == 'conv':
            return state_conv[i]
        if kind == 'win':
            return state_nsa_win_kv[i]
        if kind == 'cmp':
            return gather_pages(cache_nsa_cmp_kv[i], page_table)
        if kind == 'sel':
            return gather_pages(cache_nsa_sel_kv[i], page_table)
        return gather_pages(cache_moba_kv[i], page_table)

    past_len = page_table.shape[1] * cache_nsa_cmp_kv.shape[2]
    y_prompt, p_cmp, p_sel, p_win, p_ssm, p_conv, p_moba = trunk(x_prompt, 0, prompt_past, True)
    y_sample, s_cmp, s_sel, s_win, s_ssm, s_conv, s_moba = trunk(x_sample, past_len, sample_past, False)
    return (y_prompt, y_sample, p_cmp, p_sel, p_win, p_ssm, p_conv, p_moba,
            s_cmp, s_sel, s_win, s_ssm, s_conv, s_moba)
```

```python
import functools
import math

import jax
import jax.numpy as jnp
from jax import lax
from jax.experimental import pallas as pl
from jax.experimental.pallas import tpu as pltpu

D_MODEL = 1024
DEPTH = 4
HEAD_DIM = 64
ROT_DIM = HEAD_DIM // 4
ROPE_THETA = 500000.0
ATTN_SCALE = HEAD_DIM ** -0.5
NORM_EPS = 1e-6
NEG_INF = -1e30

SSM_HEADS = 16
SSM_HEAD_DIM = 64
SSM_D_INNER = SSM_HEADS * SSM_HEAD_DIM
SSM_GROUPS = 2
SSM_STATE = 128
SSM_CONV = 4
SSM_CONV_DIM = SSM_D_INNER + 2 * SSM_GROUPS * SSM_STATE
SSD_CHUNK = 128

NSA_HEADS = 16
NSA_KV_HEADS = 4
NSA_GQ = NSA_HEADS // NSA_KV_HEADS
NSA_CMP_BLOCK = 32
NSA_CMP_STRIDE = 16
NSA_SEL_BLOCK = 64
NSA_TOPN = 8
NSA_LOCAL_BLOCKS = 2
NSA_FORCE = 1e6
NSA_WINDOW = 512

MOBA_HEADS = 16
MOBA_KV_HEADS = 4
MOBA_GQ = MOBA_HEADS // MOBA_KV_HEADS
MOBA_BLOCK = 256
MOBA_TOPK = 3

PEER_HEADS = 8
PEER_N_KEYS = 128
PEER_KEY_DIM = 256
PEER_TOPK = 16

EVEN_WIDTHS = (SSM_D_INNER, SSM_CONV_DIM, SSM_HEADS, NSA_HEADS * HEAD_DIM,
               2 * NSA_KV_HEADS * HEAD_DIM, 2 * NSA_KV_HEADS * HEAD_DIM, 2 * NSA_KV_HEADS * HEAD_DIM,
               3 * NSA_HEADS)
ODD_WIDTHS = (MOBA_HEADS * HEAD_DIM, 2 * MOBA_KV_HEADS * HEAD_DIM)

WIN_QBLOCK = 128
SEL_QBLOCK = 16
MOBA_QBLOCK = 4
PEER_TBLOCK = 128

VMEM_LIMIT_BYTES = 48 * 1024 * 1024


ROW_TILE = 256


def _row_call(body, m, row_inputs, fixed_inputs, n_out, name):
    tm = min(m, ROW_TILE)
    assert m % tm == 0
    return pl.pallas_call(
        body,
        grid=(m // tm,),
        in_specs=[pl.BlockSpec((tm, a.shape[1]), lambda i: (i, 0)) for a in row_inputs]
                 + [pl.BlockSpec(a.shape, lambda i: (0, 0)) for a in fixed_inputs],
        out_specs=pl.BlockSpec((tm, n_out), lambda i: (i, 0)),
        out_shape=jax.ShapeDtypeStruct((m, n_out), jnp.float32),
        compiler_params=pltpu.CompilerParams(dimension_semantics=("arbitrary",),
                                             vmem_limit_bytes=VMEM_LIMIT_BYTES),
        name=name,
    )(*row_inputs, *fixed_inputs)


def _odd_out_kernel(x_ref, y_ref, w_ref, o_ref):
    o_ref[...] = x_ref[...] + jnp.dot(y_ref[...].astype(jnp.bfloat16), w_ref[...], preferred_element_type=jnp.float32)


def _even_out_kernel(x_ref, ssm_ref, cmp_ref, sel_ref, win_ref, gd_ref, gb_ref, ex_ref, wa_ref, wb_ref, o_ref):
    f32, bf16 = jnp.float32, jnp.bfloat16
    gates = jax.nn.sigmoid(gd_ref[...] + gb_ref[...])
    y_nsa = None
    for c, branch in enumerate((cmp_ref, sel_ref, win_ref)):
        g_c = jnp.dot(gates, ex_ref[c], precision=lax.Precision.HIGHEST, preferred_element_type=f32)
        term = g_c * branch[...]
        y_nsa = term if y_nsa is None else y_nsa + term
    o_ref[...] = (x_ref[...] + jnp.dot(ssm_ref[...].astype(bf16), wa_ref[...], preferred_element_type=f32)
                  + jnp.dot(y_nsa.astype(bf16), wb_ref[...], preferred_element_type=f32))


def _final_norm_kernel(x_ref, g_ref, o_ref):
    x = x_ref[...]
    o_ref[...] = x * lax.rsqrt(jnp.mean(x * x, axis=-1, keepdims=True) + NORM_EPS) * g_ref[...]


def even_output(x2, y_ssm, o_cmp, o_sel, o_win, gd, gate_b, w_out):
    f32 = jnp.float32
    n_g = 3 * NSA_HEADS
    gb = jnp.concatenate([gate_b.astype(f32), jnp.zeros((N_GD - n_g,), f32)]).reshape(1, N_GD)
    lane_head = jnp.arange(N_Q) // HEAD_DIM
    expand = jnp.stack([(jnp.arange(N_GD)[:, None] == 3 * lane_head[None, :] + c).astype(f32) for c in range(3)])
    w = w_out.astype(jnp.bfloat16)
    m = x2.shape[0]
    tm = min(m, ROW_TILE)
    row = lambda a: pl.BlockSpec((tm, a.shape[1]), lambda i: (i, 0))
    whole = lambda a: pl.BlockSpec(a.shape, lambda i: (0,) * a.ndim)
    rows = (x2, y_ssm, o_cmp, o_sel, o_win, gd)
    fixed = (gb, expand, w[:SSM_D_INNER], w[SSM_D_INNER:])
    return pl.pallas_call(
        _even_out_kernel,
        grid=(m // tm,),
        in_specs=[row(a) for a in rows] + [whole(a) for a in fixed],
        out_specs=pl.BlockSpec((tm, D_MODEL), lambda i: (i, 0)),
        out_shape=jax.ShapeDtypeStruct((m, D_MODEL), f32),
        compiler_params=pltpu.CompilerParams(dimension_semantics=("arbitrary",),
                                             vmem_limit_bytes=VMEM_LIMIT_BYTES),
        name="even_out",
    )(*rows, *fixed)


def odd_output(x2, y, w_out):
    return _row_call(_odd_out_kernel, x2.shape[0], (x2, y), (w_out.astype(jnp.bfloat16),), D_MODEL, "odd_out")


def final_norm(x2, g):
    return _row_call(_final_norm_kernel, x2.shape[0], (x2,), (g.reshape(1, -1).astype(jnp.float32),), D_MODEL,
                     "final_norm")


LANES = 128
ROT_HALF = ROT_DIM // 2
PROJ_VMEM_LIMIT_BYTES = 56 * 1024 * 1024
N_Q = NSA_HEADS * HEAD_DIM
N_KV = 2 * NSA_KV_HEADS * HEAD_DIM
EVEN_OFF_XBC = SSM_D_INNER
EVEN_OFF_Q = EVEN_OFF_XBC + SSM_CONV_DIM
EVEN_OFF_KVC = EVEN_OFF_Q + N_Q
EVEN_OFF_KVS = EVEN_OFF_KVC + N_KV
EVEN_OFF_KVW = EVEN_OFF_KVS + N_KV
EVEN_OFF_GD = EVEN_OFF_KVW + N_KV
N_GD = 3 * NSA_HEADS + SSM_HEADS
EVEN_TOTAL = EVEN_OFF_GD + N_GD


def rope_tables(pos):
    f32 = jnp.float32
    inv = ROPE_THETA ** (-jnp.arange(0, ROT_DIM, 2, dtype=f32) / ROT_DIM)
    ang = pos.astype(f32)[:, None] * inv[None, :]
    cos, sin = jnp.cos(ang), jnp.sin(ang)
    r = pos.shape[0]
    zeros = lambda n: jnp.zeros((r, n), f32)
    c = jnp.concatenate([cos, cos, jnp.ones((r, HEAD_DIM - ROT_DIM), f32)], axis=1)
    sm = jnp.concatenate([-sin, zeros(HEAD_DIM - ROT_HALF)], axis=1)
    sp = jnp.concatenate([zeros(ROT_HALF), sin, zeros(HEAD_DIM - ROT_DIM)], axis=1)
    rep = LANES // HEAD_DIM
    return tuple(jnp.tile(a, (1, rep)) for a in (c, sm, sp))


def _rope_lanes(x, c, sm, sp):
    w = x.shape[1]
    reps = w // LANES
    tile = lambda a: jnp.concatenate([a] * reps, axis=1) if reps > 1 else a
    return (x * tile(c) + pltpu.roll(x, w - ROT_HALF, axis=1) * tile(sm)
            + pltpu.roll(x, ROT_HALF, axis=1) * tile(sp))


def _rope_keys_lanes(kv, c, sm, sp):
    half = kv.shape[1] // 2
    return jnp.concatenate([_rope_lanes(kv[:, :half], c, sm, sp), kv[:, half:]], axis=1)


def _normed_dot(x_ref, g_ref, w_ref):
    x = x_ref[...]
    xn = x * lax.rsqrt(jnp.mean(x * x, axis=-1, keepdims=True) + NORM_EPS) * g_ref[...]
    return jnp.dot(xn.astype(jnp.bfloat16), w_ref[...], preferred_element_type=jnp.float32)


def _even_proj_kernel(x_ref, g_ref, w_ref, c_ref, sm_ref, sp_ref,
                      z_ref, xbc_ref, q_ref, qr_ref, kvc_ref, kvs_ref, kvw_ref, gd_ref):
    o = _normed_dot(x_ref, g_ref, w_ref)
    rope = (c_ref[...], sm_ref[...], sp_ref[...])
    z_ref[...] = o[:, :EVEN_OFF_XBC]
    xbc_ref[...] = o[:, EVEN_OFF_XBC:EVEN_OFF_Q]
    q = o[:, EVEN_OFF_Q:EVEN_OFF_KVC]
    q_ref[...] = q
    qr_ref[...] = _rope_lanes(q, *rope)
    kvc_ref[...] = o[:, EVEN_OFF_KVC:EVEN_OFF_KVS]
    kvs_ref[...] = _rope_keys_lanes(o[:, EVEN_OFF_KVS:EVEN_OFF_KVW], *rope)
    kvw_ref[...] = _rope_keys_lanes(o[:, EVEN_OFF_KVW:EVEN_OFF_GD], *rope)
    gd_ref[...] = o[:, EVEN_OFF_GD:EVEN_TOTAL]


def _odd_proj_kernel(x_ref, g_ref, w_ref, c_ref, sm_ref, sp_ref, qr_ref, kv_ref):
    o = _normed_dot(x_ref, g_ref, w_ref)
    rope = (c_ref[...], sm_ref[...], sp_ref[...])
    qr_ref[...] = _rope_lanes(o[:, :N_Q], *rope)
    kv_ref[...] = _rope_keys_lanes(o[:, N_Q:], *rope)


def relayout_even_weight(w):
    dt0 = SSM_D_INNER + SSM_CONV_DIM
    return jnp.concatenate([w[:, :dt0], w[:, dt0 + SSM_HEADS:], w[:, dt0:dt0 + SSM_HEADS]], axis=1).astype(jnp.bfloat16)


def layer_projection(x2, g, w_bf16, rope_tabs, widths, body):
    m, k = x2.shape
    r = rope_tabs[0].shape[0]
    tm = min(m, r, 256)
    assert m % tm == 0 and r % tm == 0
    nr = r // tm
    row = lambda i: (i, 0)
    fixed = lambda i: (0, 0)
    tab = pl.BlockSpec((tm, LANES), lambda i: (i % nr, 0))
    return pl.pallas_call(
        body,
        grid=(m // tm,),
        in_specs=[pl.BlockSpec((tm, k), row), pl.BlockSpec((1, k), fixed), pl.BlockSpec(w_bf16.shape, fixed),
                  tab, tab, tab],
        out_specs=[pl.BlockSpec((tm, n), row) for n in widths],
        out_shape=[jax.ShapeDtypeStruct((m, n), jnp.float32) for n in widths],
        compiler_params=pltpu.CompilerParams(dimension_semantics=("arbitrary",),
                                             vmem_limit_bytes=PROJ_VMEM_LIMIT_BYTES),
        name=body.__name__.strip("_"),
    )(x2, g.reshape(1, k).astype(jnp.float32), w_bf16, *rope_tabs)


EVEN_OUT_WIDTHS = (SSM_D_INNER, SSM_CONV_DIM, N_Q, N_Q, N_KV, N_KV, N_KV, N_GD)
ODD_OUT_WIDTHS = (N_Q, N_KV)


def _ssd_kernel(z_ref, xbc_ref, gd_ref, dtt_ref, conv0_ref, h0_ref, cw_ref, cb_ref, dtb_ref, dtbt_ref,
                alog_ref, alogt_ref, dskip_ref, ng_ref, y_ref, hlast_ref, h_scr, xw_scr):
    f32, bf16 = jnp.float32, jnp.bfloat16
    hi = lax.Precision.HIGHEST
    Q = SSD_CHUNK
    c_idx = pl.program_id(1)
    tail = 8

    @pl.when(c_idx == 0)
    def _():
        h_scr[...] = h0_ref[0]
        xw_scr[0:tail, :] = conv0_ref[0]

    xw_scr[tail:tail + Q, :] = xbc_ref[...]
    conv = cb_ref[...]
    for k in range(SSM_CONV):
        start = tail - (SSM_CONV - 1) + k
        conv = conv + xw_scr[start:start + Q, :] * cw_ref[k:k + 1, :]
    xw_scr[0:tail, :] = xw_scr[Q:Q + tail, :]
    xc = conv * jax.nn.sigmoid(conv)
    xs = xc[:, :SSM_D_INNER]
    n_bc = SSM_GROUPS * SSM_STATE
    bm = xc[:, SSM_D_INNER:SSM_D_INNER + n_bc]
    cm = xc[:, SSM_D_INNER + n_bc:]

    dt = jax.nn.softplus(gd_ref[:, 3 * NSA_HEADS:] + dtb_ref[...])
    dtt = jax.nn.softplus(dtt_ref[0] + dtbt_ref[...])
    da = dt * (-jnp.exp(alog_ref[...]))
    dat = dtt * (-jnp.exp(alogt_ref[...]))
    ri = lax.broadcasted_iota(jnp.int32, (Q, Q), 0)
    ci = lax.broadcasted_iota(jnp.int32, (Q, Q), 1)
    causal = ci <= ri
    acum = jnp.dot(jnp.where(causal, 1.0, 0.0), da, precision=hi, preferred_element_type=f32)
    acumt = jnp.dot(dat, jnp.where(ri <= ci, 1.0, 0.0), precision=hi, preferred_element_type=f32)
    tot_t = acumt[:, Q - 1:Q]
    hh = lax.broadcasted_iota(jnp.int32, (SSM_HEADS, SSM_D_INNER), 0)
    ch = lax.broadcasted_iota(jnp.int32, (SSM_HEADS, SSM_D_INNER), 1) // SSM_HEAD_DIM
    expand = jnp.where(hh == ch, 1.0, 0.0)
    xdt = xs * jnp.dot(dt, expand, precision=hi, preferred_element_type=f32)
    xst = xs.T
    rep = SSM_HEADS // SSM_GROUPS
    ys = []
    for g in range(SSM_GROUPS):
        b_g = bm[:, g * SSM_STATE:(g + 1) * SSM_STATE]
        c_g = cm[:, g * SSM_STATE:(g + 1) * SSM_STATE]
        b_bf, c_bf = b_g.astype(bf16), c_g.astype(bf16)
        cb = lax.dot_general(c_bf, b_bf, (((1,), (1,)), ((), ())), preferred_element_type=f32)
        for h in range(g * rep, (g + 1) * rep):
            p0 = h * SSM_HEAD_DIM
            acol = acum[:, h:h + 1]
            arow = acumt[h:h + 1, :]
            decay = jnp.exp(jnp.where(causal, acol - arow, NEG_INF))
            y_h = jnp.dot((cb * decay).astype(bf16), xdt[:, p0:p0 + SSM_HEAD_DIM].astype(bf16),
                          preferred_element_type=f32)
            h_old = h_scr[h]
            y_h = y_h + jnp.exp(acol) * lax.dot_general(c_bf, h_old.astype(bf16), (((1,), (1,)), ((), ())),
                                                        preferred_element_type=f32)
            ys.append(y_h)
            w_row = dtt[h:h + 1, :] * jnp.exp(tot_t[h:h + 1, :] - arow)
            xdt_t = (xst[p0:p0 + SSM_HEAD_DIM, :] * w_row).astype(bf16)
            h_scr[h] = h_old * jnp.exp(tot_t[h:h + 1, :]) + jnp.dot(xdt_t, b_bf, preferred_element_type=f32)
    y = jnp.concatenate(ys, axis=1) + dskip_ref[...] * xs
    zz = z_ref[...]
    y = y * (zz * jax.nn.sigmoid(zz))
    gw = SSM_D_INNER // SSM_GROUPS
    outs = []
    for g in range(SSM_GROUPS):
        yg = y[:, g * gw:(g + 1) * gw]
        outs.append(yg * lax.rsqrt(jnp.mean(yg * yg, axis=-1, keepdims=True) + NORM_EPS))
    y_ref[...] = jnp.concatenate(outs, axis=1) * ng_ref[...]

    @pl.when(c_idx == pl.num_programs(1) - 1)
    def _():
        hlast_ref[0] = h_scr[...]


def ssd_mixer(z, xbc, gd, conv_state, ssm_state, conv_w, conv_b, dt_bias, a_log, d_skip, norm_g, batch):
    m = z.shape[0]
    t = m // batch
    assert t % SSD_CHUNK == 0
    nc = t // SSD_CHUNK
    f32 = jnp.float32
    dtt = jnp.transpose(gd[:, 3 * NSA_HEADS:].reshape(batch, t, SSM_HEADS), (0, 2, 1))
    conv0 = jnp.pad(conv_state.astype(f32), ((0, 0), (8 - (SSM_CONV - 1), 0), (0, 0)))
    row = lambda b, c: (b * nc + c, 0)
    fixed2 = lambda b, c: (0, 0)
    vec = lambda a: a.reshape(1, -1).astype(f32)
    col = lambda a: a.reshape(-1, 1).astype(f32)
    y, h_last = pl.pallas_call(
        _ssd_kernel,
        grid=(batch, nc),
        in_specs=[pl.BlockSpec((SSD_CHUNK, SSM_D_INNER), row),
                  pl.BlockSpec((SSD_CHUNK, SSM_CONV_DIM), row),
                  pl.BlockSpec((SSD_CHUNK, N_GD), row),
                  pl.BlockSpec((1, SSM_HEADS, SSD_CHUNK), lambda b, c: (b, 0, c)),
                  pl.BlockSpec((1, 8, SSM_CONV_DIM), lambda b, c: (b, 0, 0)),
                  pl.BlockSpec((1, SSM_HEADS, SSM_HEAD_DIM, SSM_STATE), lambda b, c: (b, 0, 0, 0)),
                  pl.BlockSpec((SSM_CONV, SSM_CONV_DIM), fixed2),
                  pl.BlockSpec((1, SSM_CONV_DIM), fixed2),
                  pl.BlockSpec((1, SSM_HEADS), fixed2), pl.BlockSpec((SSM_HEADS, 1), fixed2),
                  pl.BlockSpec((1, SSM_HEADS), fixed2), pl.BlockSpec((SSM_HEADS, 1), fixed2),
                  pl.BlockSpec((1, SSM_D_INNER), fixed2), pl.BlockSpec((1, SSM_D_INNER), fixed2)],
        out_specs=[pl.BlockSpec((SSD_CHUNK, SSM_D_INNER), row),
                   pl.BlockSpec((1, SSM_HEADS, SSM_HEAD_DIM, SSM_STATE), lambda b, c: (b, 0, 0, 0))],
        out_shape=[jax.ShapeDtypeStruct((m, SSM_D_INNER), f32),
                   jax.ShapeDtypeStruct((batch, SSM_HEADS, SSM_HEAD_DIM, SSM_STATE), f32)],
        scratch_shapes=[pltpu.VMEM((SSM_HEADS, SSM_HEAD_DIM, SSM_STATE), f32),
                        pltpu.VMEM((SSD_CHUNK + 8, SSM_CONV_DIM), f32)],
        compiler_params=pltpu.CompilerParams(dimension_semantics=("arbitrary", "arbitrary"),
                                             vmem_limit_bytes=VMEM_LIMIT_BYTES),
        name="ssd_mixer",
    )(z, xbc, gd, dtt, conv0, ssm_state.astype(f32), conv_w.astype(f32), vec(conv_b),
      vec(dt_bias), col(dt_bias), vec(a_log), col(a_log),
      vec(jnp.repeat(d_skip, SSM_HEAD_DIM)), vec(norm_g))
    return y, h_last


def _nsa_cmp_kernel(q_ref, x_ref, pe_ref, w1_ref, b1_ref, w2_ref, b2_ref, ov_ref, o_ref, sel_ref, cmp_scr):
    f32, bf16 = jnp.float32, jnp.bfloat16
    i = pl.program_id(1)
    tq = q_ref.shape[1]
    n_seg = x_ref.shape[2]
    n_sel = ov_ref.shape[1]
    n_cg = 2 * NSA_KV_HEADS

    @pl.when(i == 0)
    def _():
        for cg in range(n_cg):
            c = cg // NSA_KV_HEADS
            x = x_ref[0, cg]
            a0 = jnp.dot((x + pe_ref[c, 0]).astype(bf16), w1_ref[c, 0], preferred_element_type=f32)
            a1 = jnp.dot((x + pe_ref[c, 1]).astype(bf16), w1_ref[c, 1], preferred_element_type=f32)
            hid = jax.nn.gelu(b1_ref[c] + a0 + pltpu.roll(a1, n_seg - 1, axis=0))
            cmp_scr[cg] = jnp.dot(hid.astype(bf16), w2_ref[c], preferred_element_type=f32) + b2_ref[c]

    q = q_ref[0]
    tpos = i * tq + lax.broadcasted_iota(jnp.int32, (tq, 1), 0)
    n_io = lax.broadcasted_iota(jnp.int32, (1, n_seg), 1)
    cmp_ok = (n_io * NSA_CMP_STRIDE + NSA_CMP_BLOCK - 1 <= tpos) & (n_io < n_seg - 1)
    sidx = lax.broadcasted_iota(jnp.int32, (1, n_sel), 1)
    blk_t = tpos // NSA_SEL_BLOCK
    valid = sidx <= blk_t
    forced = valid & ((sidx == 0) | (sidx > blk_t - NSA_LOCAL_BLOCKS))
    eye = jnp.where(lax.broadcasted_iota(jnp.int32, (n_sel, n_sel), 0)
                    == lax.broadcasted_iota(jnp.int32, (n_sel, n_sel), 1), 1.0, 0.0)
    outs = []
    for g in range(NSA_KV_HEADS):
        kc = cmp_scr[g].astype(bf16)
        vc = cmp_scr[NSA_KV_HEADS + g].astype(bf16)
        psum = jnp.zeros((tq, n_seg), f32)
        for j in range(NSA_GQ):
            h0 = (g * NSA_GQ + j) * HEAD_DIM
            hq = (q[:, h0:h0 + HEAD_DIM] * ATTN_SCALE).astype(bf16)
            s = lax.dot_general(hq, kc, (((1,), (1,)), ((), ())), preferred_element_type=f32)
            s = jnp.where(cmp_ok, s, NEG_INF)
            p = jnp.where(cmp_ok, jnp.exp(s - jnp.max(s, axis=-1, keepdims=True)), 0.0)
            p = p / jnp.maximum(jnp.sum(p, axis=-1, keepdims=True), 1e-30)
            outs.append(jnp.dot(p.astype(bf16), vc, preferred_element_type=f32))
            psum = psum + p
        imp = jnp.dot(psum, ov_ref[...], precision=lax.Precision.HIGHEST, preferred_element_type=f32)
        imp = jnp.where(forced, NSA_FORCE, imp)
        imp = jnp.where(valid, imp, NEG_INF)
        sel = jnp.zeros((tq, n_sel), f32)
        for _ in range(min(NSA_TOPN, n_sel)):
            mx = jnp.max(imp, axis=-1, keepdims=True)
            first = jnp.min(jnp.where(imp == mx, sidx, n_sel), axis=-1, keepdims=True)
            hit = sidx == first
            sel = jnp.where(hit & (mx > 0.5 * NEG_INF), 1.0, sel)
            imp = jnp.where(hit, REMOVED, imp)
        sel_ref[0, g] = lax.dot_general(eye, sel, (((1,), (1,)), ((), ())), preferred_element_type=f32)
    o_ref[0] = jnp.concatenate(outs, axis=1)


def nsa_compressed_branch(q, kvc, batch, cmp_pos, w1, b1, w2, b2):
    f32, bf16 = jnp.float32, jnp.bfloat16
    t = q.shape[0] // batch
    tq = ATTN_TILE
    S = NSA_CMP_STRIDE
    r = NSA_CMP_BLOCK // S
    assert t % tq == 0 and r == 2
    n_seg, n_sel = t // S, t // NSA_SEL_BLOCK
    n_cg = 2 * NSA_KV_HEADS
    x = jnp.transpose(kvc.reshape(batch, n_seg, S, n_cg, HEAD_DIM), (0, 3, 1, 2, 4)).reshape(batch, n_cg, n_seg, S * HEAD_DIM)
    pe = cmp_pos.reshape(2, r, 1, S * HEAD_DIM).astype(f32)
    w1r = w1.reshape(2, r, S * HEAD_DIM, -1).astype(bf16)
    hid = w1r.shape[-1]
    cmp_start = S * jnp.arange(n_seg, dtype=jnp.int32)
    sel_start = NSA_SEL_BLOCK * jnp.arange(n_sel, dtype=jnp.int32)
    overlap = ((cmp_start[:, None] < sel_start[None, :] + NSA_SEL_BLOCK)
               & (cmp_start[:, None] + NSA_CMP_BLOCK > sel_start[None, :])).astype(f32)
    full = lambda a: pl.BlockSpec(a.shape, lambda b, i: (0,) * a.ndim)
    b1r, b2r, w2r = b1.reshape(2, 1, hid).astype(f32), b2.reshape(2, 1, HEAD_DIM).astype(f32), w2.astype(bf16)
    o_cmp, sel = pl.pallas_call(
        _nsa_cmp_kernel,
        grid=(batch, t // tq),
        in_specs=[pl.BlockSpec((1, tq, N_Q), lambda b, i: (b, i, 0)),
                  pl.BlockSpec((1, n_cg, n_seg, S * HEAD_DIM), lambda b, i: (b, 0, 0, 0)),
                  full(pe), full(w1r), full(b1r), full(w2r), full(b2r), full(overlap)],
        out_specs=[pl.BlockSpec((1, tq, N_Q), lambda b, i: (b, i, 0)),
                   pl.BlockSpec((1, NSA_KV_HEADS, n_sel, tq), lambda b, i: (b, 0, 0, i))],
        out_shape=[jax.ShapeDtypeStruct((batch, t, N_Q), f32),
                   jax.ShapeDtypeStruct((batch, NSA_KV_HEADS, n_sel, t), f32)],
        scratch_shapes=[pltpu.VMEM((n_cg, n_seg, HEAD_DIM), f32)],
        compiler_params=pltpu.CompilerParams(dimension_semantics=("arbitrary", "arbitrary"),
                                             vmem_limit_bytes=VMEM_LIMIT_BYTES),
        name="nsa_cmp",
    )(q.reshape(batch, t, N_Q), x, pe, w1r, b1r, w2r, b2r, overlap)
    return o_cmp, sel


ATTN_TILE = 256


def _flash_t_kernel(*refs, mode, n_kblocks):
    f32, bf16 = jnp.float32, jnp.bfloat16
    if mode == 'sel':
        q_ref, k_ref, v_ref, sel_ref, o_ref, qt_scr, m_scr, l_scr, acc_scr = refs
    elif mode == 'moba':
        q_ref, k_ref, v_ref, o_ref, qt_scr, m_scr, l_scr, acc_scr, selm_scr = refs
    else:
        q_ref, k_ref, v_ref, o_ref, qt_scr, m_scr, l_scr, acc_scr = refs
    tq = tk = ATTN_TILE
    D = HEAD_DIM
    i = pl.program_id(2)
    nh = q_ref.shape[2] // D
    gq = nh // 2
    qt = q_ref[0].T
    qt_scr[...] = (qt * ATTN_SCALE).astype(bf16)
    m_scr[...] = jnp.full(m_scr.shape, NEG_INF, f32)
    l_scr[...] = jnp.zeros(l_scr.shape, f32)
    acc_scr[...] = jnp.zeros(acc_scr.shape, f32)
    tpos = i * tq + lax.broadcasted_iota(jnp.int32, (1, tq), 1)
    krow = lax.broadcasted_iota(jnp.int32, (tk, 1), 0)
    brow = lax.broadcasted_iota(jnp.int32, (n_kblocks, tq), 0)

    if mode == 'moba':
        kmean = jnp.mean(k_ref[0].reshape(n_kblocks, tk, 2 * D), axis=1)
        valid = brow < i
        for jj in range(nh):
            gg = jj // gq
            gate = jnp.dot(kmean[:, gg * D:(gg + 1) * D], qt[jj * D:(jj + 1) * D, :],
                           precision=lax.Precision.HIGHEST, preferred_element_type=f32)
            gate = jnp.where(valid, gate, NEG_INF)
            sel = jnp.zeros((n_kblocks, tq), f32)
            for _ in range(min(MOBA_TOPK, n_kblocks)):
                mx = jnp.max(gate, axis=0, keepdims=True)
                first = jnp.min(jnp.where(gate == mx, brow, n_kblocks), axis=0, keepdims=True)
                hit = brow == first
                sel = jnp.where(hit, 1.0, sel)
                gate = jnp.where(hit, REMOVED, gate)
            selm_scr[jj] = jnp.where(valid, sel, 0.0)

    def body(n, carry):
        off = pl.multiple_of(n * tk, tk)
        kblk = k_ref[0, pl.ds(off, tk), :]
        vblk_t = v_ref[0, pl.ds(off, tk), :].T
        kpos = n * tk + krow
        base = kpos <= tpos
        if mode == 'win':
            base = base & (tpos - kpos < NSA_WINDOW)
        for gg in range(2):
            kb = kblk[:, gg * D:(gg + 1) * D].astype(bf16)
            vt = vblk_t[gg * D:(gg + 1) * D, :].astype(bf16)
            mask_g = base
            if mode == 'sel':
                st = sel_ref[0, gg]
                per = tk // NSA_SEL_BLOCK
                srow = lax.broadcasted_iota(jnp.int32, (st.shape[0], 1), 0)
                pieces = []
                for r in range(per):
                    row = jnp.sum(jnp.where(srow == n * per + r, st, 0.0), axis=0, keepdims=True)
                    pieces.append(jnp.broadcast_to(row, (NSA_SEL_BLOCK, tq)))
                mask_g = base & (jnp.concatenate(pieces, axis=0) > 0.0)
            for j in range(gq):
                jj = gg * gq + j
                mask = mask_g
                if mode == 'moba':
                    row = jnp.sum(jnp.where(brow == n, selm_scr[jj], 0.0), axis=0, keepdims=True)
                    own = jnp.where(n == i, 1.0, 0.0)
                    mask = base & ((row + own) > 0.0)
                s = jnp.dot(kb, qt_scr[jj * D:(jj + 1) * D, :], preferred_element_type=f32)
                s = jnp.where(mask, s, NEG_INF)
                m_old = m_scr[jj:jj + 1, :]
                m_new = jnp.maximum(m_old, jnp.max(s, axis=0, keepdims=True))
                p = jnp.where(mask, jnp.exp(s - m_new), 0.0)
                alpha = jnp.exp(m_old - m_new)
                l_scr[jj:jj + 1, :] = alpha * l_scr[jj:jj + 1, :] + jnp.sum(p, axis=0, keepdims=True)
                acc_scr[jj * D:(jj + 1) * D, :] = (alpha * acc_scr[jj * D:(jj + 1) * D, :]
                                                   + jnp.dot(vt, p.astype(bf16), preferred_element_type=f32))
                m_scr[jj:jj + 1, :] = m_new
        return carry

    if mode == 'win':
        lo = jnp.maximum(i - (NSA_WINDOW + tk - 1) // tk, 0)
    else:
        lo = 0
    lax.fori_loop(lo, i + 1, body, 0)
    outs = [acc_scr[jj * D:(jj + 1) * D, :] / jnp.maximum(l_scr[jj:jj + 1, :], 1e-30) for jj in range(nh)]
    o_ref[0] = jnp.concatenate(outs, axis=0).T


def flash_attention_t(q, kv, mode, batch, sel_t=None):
    f32 = jnp.float32
    t = q.shape[0] // batch
    tq = ATTN_TILE
    assert t % tq == 0
    n_pairs = NSA_KV_HEADS // 2
    pw = 2 * NSA_GQ * HEAD_DIM
    q3 = q.reshape(batch, t, N_Q)
    kv3 = kv.reshape(batch, t, N_KV)
    args = [q3, kv3, kv3]
    in_specs = [pl.BlockSpec((1, tq, pw), lambda b, g, i: (b, i, g)),
                pl.BlockSpec((1, t, 2 * HEAD_DIM), lambda b, g, i: (b, 0, g)),
                pl.BlockSpec((1, t, 2 * HEAD_DIM), lambda b, g, i: (b, 0, n_pairs + g))]
    scratch = [pltpu.VMEM((pw, tq), jnp.bfloat16), pltpu.VMEM((pw // HEAD_DIM, tq), f32),
               pltpu.VMEM((pw // HEAD_DIM, tq), f32), pltpu.VMEM((pw, tq), f32)]
    if mode == 'sel':
        args.append(sel_t)
        in_specs.append(pl.BlockSpec((1, 2, sel_t.shape[2], tq), lambda b, g, i: (b, g, 0, i)))
    if mode == 'moba':
        scratch.append(pltpu.VMEM((pw // HEAD_DIM, t // tq, tq), f32))
    return pl.pallas_call(
        functools.partial(_flash_t_kernel, mode=mode, n_kblocks=t // tq),
        grid=(batch, n_pairs, t // tq),
        in_specs=in_specs,
        out_specs=pl.BlockSpec((1, tq, pw), lambda b, g, i: (b, i, g)),
        out_shape=jax.ShapeDtypeStruct((batch, t, N_Q), f32),
        scratch_shapes=scratch,
        compiler_params=pltpu.CompilerParams(dimension_semantics=("arbitrary", "arbitrary", "arbitrary"),
                                             vmem_limit_bytes=VMEM_LIMIT_BYTES),
        name="flash_" + mode,
    )(*args)


PEER_N_EXPERTS = PEER_N_KEYS * PEER_N_KEYS
PEER_SCORE_TILE = 256
PEER_TOKEN_TILE = 512
PEER_EXPERT_TILE = 1024
PEER_DENSE_KEY_ROWS = 32
REMOVED = -3e38


def _top_desc(s, k):
    outs = []
    for r in range(k):
        m = jnp.max(s, axis=0, keepdims=True)
        outs.append(m)
        if r + 1 < k:
            s = jnp.where(s == m, REMOVED, s)
    return jnp.concatenate(outs, axis=0)


def _peer_score_kernel(x_ref, g_ref, wq_ref, sk_ref, xt_ref, s1_ref, s2_ref, e1_ref, e2_ref, tau_ref):
    f32 = jnp.float32
    x = x_ref[...]
    xn = x * lax.rsqrt(jnp.mean(x * x, axis=-1, keepdims=True) + NORM_EPS) * g_ref[...]
    xt_ref[...] = xn.T.astype(jnp.bfloat16)
    q = jnp.dot(xn.astype(jnp.bfloat16), wq_ref[...], preferred_element_type=f32)
    half = PEER_KEY_DIM // 2
    taus = []
    for h in range(PEER_HEADS):
        st = []
        for c in range(2):
            qhc = q[:, (2 * h + c) * half:(2 * h + c + 1) * half]
            st.append(lax.dot_general(sk_ref[c], qhc, (((1,), (1,)), ((), ())),
                                      precision=lax.Precision.HIGHEST, preferred_element_type=f32))
        t1 = _top_desc(st[0], PEER_TOPK)
        t2 = _top_desc(st[1], PEER_TOPK)
        cand = jnp.concatenate([t1[i:i + 1] + t2[:PEER_TOPK // (i + 1)] for i in range(PEER_TOPK)], axis=0)
        tops = _top_desc(cand, PEER_TOPK)
        z = jnp.sum(jnp.exp(tops - tops[0:1]), axis=0, keepdims=True)
        taus.append(tops[PEER_TOPK - 1:PEER_TOPK])
        s1_ref[h] = st[0]
        s2_ref[h] = st[1]
        e1_ref[h] = jnp.exp(st[0] - t1[0:1]) / z
        e2_ref[h] = jnp.exp(st[1] - t2[0:1])
    tau_ref[...] = jnp.concatenate(taus, axis=0)


def _peer_dense_kernel(xres_ref, xt_ref, s1_ref, s2_ref, e1_ref, e2_ref, tau_ref, u_ref, vt_ref, o_ref,
                       act_scr, w_scr, yt_scr):
    f32 = jnp.float32
    j = pl.program_id(1)
    tm = xt_ref.shape[1]
    n_a = PEER_EXPERT_TILE // PEER_N_KEYS

    @pl.when(j == 0)
    def _():
        yt_scr[...] = jnp.zeros(yt_scr.shape, f32)

    act_scr[...] = jnp.dot(u_ref[...], xt_ref[...], preferred_element_type=f32)
    a_rows = pl.ds(pl.multiple_of(j * n_a, n_a), n_a)
    kb = PEER_DENSE_KEY_ROWS
    for aa in range(n_a):
        for tc in range(tm // LANES):
            lanes = slice(tc * LANES, (tc + 1) * LANES)
            for bq in range(PEER_N_KEYS // kb):
                rows = slice(bq * kb, (bq + 1) * kb)
                acc = jnp.zeros((kb, LANES), f32)
                for h in range(PEER_HEADS):
                    val = s2_ref[h, rows, lanes] + s1_ref[h, a_rows, lanes][aa:aa + 1]
                    gate = e2_ref[h, rows, lanes] * e1_ref[h, a_rows, lanes][aa:aa + 1]
                    acc = acc + jnp.where(val >= tau_ref[h:h + 1, lanes], gate, 0.0)
                r0 = aa * PEER_N_KEYS + bq * kb
                act = act_scr[r0:r0 + kb, lanes]
                w_scr[r0:r0 + kb, lanes] = (acc * jax.nn.gelu(act)).astype(jnp.bfloat16)
    yt_scr[...] += jnp.dot(vt_ref[...], w_scr[...], preferred_element_type=f32)

    @pl.when(j == pl.num_programs(1) - 1)
    def _():
        o_ref[...] = xres_ref[...] + yt_scr[...].T


def peer_ffn_dense(x_res, g_norm, wq_bf16, subkeys, u_bf16, vt_bf16):
    n, d = x_res.shape
    tm = PEER_TOKEN_TILE if n % PEER_TOKEN_TILE == 0 else LANES
    ts = PEER_SCORE_TILE if n % PEER_SCORE_TILE == 0 else LANES
    n_pad = -(-n // tm) * tm
    xp = jnp.pad(x_res, ((0, n_pad - n), (0, 0)))
    hk = (PEER_HEADS, PEER_N_KEYS, n_pad)
    stat_spec = lambda t: pl.BlockSpec((PEER_HEADS, PEER_N_KEYS, t), lambda i, *_: (0, 0, i))
    xt, s1, s2, e1, e2, tau = pl.pallas_call(
        _peer_score_kernel,
        grid=(n_pad // ts,),
        in_specs=[pl.BlockSpec((ts, d), lambda i: (i, 0)),
                  pl.BlockSpec((1, d), lambda i: (0, 0)),
                  pl.BlockSpec(wq_bf16.shape, lambda i: (0, 0)),
                  pl.BlockSpec(subkeys.shape, lambda i: (0, 0, 0))],
        out_specs=[pl.BlockSpec((d, ts), lambda i: (0, i)), stat_spec(ts), stat_spec(ts), stat_spec(ts), stat_spec(ts),
                   pl.BlockSpec((PEER_HEADS, ts), lambda i: (0, i))],
        out_shape=[jax.ShapeDtypeStruct((d, n_pad), jnp.bfloat16)] + [jax.ShapeDtypeStruct(hk, jnp.float32)] * 4
                  + [jax.ShapeDtypeStruct((PEER_HEADS, n_pad), jnp.float32)],
        compiler_params=pltpu.CompilerParams(dimension_semantics=("arbitrary",),
                                             vmem_limit_bytes=VMEM_LIMIT_BYTES),
        name="peer_score",
    )(xp, g_norm.reshape(1, d).astype(jnp.float32), wq_bf16, subkeys.astype(jnp.float32))
    te = PEER_EXPERT_TILE
    out = pl.pallas_call(
        _peer_dense_kernel,
        grid=(n_pad // tm, PEER_N_EXPERTS // te),
        in_specs=[pl.BlockSpec((tm, d), lambda i, j: (i, 0)),
                  pl.BlockSpec((d, tm), lambda i, j: (0, i)),
                  stat_spec(tm), stat_spec(tm), stat_spec(tm), stat_spec(tm),
                  pl.BlockSpec((PEER_HEADS, tm), lambda i, j: (0, i)),
                  pl.BlockSpec((te, d), lambda i, j: (j, 0)),
                  pl.BlockSpec((d, te), lambda i, j: (0, j))],
        out_specs=pl.BlockSpec((tm, d), lambda i, j: (i, 0)),
        out_shape=jax.ShapeDtypeStruct((n_pad, d), jnp.float32),
        scratch_shapes=[pltpu.VMEM((te, tm), jnp.float32), pltpu.VMEM((te, tm), jnp.bfloat16),
                        pltpu.VMEM((d, tm), jnp.float32)],
        compiler_params=pltpu.CompilerParams(dimension_semantics=("arbitrary", "arbitrary"),
                                             vmem_limit_bytes=VMEM_LIMIT_BYTES),
        name="peer_dense",
    )(xp, xt, s1, s2, e1, e2, tau, u_bf16, vt_bf16)
    return out[:n]


def rmsnorm(x, g):
    xf = x.astype(jnp.float32)
    y = xf * lax.rsqrt(jnp.mean(xf * xf, axis=-1, keepdims=True) + NORM_EPS)
    return (y * g.astype(jnp.float32)).astype(x.dtype)


def split_cols(a, widths):
    outs, off = [], 0
    for w in widths:
        outs.append(a[..., off:off + w])
        off += w
    return outs


def pad_axis1(a, n):
    return jnp.pad(a, [(0, 0), (0, n - a.shape[1])] + [(0, 0)] * (a.ndim - 2))


def qblocks(T, qmax):
    qb = min(qmax, T)
    nb = -(-T // qb)
    return qb, nb, nb * qb


def run_blocks(fn, nb, qb, T):
    out = lax.map(fn, jnp.arange(nb))
    out = jnp.moveaxis(out, 0, 1)
    return out.reshape(out.shape[:1] + (nb * qb,) + out.shape[3:])[:, :T]


def masked_softmax(s, mask):
    s = jnp.where(mask, s.astype(jnp.float32), NEG_INF)
    m = jnp.max(s, axis=-1, keepdims=True)
    p = jnp.exp(s - m) * mask
    return p / jnp.maximum(jnp.sum(p, axis=-1, keepdims=True), 1e-30)


def partial_rope(x, pos):
    half = ROT_DIM // 2
    inv = ROPE_THETA ** (-jnp.arange(0, ROT_DIM, 2, dtype=jnp.float32) / ROT_DIM)
    ang = pos.astype(jnp.float32)[:, None] * inv[None, :]
    cos = jnp.cos(ang)[:, None, :]
    sin = jnp.sin(ang)[:, None, :]
    x1 = x[..., :half].astype(jnp.float32)
    x2 = x[..., half:ROT_DIM].astype(jnp.float32)
    rot = jnp.concatenate([x1 * cos - x2 * sin, x2 * cos + x1 * sin], axis=-1).astype(x.dtype)
    return jnp.concatenate([rot, x[..., ROT_DIM:]], axis=-1)


def rope_keys(kv, pos):
    return jnp.stack([partial_rope(kv[:, :, 0], pos), kv[:, :, 1]], axis=2)


def gather_pages(pool, page_table):
    g = pool[page_table]
    return g.reshape((g.shape[0], g.shape[1] * g.shape[2]) + g.shape[3:])


def ssd_chunked(x, dt, a, b_h, c_h, h0):
    B, L, H, P = x.shape
    N = b_h.shape[-1]
    f32 = jnp.float32
    Q = min(SSD_CHUNK, L)
    nc = -(-L // Q)
    Lp = nc * Q
    xdt = pad_axis1(x.astype(f32) * dt[..., None], Lp).reshape(B, nc, Q, H, P)
    da = pad_axis1(dt * a, Lp).reshape(B, nc, Q, H)
    bc = pad_axis1(b_h.astype(f32), Lp).reshape(B, nc, Q, H, N)
    cc = pad_axis1(c_h.astype(f32), Lp).reshape(B, nc, Q, H, N)
    acum = jnp.cumsum(da, axis=2)
    causal = jnp.tril(jnp.ones((Q, Q), bool))
    seg = acum[:, :, :, None, :] - acum[:, :, None, :, :]
    decay_in = jnp.exp(jnp.where(causal[None, None, :, :, None], seg, NEG_INF))
    scores = jnp.einsum('bclhn,bcshn->bclsh', cc, bc) * decay_in
    y_diag = jnp.einsum('bclsh,bcshp->bclhp', scores, xdt)
    decay_out = jnp.exp(acum[:, :, -1:] - acum)
    chunk_states = jnp.einsum('bcshn,bcshp->bchpn', bc * decay_out[..., None], xdt)
    chunk_decay = jnp.exp(acum[:, :, -1])

    def step(h, inp):
        st, dec = inp
        return h * dec[:, :, None, None] + st, h

    h_last, h_enter = lax.scan(step, h0.astype(f32),
                               (jnp.moveaxis(chunk_states, 1, 0), jnp.moveaxis(chunk_decay, 1, 0)))
    h_enter = jnp.moveaxis(h_enter, 0, 1)
    y_off = jnp.einsum('bclhn,bchpn->bclhp', cc * jnp.exp(acum)[..., None], h_enter)
    y = (y_diag + y_off).reshape(B, Lp, H, P)[:, :L]
    return y, h_last


def mamba2_ssd(z, xbc, dt_raw, conv_state, ssm_state, conv_w, conv_b, dt_bias, a_log, d_skip, norm_g):
    B, T, _ = xbc.shape
    f32 = jnp.float32
    xpad = jnp.concatenate([conv_state.astype(xbc.dtype), xbc], axis=1)
    new_conv = xpad[:, T:]
    conv = conv_b
    for k in range(SSM_CONV):
        conv = conv + xpad[:, k:k + T] * conv_w[k]
    xbc_c = jax.nn.silu(conv)
    n_bc = SSM_GROUPS * SSM_STATE
    rep = SSM_HEADS // SSM_GROUPS
    xs = xbc_c[..., :SSM_D_INNER].reshape(B, T, SSM_HEADS, SSM_HEAD_DIM)
    b_h = jnp.repeat(xbc_c[..., SSM_D_INNER:SSM_D_INNER + n_bc].reshape(B, T, SSM_GROUPS, SSM_STATE), rep, axis=2)
    c_h = jnp.repeat(xbc_c[..., SSM_D_INNER + n_bc:].reshape(B, T, SSM_GROUPS, SSM_STATE), rep, axis=2)
    dt = jax.nn.softplus(dt_raw.astype(f32) + dt_bias.astype(f32))
    a = -jnp.exp(a_log.astype(f32))
    y, h_last = ssd_chunked(xs, dt, a, b_h, c_h, ssm_state)
    y = y + d_skip.astype(f32)[:, None] * xs.astype(f32)
    y = y.reshape(B, T, SSM_D_INNER) * jax.nn.silu(z.astype(f32))
    yg = y.reshape(B, T, SSM_GROUPS, SSM_D_INNER // SSM_GROUPS)
    yg = yg * lax.rsqrt(jnp.mean(yg * yg, axis=-1, keepdims=True) + NORM_EPS)
    y = yg.reshape(B, T, SSM_D_INNER) * norm_g.astype(f32)
    return y.astype(xbc.dtype), new_conv, h_last.astype(ssm_state.dtype)


def nsa_compress(kv_all, pos_emb, w1, b1, w2, b2):
    B, L = kv_all.shape[:2]
    S = NSA_CMP_STRIDE
    r = NSA_CMP_BLOCK // S
    n_seg = L // S
    n_cmp = n_seg - r + 1
    seg = kv_all[:, :n_seg * S].reshape(B, n_seg, S, 2, NSA_KV_HEADS, HEAD_DIM)
    h = b1[None, None, :, None, :]
    for j in range(r):
        pe = jnp.transpose(pos_emb[:, j * S:(j + 1) * S], (1, 0, 2))[:, :, None, :]
        h = h + jnp.einsum('bnlcgd,cldh->bncgh', seg[:, j:j + n_cmp] + pe, w1[:, j * S:(j + 1) * S])
    h = jax.nn.gelu(h)
    return jnp.einsum('bncgh,chd->bncgd', h, w2) + b2[None, None, :, None, :]


def nsa_selected(qr, kvs_all, sel_idx, sel_ok, pos0):
    B, T = qr.shape[:2]
    Lk = kvs_all.shape[1]
    n_sel = -(-Lk // NSA_SEL_BLOCK)
    topn = sel_idx.shape[-1]
    kvb = pad_axis1(kvs_all, n_sel * NSA_SEL_BLOCK).reshape(B, n_sel, NSA_SEL_BLOCK, 2, NSA_KV_HEADS, HEAD_DIM)
    kvb = jnp.transpose(kvb, (0, 4, 1, 2, 3, 5))
    qb, nb, Tp = qblocks(T, SEL_QBLOCK)
    qp, ip, okp = pad_axis1(qr, Tp), pad_axis1(sel_idx, Tp), pad_axis1(sel_ok, Tp)
    bi = jnp.arange(B)[:, None, None, None]
    gi = jnp.arange(NSA_KV_HEADS)[None, None, :, None]
    n_keys = topn * NSA_SEL_BLOCK

    def blk(i):
        start = i * qb
        q_b = lax.dynamic_slice_in_dim(qp, start, qb, 1)
        i_b = lax.dynamic_slice_in_dim(ip, start, qb, 1)
        ok_b = lax.dynamic_slice_in_dim(okp, start, qb, 1)
        qpos = pos0 + start + jnp.arange(qb)
        g = kvb[bi, gi, i_b]
        kpos = i_b[..., None] * NSA_SEL_BLOCK + jnp.arange(NSA_SEL_BLOCK)
        mask = (ok_b[..., None] & (kpos <= qpos[None, :, None, None, None])).reshape(B, qb, NSA_KV_HEADS, 1, n_keys)
        kk = g[..., 0, :].reshape(B, qb, NSA_KV_HEADS, n_keys, HEAD_DIM)
        vv = g[..., 1, :].reshape(B, qb, NSA_KV_HEADS, n_keys, HEAD_DIM)
        p = masked_softmax(jnp.einsum('bqgjd,bqgkd->bqgjk', q_b, kk) * ATTN_SCALE, mask)
        return jnp.einsum('bqgjk,bqgkd->bqgjd', p, vv)

    return run_blocks(blk, nb, qb, T)


def nsa_window(qr, kvw_ext, pos0):
    B, T = qr.shape[:2]
    Wb = kvw_ext.shape[1] - T
    W = NSA_WINDOW
    qb, nb, Tp = qblocks(T, WIN_QBLOCK)
    qp = pad_axis1(qr, Tp)
    kvp = jnp.pad(kvw_ext, [(0, 0), (W, Tp - T), (0, 0), (0, 0), (0, 0)])
    n_kp = W + Wb + Tp
    idx = jnp.arange(n_kp)
    kpos = (pos0 - Wb - W) + idx
    kval = (idx >= W) & (idx < W + Wb + T)

    def blk(i):
        start = i * qb
        q_b = lax.dynamic_slice_in_dim(qp, start, qb, 1)
        kv_b = lax.dynamic_slice_in_dim(kvp, start + Wb, W + qb, 1)
        kp_b = lax.dynamic_slice_in_dim(kpos, start + Wb, W + qb, 0)
        ok_b = lax.dynamic_slice_in_dim(kval, start + Wb, W + qb, 0)
        qpos = pos0 + start + jnp.arange(qb)
        mask = ok_b[None, :] & (kp_b[None, :] <= qpos[:, None]) & (qpos[:, None] - kp_b[None, :] < W)
        s = jnp.einsum('bqgjd,bkgd->bqgjk', q_b, kv_b[:, :, 0]) * ATTN_SCALE
        p = masked_softmax(s, mask[None, :, None, None, :])
        return jnp.einsum('bqgjk,bkgd->bqgjd', p, kv_b[:, :, 1])

    return run_blocks(blk, nb, qb, T)


def nsa_attention(q, qr, kvc_all, kvs_all, kvw_ext, pos0, cmp_pos, cmp_w1, cmp_b1, cmp_w2, cmp_b2):
    B, T = q.shape[:2]
    pos_q = pos0 + jnp.arange(T, dtype=jnp.int32)
    qg = q.reshape(B, T, NSA_KV_HEADS, NSA_GQ, HEAD_DIM)
    qr = qr.reshape(B, T, NSA_KV_HEADS, NSA_GQ, HEAD_DIM)
    cmp = nsa_compress(kvc_all, cmp_pos, cmp_w1, cmp_b1, cmp_w2, cmp_b2)
    n_cmp = cmp.shape[1]
    cmp_start = NSA_CMP_STRIDE * jnp.arange(n_cmp, dtype=jnp.int32)
    cmp_mask = (cmp_start + NSA_CMP_BLOCK - 1)[None, :] <= pos_q[:, None]
    s = jnp.einsum('btgjd,bngd->btgjn', qg, cmp[:, :, 0]) * ATTN_SCALE
    p_cmp = masked_softmax(s, cmp_mask[None, :, None, None, :])
    o_cmp = jnp.einsum('btgjn,bngd->btgjd', p_cmp, cmp[:, :, 1])
    n_sel = -(-kvs_all.shape[1] // NSA_SEL_BLOCK)
    sel_start = NSA_SEL_BLOCK * jnp.arange(n_sel, dtype=jnp.int32)
    overlap = ((cmp_start[:, None] < sel_start[None, :] + NSA_SEL_BLOCK)
               & (cmp_start[:, None] + NSA_CMP_BLOCK > sel_start[None, :])).astype(jnp.float32)
    imp = jnp.einsum('btgjn,ns->btgs', p_cmp, overlap)
    blk_t = pos_q // NSA_SEL_BLOCK
    sidx = jnp.arange(n_sel, dtype=jnp.int32)
    valid = sidx[None, :] <= blk_t[:, None]
    forced = valid & ((sidx[None, :] == 0) | (sidx[None, :] > blk_t[:, None] - NSA_LOCAL_BLOCKS))
    imp = jnp.where(forced[None, :, None, :], NSA_FORCE, imp)
    imp = jnp.where(valid[None, :, None, :], imp, NEG_INF)
    sel_score, sel_idx = lax.top_k(imp, min(NSA_TOPN, n_sel))
    o_sel = nsa_selected(qr, kvs_all, sel_idx, sel_score > 0.5 * NEG_INF, pos0)
    o_win = nsa_window(qr, kvw_ext, pos0)
    return o_cmp, o_sel, o_win


def moba_attention(qr, kv_all, pos0):
    B, T = qr.shape[:2]
    Lk = kv_all.shape[1]
    nblk = -(-Lk // MOBA_BLOCK)
    kvb = pad_axis1(kv_all, nblk * MOBA_BLOCK).reshape(B, nblk, MOBA_BLOCK, 2, MOBA_KV_HEADS, HEAD_DIM)
    kvb = jnp.transpose(kvb, (0, 4, 1, 2, 3, 5))
    kmean = jnp.mean(kvb[..., 0, :].astype(jnp.float32), axis=3)
    pos_q = pos0 + jnp.arange(T, dtype=jnp.int32)
    qg = qr.reshape(B, T, MOBA_KV_HEADS, MOBA_GQ, HEAD_DIM)
    gate = jnp.einsum('btgjd,bgnd->btgjn', qg.astype(jnp.float32), kmean)
    past_ok = jnp.arange(nblk)[None, :] < (pos_q // MOBA_BLOCK)[:, None]
    gate = jnp.where(past_ok[None, :, None, None, :], gate, NEG_INF)
    k = min(MOBA_TOPK, nblk)
    sc, idx = lax.top_k(gate, k)
    ok = sc > 0.5 * NEG_INF
    qb, nb, Tp = qblocks(T, MOBA_QBLOCK)
    qp, ip, okp = pad_axis1(qg, Tp), pad_axis1(idx, Tp), pad_axis1(ok, Tp)
    bi = jnp.arange(B)[:, None, None, None, None]
    gi = jnp.arange(MOBA_KV_HEADS)[None, None, :, None, None]
    bo = jnp.arange(B)[:, None, None]
    go = jnp.arange(MOBA_KV_HEADS)[None, None, :]
    n_sel = k * MOBA_BLOCK

    def blk(i):
        start = i * qb
        q_b = lax.dynamic_slice_in_dim(qp, start, qb, 1)
        i_b = lax.dynamic_slice_in_dim(ip, start, qb, 1)
        ok_b = lax.dynamic_slice_in_dim(okp, start, qb, 1)
        qpos = pos0 + start + jnp.arange(qb)
        own_blk = jnp.minimum(qpos // MOBA_BLOCK, nblk - 1)
        g_sel = kvb[bi, gi, i_b]
        g_own = kvb[bo, go, own_blk[None, :, None]]
        s_sel = jnp.einsum('bqgjd,bqgjkld->bqgjkl', q_b, g_sel[..., 0, :]).reshape(B, qb, MOBA_KV_HEADS, MOBA_GQ, n_sel)
        s_own = jnp.einsum('bqgjd,bqgld->bqgjl', q_b, g_own[..., 0, :])
        m_sel = jnp.broadcast_to(ok_b[..., None], ok_b.shape + (MOBA_BLOCK,)).reshape(B, qb, MOBA_KV_HEADS, MOBA_GQ, n_sel)
        own_pos = own_blk[:, None] * MOBA_BLOCK + jnp.arange(MOBA_BLOCK)
        m_own = jnp.broadcast_to((own_pos <= qpos[:, None])[None, :, None, None, :], (B, qb, MOBA_KV_HEADS, MOBA_GQ, MOBA_BLOCK))
        p = masked_softmax(jnp.concatenate([s_sel, s_own], axis=-1) * ATTN_SCALE,
                           jnp.concatenate([m_sel, m_own], axis=-1))
        v_sel = g_sel[..., 1, :].reshape(B, qb, MOBA_KV_HEADS, MOBA_GQ, n_sel, HEAD_DIM)
        return (jnp.einsum('bqgjm,bqgjmd->bqgjd', p[..., :n_sel], v_sel)
                + jnp.einsum('bqgjl,bqgld->bqgjd', p[..., n_sel:], g_own[..., 1, :]))

    o = run_blocks(blk, nb, qb, T)
    return o.reshape(B, T, MOBA_HEADS * HEAD_DIM).astype(qr.dtype)


def kernel(x_prompt, x_sample, cache_nsa_cmp_kv, cache_nsa_sel_kv, state_nsa_win_kv, state_ssm, state_conv,
           cache_moba_kv, page_table, norm_mix, norm_ffn, norm_final, w_in_even, w_out_even,
           ssm_conv_w, ssm_conv_b, ssm_dt_bias, ssm_a_log, ssm_d_skip, ssm_norm,
           nsa_cmp_pos, nsa_cmp_w1, nsa_cmp_b1, nsa_cmp_w2, nsa_cmp_b2, nsa_gate_b,
           w_in_odd, w_out_odd, peer_wq, peer_subkeys, peer_u, peer_v):

    peer_wq_bf16 = peer_wq.astype(jnp.bfloat16)
    peer_u_bf16 = peer_u.astype(jnp.bfloat16)
    peer_vt_bf16 = jnp.transpose(peer_v, (0, 2, 1)).astype(jnp.bfloat16)
    w_even_bf16 = [relayout_even_weight(w_in_even[e]) for e in range(w_in_even.shape[0])]
    w_odd_bf16 = w_in_odd.astype(jnp.bfloat16)

    def trunk(x, pos0, get_past, prompt):
        B, T, _ = x.shape
        pos_q = pos0 + jnp.arange(T, dtype=jnp.int32)
        new_cmp, new_sel, new_win, new_ssm, new_conv, new_moba = [], [], [], [], [], []
        tabs = rope_tables(pos_q)
        if T % ATTN_TILE != 0:
            tabs = tuple(jnp.tile(a, (B, 1)) for a in tabs)
        for l in range(DEPTH):
            x2 = x.reshape(B * T, D_MODEL)
            if l % 2 == 0:
                e = l // 2
                z, xbc, q, qr, kvc, kvs, kvw, gd = layer_projection(
                    x2, norm_mix[l], w_even_bf16[e], tabs, EVEN_OUT_WIDTHS, _even_proj_kernel)
                kv_shape = (B, T, 2, NSA_KV_HEADS, HEAD_DIM)
                kvc, kvs, kvw = kvc.reshape(kv_shape), kvs.reshape(kv_shape), kvw.reshape(kv_shape)
                kvw_ext = jnp.concatenate([get_past('win', e), kvw], axis=1)
                if prompt:
                    conv_past = get_past('conv', e)
                    y_ssm, ssm_new = ssd_mixer(z, xbc, gd, conv_past, get_past('ssm', e), ssm_conv_w[e], ssm_conv_b[e],
                                               ssm_dt_bias[e], ssm_a_log[e], ssm_d_skip[e], ssm_norm[e], B)
                    conv_new = jnp.concatenate([conv_past, xbc.reshape(B, T, SSM_CONV_DIM)], axis=1)[:, T:]
                    o_cmp, sel = nsa_compressed_branch(q, kvc.reshape(B * T, N_KV), B, nsa_cmp_pos[e], nsa_cmp_w1[e],
                                                       nsa_cmp_b1[e], nsa_cmp_w2[e], nsa_cmp_b2[e])
                    o_sel = flash_attention_t(qr, kvs.reshape(B * T, N_KV), 'sel', B, sel)
                    o_win = flash_attention_t(qr, kvw.reshape(B * T, N_KV), 'win', B)
                else:
                    y_ssm, conv_new, ssm_new = mamba2_ssd(
                        z.reshape(B, T, -1), xbc.reshape(B, T, -1), gd[:, 3 * NSA_HEADS:].reshape(B, T, SSM_HEADS),
                        get_past('conv', e), get_past('ssm', e), ssm_conv_w[e], ssm_conv_b[e], ssm_dt_bias[e],
                        ssm_a_log[e], ssm_d_skip[e], ssm_norm[e])
                    o_cmp, o_sel, o_win = nsa_attention(
                        q.reshape(B, T, NSA_HEADS, HEAD_DIM), qr,
                        jnp.concatenate([get_past('cmp', e), kvc], axis=1),
                        jnp.concatenate([get_past('sel', e), kvs], axis=1), kvw_ext, pos0,
                        nsa_cmp_pos[e], nsa_cmp_w1[e], nsa_cmp_b1[e], nsa_cmp_w2[e], nsa_cmp_b2[e])
                flat = lambda a: a.reshape(B * T, -1)
                x = even_output(x2, flat(y_ssm), flat(o_cmp), flat(o_sel), flat(o_win), gd, nsa_gate_b[e],
                                w_out_even[e])
                keep = min(NSA_WINDOW, kvw_ext.shape[1])
                new_cmp.append(kvc)
                new_sel.append(kvs)
                new_win.append(kvw_ext[:, kvw_ext.shape[1] - keep:])
                new_ssm.append(ssm_new)
                new_conv.append(conv_new)
            else:
                o = l // 2
                q, kv2 = layer_projection(x2, norm_mix[l], w_odd_bf16[o], tabs, ODD_OUT_WIDTHS, _odd_proj_kernel)
                kv = kv2.reshape(B, T, 2, MOBA_KV_HEADS, HEAD_DIM)
                if prompt:
                    y_moba = flash_attention_t(q, kv2, 'moba', B)
                else:
                    y_moba = moba_attention(q.reshape(B, T, MOBA_HEADS, HEAD_DIM),
                                            jnp.concatenate([get_past('moba', o), kv], axis=1), pos0)
                x = odd_output(x2, y_moba.reshape(B * T, -1), w_out_odd[o])
                new_moba.append(kv)
            x = peer_ffn_dense(x, norm_ffn[l], peer_wq_bf16[l], peer_subkeys[l], peer_u_bf16[l], peer_vt_bf16[l])
        y_out = final_norm(x, norm_final).reshape(B, T, D_MODEL)
        return (y_out, jnp.stack(new_cmp), jnp.stack(new_sel), jnp.stack(new_win),
                jnp.stack(new_ssm), jnp.stack(new_conv), jnp.stack(new_moba))

    bp = x_prompt.shape[0]
    dtp = x_prompt.dtype

    def prompt_past(kind, i):
        if kind == 'ssm':
            return jnp.zeros((bp, SSM_HEADS, SSM_HEAD_DIM, SSM_STATE), state_ssm.dtype)
        if kind == 'conv':
            return jnp.zeros((bp, SSM_CONV - 1, SSM_CONV_DIM), dtp)
        if kind == 'moba':
            return jnp.zeros((bp, 0, 2, MOBA_KV_HEADS, HEAD_DIM), dtp)
        return jnp.zeros((bp, 0, 2, NSA_KV_HEADS, HEAD_DIM), dtp)

    def sample_past(kind, i):
        if kind == 'ssm':
            return state_ssm[i]
        if kind == 'conv':
            return state_conv[i]
        if kind == 'win':
            return state_nsa_win_kv[i]
        if kind == 'cmp':
            return gather_pages(cache_nsa_cmp_kv[i], page_table)
        if kind == 'sel':
            return gather_pages(cache_nsa_sel_kv[i], page_table)
        return gather_pages(cache_moba_kv[i], page_table)

    past_len = page_table.shape[1] * cache_nsa_cmp_kv.shape[2]
    y_prompt, p_cmp, p_sel, p_win, p_ssm, p_conv, p_moba = trunk(x_prompt, 0, prompt_past, True)
    y_sample, s_cmp, s_sel, s_win, s_ssm, s_conv, s_moba = trunk(x_sample, past_len, sample_past, False)
    return (y_prompt, y_sample, p_cmp, p_sel, p_win, p_ssm, p_conv, p_moba,
            s_cmp, s_sel, s_win, s_ssm, s_conv, s_moba)
```

```python
import functools
import math

import jax
import jax.numpy as jnp
from jax import lax
from jax.experimental import pallas as pl
from jax.experimental.pallas import tpu as pltpu

D_MODEL = 1024
DEPTH = 4
HEAD_DIM = 64
ROT_DIM = HEAD_DIM // 4
ROPE_THETA = 500000.0
ATTN_SCALE = HEAD_DIM ** -0.5
NORM_EPS = 1e-6
NEG_INF = -1e30

SSM_HEADS = 16
SSM_HEAD_DIM = 64
SSM_D_INNER = SSM_HEADS * SSM_HEAD_DIM
SSM_GROUPS = 2
SSM_STATE = 128
SSM_CONV = 4
SSM_CONV_DIM = SSM_D_INNER + 2 * SSM_GROUPS * SSM_STATE
SSD_CHUNK = 128

NSA_HEADS = 16
NSA_KV_HEADS = 4
NSA_GQ = NSA_HEADS // NSA_KV_HEADS
NSA_CMP_BLOCK = 32
NSA_CMP_STRIDE = 16
NSA_SEL_BLOCK = 64
NSA_TOPN = 8
NSA_LOCAL_BLOCKS = 2
NSA_FORCE = 1e6
NSA_WINDOW = 512

MOBA_HEADS = 16
MOBA_KV_HEADS = 4
MOBA_GQ = MOBA_HEADS // MOBA_KV_HEADS
MOBA_BLOCK = 256
MOBA_TOPK = 3

PEER_HEADS = 8
PEER_N_KEYS = 128
PEER_KEY_DIM = 256
PEER_TOPK = 16

EVEN_WIDTHS = (SSM_D_INNER, SSM_CONV_DIM, SSM_HEADS, NSA_HEADS * HEAD_DIM,
               2 * NSA_KV_HEADS * HEAD_DIM, 2 * NSA_KV_HEADS * HEAD_DIM, 2 * NSA_KV_HEADS * HEAD_DIM,
               3 * NSA_HEADS)
ODD_WIDTHS = (MOBA_HEADS * HEAD_DIM, 2 * MOBA_KV_HEADS * HEAD_DIM)

WIN_QBLOCK = 128
SEL_QBLOCK = 16
MOBA_QBLOCK = 4
PEER_TBLOCK = 128

VMEM_LIMIT_BYTES = 48 * 1024 * 1024


ROW_TILE = 256


def _row_call(body, m, row_inputs, fixed_inputs, n_out, name):
    tm = min(m, ROW_TILE)
    assert m % tm == 0
    return pl.pallas_call(
        body,
        grid=(m // tm,),
        in_specs=[pl.BlockSpec((tm, a.shape[1]), lambda i: (i, 0)) for a in row_inputs]
                 + [pl.BlockSpec(a.shape, lambda i: (0, 0)) for a in fixed_inputs],
        out_specs=pl.BlockSpec((tm, n_out), lambda i: (i, 0)),
        out_shape=jax.ShapeDtypeStruct((m, n_out), jnp.float32),
        compiler_params=pltpu.CompilerParams(dimension_semantics=("arbitrary",),
                                             vmem_limit_bytes=VMEM_LIMIT_BYTES),
        name=name,
    )(*row_inputs, *fixed_inputs)


def _odd_out_kernel(x_ref, y_ref, w_ref, o_ref):
    o_ref[...] = x_ref[...] + jnp.dot(y_ref[...].astype(jnp.bfloat16), w_ref[...], preferred_element_type=jnp.float32)


def _even_out_kernel(x_ref, ssm_ref, cmp_ref, sel_ref, win_ref, gd_ref, gb_ref, ex_ref, wa_ref, wb_ref, o_ref):
    f32, bf16 = jnp.float32, jnp.bfloat16
    gates = jax.nn.sigmoid(gd_ref[...] + gb_ref[...])
    y_nsa = None
    for c, branch in enumerate((cmp_ref, sel_ref, win_ref)):
        g_c = jnp.dot(gates, ex_ref[c], precision=lax.Precision.HIGHEST, preferred_element_type=f32)
        term = g_c * branch[...]
        y_nsa = term if y_nsa is None else y_nsa + term
    o_ref[...] = (x_ref[...] + jnp.dot(ssm_ref[...].astype(bf16), wa_ref[...], preferred_element_type=f32)
                  + jnp.dot(y_nsa.astype(bf16), wb_ref[...], preferred_element_type=f32))


def _final_norm_kernel(x_ref, g_ref, o_ref):
    x = x_ref[...]
    o_ref[...] = x * lax.rsqrt(jnp.mean(x * x, axis=-1, keepdims=True) + NORM_EPS) * g_ref[...]


def even_output(x2, y_ssm, o_cmp, o_sel, o_win, gd, gate_b, w_out):
    f32 = jnp.float32
    n_g = 3 * NSA_HEADS
    gb = jnp.concatenate([gate_b.astype(f32), jnp.zeros((N_GD - n_g,), f32)]).reshape(1, N_GD)
    lane_head = jnp.arange(N_Q) // HEAD_DIM
    expand = jnp.stack([(jnp.arange(N_GD)[:, None] == 3 * lane_head[None, :] + c).astype(f32) for c in range(3)])
    w = w_out.astype(jnp.bfloat16)
    m = x2.shape[0]
    tm = min(m, ROW_TILE)
    row = lambda a: pl.BlockSpec((tm, a.shape[1]), lambda i: (i, 0))
    whole = lambda a: pl.BlockSpec(a.shape, lambda i: (0,) * a.ndim)
    rows = (x2, y_ssm, o_cmp, o_sel, o_win, gd)
    fixed = (gb, expand, w[:SSM_D_INNER], w[SSM_D_INNER:])
    return pl.pallas_call(
        _even_out_kernel,
        grid=(m // tm,),
        in_specs=[row(a) for a in rows] + [whole(a) for a in fixed],
        out_specs=pl.BlockSpec((tm, D_MODEL), lambda i: (i, 0)),
        out_shape=jax.ShapeDtypeStruct((m, D_MODEL), f32),
        compiler_params=pltpu.CompilerParams(dimension_semantics=("arbitrary",),
                                             vmem_limit_bytes=VMEM_LIMIT_BYTES),
        name="even_out",
    )(*rows, *fixed)


def odd_output(x2, y, w_out):
    return _row_call(_odd_out_kernel, x2.shape[0], (x2, y), (w_out.astype(jnp.bfloat16),), D_MODEL, "odd_out")


def final_norm(x2, g):
    return _row_call(_final_norm_kernel, x2.shape[0], (x2,), (g.reshape(1, -1).astype(jnp.float32),), D_MODEL,
                     "final_norm")


LANES = 128
ROT_HALF = ROT_DIM // 2
PROJ_VMEM_LIMIT_BYTES = 56 * 1024 * 1024
N_Q = NSA_HEADS * HEAD_DIM
N_KV = 2 * NSA_KV_HEADS * HEAD_DIM
EVEN_OFF_XBC = SSM_D_INNER
EVEN_OFF_Q = EVEN_OFF_XBC + SSM_CONV_DIM
EVEN_OFF_KVC = EVEN_OFF_Q + N_Q
EVEN_OFF_KVS = EVEN_OFF_KVC + N_KV
EVEN_OFF_KVW = EVEN_OFF_KVS + N_KV
EVEN_OFF_GD = EVEN_OFF_KVW + N_KV
N_GD = 3 * NSA_HEADS + SSM_HEADS
EVEN_TOTAL = EVEN_OFF_GD + N_GD


def rope_tables(pos):
    f32 = jnp.float32
    inv = ROPE_THETA ** (-jnp.arange(0, ROT_DIM, 2, dtype=f32) / ROT_DIM)
    ang = pos.astype(f32)[:, None] * inv[None, :]
    cos, sin = jnp.cos(ang), jnp.sin(ang)
    r = pos.shape[0]
    zeros = lambda n: jnp.zeros((r, n), f32)
    c = jnp.concatenate([cos, cos, jnp.ones((r, HEAD_DIM - ROT_DIM), f32)], axis=1)
    sm = jnp.concatenate([-sin, zeros(HEAD_DIM - ROT_HALF)], axis=1)
    sp = jnp.concatenate([zeros(ROT_HALF), sin, zeros(HEAD_DIM - ROT_DIM)], axis=1)
    rep = LANES // HEAD_DIM
    return tuple(jnp.tile(a, (1, rep)) for a in (c, sm, sp))


def _rope_lanes(x, c, sm, sp):
    w = x.shape[1]
    reps = w // LANES
    tile = lambda a: jnp.concatenate([a] * reps, axis=1) if reps > 1 else a
    return (x * tile(c) + pltpu.roll(x, w - ROT_HALF, axis=1) * tile(sm)
            + pltpu.roll(x, ROT_HALF, axis=1) * tile(sp))


def _rope_keys_lanes(kv, c, sm, sp):
    half = kv.shape[1] // 2
    return jnp.concatenate([_rope_lanes(kv[:, :half], c, sm, sp), kv[:, half:]], axis=1)


def _normed_dot(x_ref, g_ref, w_ref):
    x = x_ref[...]
    xn = x * lax.rsqrt(jnp.mean(x * x, axis=-1, keepdims=True) + NORM_EPS) * g_ref[...]
    return jnp.dot(xn.astype(jnp.bfloat16), w_ref[...], preferred_element_type=jnp.float32)


KV_CACHE_TAIL = (2, NSA_KV_HEADS, HEAD_DIM)


def _store_cache_rows(ref, kv):
    for c in range(KV_CACHE_TAIL[0]):
        for g in range(KV_CACHE_TAIL[1]):
            off = (c * KV_CACHE_TAIL[1] + g) * HEAD_DIM
            ref[:, c, g, :] = kv[:, off:off + HEAD_DIM]


def _even_proj_kernel(x_ref, g_ref, w_ref, c_ref, sm_ref, sp_ref,
                      z_ref, xbc_ref, q_ref, qr_ref, kvc_ref, kvs_ref, kvw_ref, gd_ref,
                      kvc_cache_ref, kvs_cache_ref, kvw_cache_ref):
    o = _normed_dot(x_ref, g_ref, w_ref)
    rope = (c_ref[...], sm_ref[...], sp_ref[...])
    z_ref[...] = o[:, :EVEN_OFF_XBC]
    xbc_ref[...] = o[:, EVEN_OFF_XBC:EVEN_OFF_Q]
    q = o[:, EVEN_OFF_Q:EVEN_OFF_KVC]
    q_ref[...] = q
    qr_ref[...] = _rope_lanes(q, *rope)
    kvc = o[:, EVEN_OFF_KVC:EVEN_OFF_KVS]
    kvs = _rope_keys_lanes(o[:, EVEN_OFF_KVS:EVEN_OFF_KVW], *rope)
    kvw = _rope_keys_lanes(o[:, EVEN_OFF_KVW:EVEN_OFF_GD], *rope)
    kvc_ref[...] = kvc
    kvs_ref[...] = kvs
    kvw_ref[...] = kvw
    gd_ref[...] = o[:, EVEN_OFF_GD:EVEN_TOTAL]
    _store_cache_rows(kvc_cache_ref, kvc)
    _store_cache_rows(kvs_cache_ref, kvs)
    _store_cache_rows(kvw_cache_ref, kvw)


def _odd_proj_kernel(x_ref, g_ref, w_ref, c_ref, sm_ref, sp_ref, qr_ref, kv_ref, kv_cache_ref):
    o = _normed_dot(x_ref, g_ref, w_ref)
    rope = (c_ref[...], sm_ref[...], sp_ref[...])
    qr_ref[...] = _rope_lanes(o[:, :N_Q], *rope)
    kv = _rope_keys_lanes(o[:, N_Q:], *rope)
    kv_ref[...] = kv
    _store_cache_rows(kv_cache_ref, kv)


def relayout_even_weight(w):
    dt0 = SSM_D_INNER + SSM_CONV_DIM
    return jnp.concatenate([w[:, :dt0], w[:, dt0 + SSM_HEADS:], w[:, dt0:dt0 + SSM_HEADS]], axis=1).astype(jnp.bfloat16)


def layer_projection(x2, g, w_bf16, rope_tabs, widths, body):
    m, k = x2.shape
    r = rope_tabs[0].shape[0]
    tm = min(m, r, 256)
    assert m % tm == 0 and r % tm == 0
    nr = r // tm
    row = lambda i: (i, 0)
    fixed = lambda i: (0, 0)
    tab = pl.BlockSpec((tm, LANES), lambda i: (i % nr, 0))
    return pl.pallas_call(
        body,
        grid=(m // tm,),
        in_specs=[pl.BlockSpec((tm, k), row), pl.BlockSpec((1, k), fixed), pl.BlockSpec(w_bf16.shape, fixed),
                  tab, tab, tab],
        out_specs=[pl.BlockSpec((tm,) + tail, lambda i, nd=len(tail): (i,) + (0,) * nd) for tail in widths],
        out_shape=[jax.ShapeDtypeStruct((m,) + tail, jnp.float32) for tail in widths],
        compiler_params=pltpu.CompilerParams(dimension_semantics=("arbitrary",),
                                             vmem_limit_bytes=PROJ_VMEM_LIMIT_BYTES),
        name=body.__name__.strip("_"),
    )(x2, g.reshape(1, k).astype(jnp.float32), w_bf16, *rope_tabs)


EVEN_OUT_WIDTHS = ((SSM_D_INNER,), (SSM_CONV_DIM,), (N_Q,), (N_Q,), (N_KV,), (N_KV,), (N_KV,), (N_GD,),
                   KV_CACHE_TAIL, KV_CACHE_TAIL, KV_CACHE_TAIL)
ODD_OUT_WIDTHS = ((N_Q,), (N_KV,), KV_CACHE_TAIL)


def _ssd_kernel(z_ref, xbc_ref, gd_ref, dtt_ref, conv0_ref, h0_ref, cw_ref, cb_ref, dtb_ref, dtbt_ref,
                alog_ref, alogt_ref, dskip_ref, ng_ref, y_ref, hlast_ref, h_scr, xw_scr):
    f32, bf16 = jnp.float32, jnp.bfloat16
    hi = lax.Precision.HIGHEST
    Q = SSD_CHUNK
    c_idx = pl.program_id(1)
    tail = 8

    @pl.when(c_idx == 0)
    def _():
        h_scr[...] = h0_ref[0]
        xw_scr[0:tail, :] = conv0_ref[0]

    xw_scr[tail:tail + Q, :] = xbc_ref[...]
    conv = cb_ref[...]
    for k in range(SSM_CONV):
        start = tail - (SSM_CONV - 1) + k
        conv = conv + xw_scr[start:start + Q, :] * cw_ref[k:k + 1, :]
    xw_scr[0:tail, :] = xw_scr[Q:Q + tail, :]
    xc = conv * jax.nn.sigmoid(conv)
    xs = xc[:, :SSM_D_INNER]
    n_bc = SSM_GROUPS * SSM_STATE
    bm = xc[:, SSM_D_INNER:SSM_D_INNER + n_bc]
    cm = xc[:, SSM_D_INNER + n_bc:]

    dt = jax.nn.softplus(gd_ref[:, 3 * NSA_HEADS:] + dtb_ref[...])
    dtt = jax.nn.softplus(dtt_ref[0] + dtbt_ref[...])
    da = dt * (-jnp.exp(alog_ref[...]))
    dat = dtt * (-jnp.exp(alogt_ref[...]))
    ri = lax.broadcasted_iota(jnp.int32, (Q, Q), 0)
    ci = lax.broadcasted_iota(jnp.int32, (Q, Q), 1)
    causal = ci <= ri
    acum = jnp.dot(jnp.where(causal, 1.0, 0.0), da, precision=hi, preferred_element_type=f32)
    acumt = jnp.dot(dat, jnp.where(ri <= ci, 1.0, 0.0), precision=hi, preferred_element_type=f32)
    tot_t = acumt[:, Q - 1:Q]
    hh = lax.broadcasted_iota(jnp.int32, (SSM_HEADS, SSM_D_INNER), 0)
    ch = lax.broadcasted_iota(jnp.int32, (SSM_HEADS, SSM_D_INNER), 1) // SSM_HEAD_DIM
    expand = jnp.where(hh == ch, 1.0, 0.0)
    xdt = xs * jnp.dot(dt, expand, precision=hi, preferred_element_type=f32)
    xst = xs.T
    rep = SSM_HEADS // SSM_GROUPS
    ys = []
    for g in range(SSM_GROUPS):
        b_g = bm[:, g * SSM_STATE:(g + 1) * SSM_STATE]
        c_g = cm[:, g * SSM_STATE:(g + 1) * SSM_STATE]
        b_bf, c_bf = b_g.astype(bf16), c_g.astype(bf16)
        cb = lax.dot_general(c_bf, b_bf, (((1,), (1,)), ((), ())), preferred_element_type=f32)
        for h in range(g * rep, (g + 1) * rep):
            p0 = h * SSM_HEAD_DIM
            acol = acum[:, h:h + 1]
            arow = acumt[h:h + 1, :]
            decay = jnp.exp(jnp.where(causal, acol - arow, NEG_INF))
            y_h = jnp.dot((cb * decay).astype(bf16), xdt[:, p0:p0 + SSM_HEAD_DIM].astype(bf16),
                          preferred_element_type=f32)
            h_old = h_scr[h]
            y_h = y_h + jnp.exp(acol) * lax.dot_general(c_bf, h_old.astype(bf16), (((1,), (1,)), ((), ())),
                                                        preferred_element_type=f32)
            ys.append(y_h)
            w_row = dtt[h:h + 1, :] * jnp.exp(tot_t[h:h + 1, :] - arow)
            xdt_t = (xst[p0:p0 + SSM_HEAD_DIM, :] * w_row).astype(bf16)
            h_scr[h] = h_old * jnp.exp(tot_t[h:h + 1, :]) + jnp.dot(xdt_t, b_bf, preferred_element_type=f32)
    y = jnp.concatenate(ys, axis=1) + dskip_ref[...] * xs
    zz = z_ref[...]
    y = y * (zz * jax.nn.sigmoid(zz))
    gw = SSM_D_INNER // SSM_GROUPS
    outs = []
    for g in range(SSM_GROUPS):
        yg = y[:, g * gw:(g + 1) * gw]
        outs.append(yg * lax.rsqrt(jnp.mean(yg * yg, axis=-1, keepdims=True) + NORM_EPS))
    y_ref[...] = jnp.concatenate(outs, axis=1) * ng_ref[...]

    @pl.when(c_idx == pl.num_programs(1) - 1)
    def _():
        hlast_ref[0] = h_scr[...]


def ssd_mixer(z, xbc, gd, conv_state, ssm_state, conv_w, conv_b, dt_bias, a_log, d_skip, norm_g, batch):
    m = z.shape[0]
    t = m // batch
    assert t % SSD_CHUNK == 0
    nc = t // SSD_CHUNK
    f32 = jnp.float32
    dtt = jnp.transpose(gd[:, 3 * NSA_HEADS:].reshape(batch, t, SSM_HEADS), (0, 2, 1))
    conv0 = jnp.pad(conv_state.astype(f32), ((0, 0), (8 - (SSM_CONV - 1), 0), (0, 0)))
    row = lambda b, c: (b * nc + c, 0)
    fixed2 = lambda b, c: (0, 0)
    vec = lambda a: a.reshape(1, -1).astype(f32)
    col = lambda a: a.reshape(-1, 1).astype(f32)
    y, h_last = pl.pallas_call(
        _ssd_kernel,
        grid=(batch, nc),
        in_specs=[pl.BlockSpec((SSD_CHUNK, SSM_D_INNER), row),
                  pl.BlockSpec((SSD_CHUNK, SSM_CONV_DIM), row),
                  pl.BlockSpec((SSD_CHUNK, N_GD), row),
                  pl.BlockSpec((1, SSM_HEADS, SSD_CHUNK), lambda b, c: (b, 0, c)),
                  pl.BlockSpec((1, 8, SSM_CONV_DIM), lambda b, c: (b, 0, 0)),
                  pl.BlockSpec((1, SSM_HEADS, SSM_HEAD_DIM, SSM_STATE), lambda b, c: (b, 0, 0, 0)),
                  pl.BlockSpec((SSM_CONV, SSM_CONV_DIM), fixed2),
                  pl.BlockSpec((1, SSM_CONV_DIM), fixed2),
                  pl.BlockSpec((1, SSM_HEADS), fixed2), pl.BlockSpec((SSM_HEADS, 1), fixed2),
                  pl.BlockSpec((1, SSM_HEADS), fixed2), pl.BlockSpec((SSM_HEADS, 1), fixed2),
                  pl.BlockSpec((1, SSM_D_INNER), fixed2), pl.BlockSpec((1, SSM_D_INNER), fixed2)],
        out_specs=[pl.BlockSpec((SSD_CHUNK, SSM_D_INNER), row),
                   pl.BlockSpec((1, SSM_HEADS, SSM_HEAD_DIM, SSM_STATE), lambda b, c: (b, 0, 0, 0))],
        out_shape=[jax.ShapeDtypeStruct((m, SSM_D_INNER), f32),
                   jax.ShapeDtypeStruct((batch, SSM_HEADS, SSM_HEAD_DIM, SSM_STATE), f32)],
        scratch_shapes=[pltpu.VMEM((SSM_HEADS, SSM_HEAD_DIM, SSM_STATE), f32),
                        pltpu.VMEM((SSD_CHUNK + 8, SSM_CONV_DIM), f32)],
        compiler_params=pltpu.CompilerParams(dimension_semantics=("arbitrary", "arbitrary"),
                                             vmem_limit_bytes=VMEM_LIMIT_BYTES),
        name="ssd_mixer",
    )(z, xbc, gd, dtt, conv0, ssm_state.astype(f32), conv_w.astype(f32), vec(conv_b),
      vec(dt_bias), col(dt_bias), vec(a_log), col(a_log),
      vec(jnp.repeat(d_skip, SSM_HEAD_DIM)), vec(norm_g))
    return y, h_last


def _nsa_cmp_kernel(q_ref, x_ref, pe_ref, w1_ref, b1_ref, w2_ref, b2_ref, ov_ref, o_ref, sel_ref, cmp_scr):
    f32, bf16 = jnp.float32, jnp.bfloat16
    i = pl.program_id(1)
    tq = q_ref.shape[1]
    n_seg = x_ref.shape[2]
    n_sel = ov_ref.shape[1]
    n_cg = 2 * NSA_KV_HEADS

    @pl.when(i == 0)
    def _():
        for cg in range(n_cg):
            c = cg // NSA_KV_HEADS
            x = x_ref[0, cg]
            a0 = jnp.dot((x + pe_ref[c, 0]).astype(bf16), w1_ref[c, 0], preferred_element_type=f32)
            a1 = jnp.dot((x + pe_ref[c, 1]).astype(bf16), w1_ref[c, 1], preferred_element_type=f32)
            hid = jax.nn.gelu(b1_ref[c] + a0 + pltpu.roll(a1, n_seg - 1, axis=0))
            cmp_scr[cg] = jnp.dot(hid.astype(bf16), w2_ref[c], preferred_element_type=f32) + b2_ref[c]

    q = q_ref[0]
    tpos = i * tq + lax.broadcasted_iota(jnp.int32, (tq, 1), 0)
    n_io = lax.broadcasted_iota(jnp.int32, (1, n_seg), 1)
    cmp_ok = (n_io * NSA_CMP_STRIDE + NSA_CMP_BLOCK - 1 <= tpos) & (n_io < n_seg - 1)
    sidx = lax.broadcasted_iota(jnp.int32, (1, n_sel), 1)
    blk_t = tpos // NSA_SEL_BLOCK
    valid = sidx <= blk_t
    forced = valid & ((sidx == 0) | (sidx > blk_t - NSA_LOCAL_BLOCKS))
    eye = jnp.where(lax.broadcasted_iota(jnp.int32, (n_sel, n_sel), 0)
                    == lax.broadcasted_iota(jnp.int32, (n_sel, n_sel), 1), 1.0, 0.0)
    outs = []
    for g in range(NSA_KV_HEADS):
        kc = cmp_scr[g].astype(bf16)
        vc = cmp_scr[NSA_KV_HEADS + g].astype(bf16)
        psum = jnp.zeros((tq, n_seg), f32)
        for j in range(NSA_GQ):
            h0 = (g * NSA_GQ + j) * HEAD_DIM
            hq = (q[:, h0:h0 + HEAD_DIM] * ATTN_SCALE).astype(bf16)
            s = lax.dot_general(hq, kc, (((1,), (1,)), ((), ())), preferred_element_type=f32)
            s = jnp.where(cmp_ok, s, NEG_INF)
            p = jnp.where(cmp_ok, jnp.exp(s - jnp.max(s, axis=-1, keepdims=True)), 0.0)
            p = p / jnp.maximum(jnp.sum(p, axis=-1, keepdims=True), 1e-30)
            outs.append(jnp.dot(p.astype(bf16), vc, preferred_element_type=f32))
            psum = psum + p
        imp = jnp.dot(psum, ov_ref[...], precision=lax.Precision.HIGHEST, preferred_element_type=f32)
        imp = jnp.where(forced, NSA_FORCE, imp)
        imp = jnp.where(valid, imp, NEG_INF)
        sel = jnp.zeros((tq, n_sel), f32)
        for _ in range(min(NSA_TOPN, n_sel)):
            mx = jnp.max(imp, axis=-1, keepdims=True)
            first = jnp.min(jnp.where(imp == mx, sidx, n_sel), axis=-1, keepdims=True)
            hit = sidx == first
            sel = jnp.where(hit & (mx > 0.5 * NEG_INF), 1.0, sel)
            imp = jnp.where(hit, REMOVED, imp)
        sel_ref[0, g] = lax.dot_general(eye, sel, (((1,), (1,)), ((), ())), preferred_element_type=f32)
    o_ref[0] = jnp.concatenate(outs, axis=1)


def nsa_compressed_branch(q, kvc, batch, cmp_pos, w1, b1, w2, b2):
    f32, bf16 = jnp.float32, jnp.bfloat16
    t = q.shape[0] // batch
    tq = ATTN_TILE
    S = NSA_CMP_STRIDE
    r = NSA_CMP_BLOCK // S
    assert t % tq == 0 and r == 2
    n_seg, n_sel = t // S, t // NSA_SEL_BLOCK
    n_cg = 2 * NSA_KV_HEADS
    x = jnp.transpose(kvc.reshape(batch, n_seg, S, n_cg, HEAD_DIM), (0, 3, 1, 2, 4)).reshape(batch, n_cg, n_seg, S * HEAD_DIM)
    pe = cmp_pos.reshape(2, r, 1, S * HEAD_DIM).astype(f32)
    w1r = w1.reshape(2, r, S * HEAD_DIM, -1).astype(bf16)
    hid = w1r.shape[-1]
    cmp_start = S * jnp.arange(n_seg, dtype=jnp.int32)
    sel_start = NSA_SEL_BLOCK * jnp.arange(n_sel, dtype=jnp.int32)
    overlap = ((cmp_start[:, None] < sel_start[None, :] + NSA_SEL_BLOCK)
               & (cmp_start[:, None] + NSA_CMP_BLOCK > sel_start[None, :])).astype(f32)
    full = lambda a: pl.BlockSpec(a.shape, lambda b, i: (0,) * a.ndim)
    b1r, b2r, w2r = b1.reshape(2, 1, hid).astype(f32), b2.reshape(2, 1, HEAD_DIM).astype(f32), w2.astype(bf16)
    o_cmp, sel = pl.pallas_call(
        _nsa_cmp_kernel,
        grid=(batch, t // tq),
        in_specs=[pl.BlockSpec((1, tq, N_Q), lambda b, i: (b, i, 0)),
                  pl.BlockSpec((1, n_cg, n_seg, S * HEAD_DIM), lambda b, i: (b, 0, 0, 0)),
                  full(pe), full(w1r), full(b1r), full(w2r), full(b2r), full(overlap)],
        out_specs=[pl.BlockSpec((1, tq, N_Q), lambda b, i: (b, i, 0)),
                   pl.BlockSpec((1, NSA_KV_HEADS, n_sel, tq), lambda b, i: (b, 0, 0, i))],
        out_shape=[jax.ShapeDtypeStruct((batch, t, N_Q), f32),
                   jax.ShapeDtypeStruct((batch, NSA_KV_HEADS, n_sel, t), f32)],
        scratch_shapes=[pltpu.VMEM((n_cg, n_seg, HEAD_DIM), f32)],
        compiler_params=pltpu.CompilerParams(dimension_semantics=("arbitrary", "arbitrary"),
                                             vmem_limit_bytes=VMEM_LIMIT_BYTES),
        name="nsa_cmp",
    )(q.reshape(batch, t, N_Q), x, pe, w1r, b1r, w2r, b2r, overlap)
    return o_cmp, sel


ATTN_TILE = 256


def _flash_t_kernel(*refs, mode, n_kblocks):
    f32, bf16 = jnp.float32, jnp.bfloat16
    if mode == 'sel':
        q_ref, k_ref, v_ref, sel_ref, o_ref, qt_scr, m_scr, l_scr, acc_scr = refs
    elif mode == 'moba':
        q_ref, k_ref, v_ref, o_ref, qt_scr, m_scr, l_scr, acc_scr, selm_scr = refs
    else:
        q_ref, k_ref, v_ref, o_ref, qt_scr, m_scr, l_scr, acc_scr = refs
    tq = tk = ATTN_TILE
    D = HEAD_DIM
    i = pl.program_id(2)
    nh = q_ref.shape[2] // D
    gq = nh // 2
    qt = q_ref[0].T
    qt_scr[...] = (qt * ATTN_SCALE).astype(bf16)
    m_scr[...] = jnp.full(m_scr.shape, NEG_INF, f32)
    l_scr[...] = jnp.zeros(l_scr.shape, f32)
    acc_scr[...] = jnp.zeros(acc_scr.shape, f32)
    tpos = i * tq + lax.broadcasted_iota(jnp.int32, (1, tq), 1)
    krow = lax.broadcasted_iota(jnp.int32, (tk, 1), 0)
    brow = lax.broadcasted_iota(jnp.int32, (n_kblocks, tq), 0)

    if mode == 'moba':
        kmean = jnp.mean(k_ref[0].reshape(n_kblocks, tk, 2 * D), axis=1)
        valid = brow < i
        for jj in range(nh):
            gg = jj // gq
            gate = jnp.dot(kmean[:, gg * D:(gg + 1) * D], qt[jj * D:(jj + 1) * D, :],
                           precision=lax.Precision.HIGHEST, preferred_element_type=f32)
            gate = jnp.where(valid, gate, NEG_INF)
            sel = jnp.zeros((n_kblocks, tq), f32)
            for _ in range(min(MOBA_TOPK, n_kblocks)):
                mx = jnp.max(gate, axis=0, keepdims=True)
                first = jnp.min(jnp.where(gate == mx, brow, n_kblocks), axis=0, keepdims=True)
                hit = brow == first
                sel = jnp.where(hit, 1.0, sel)
                gate = jnp.where(hit, REMOVED, gate)
            selm_scr[jj] = jnp.where(valid, sel, 0.0)

    def body(n, carry):
        off = pl.multiple_of(n * tk, tk)
        kblk = k_ref[0, pl.ds(off, tk), :]
        vblk_t = v_ref[0, pl.ds(off, tk), :].T
        kpos = n * tk + krow
        base = kpos <= tpos
        if mode == 'win':
            base = base & (tpos - kpos < NSA_WINDOW)
        for gg in range(2):
            kb = kblk[:, gg * D:(gg + 1) * D].astype(bf16)
            vt = vblk_t[gg * D:(gg + 1) * D, :].astype(bf16)
            mask_g = base
            if mode == 'sel':
                st = sel_ref[0, gg]
                per = tk // NSA_SEL_BLOCK
                srow = lax.broadcasted_iota(jnp.int32, (st.shape[0], 1), 0)
                pieces = []
                for r in range(per):
                    row = jnp.sum(jnp.where(srow == n * per + r, st, 0.0), axis=0, keepdims=True)
                    pieces.append(jnp.broadcast_to(row, (NSA_SEL_BLOCK, tq)))
                mask_g = base & (jnp.concatenate(pieces, axis=0) > 0.0)
            for j in range(gq):
                jj = gg * gq + j
                mask = mask_g
                if mode == 'moba':
                    row = jnp.sum(jnp.where(brow == n, selm_scr[jj], 0.0), axis=0, keepdims=True)
                    own = jnp.where(n == i, 1.0, 0.0)
                    mask = base & ((row + own) > 0.0)
                s = jnp.dot(kb, qt_scr[jj * D:(jj + 1) * D, :], preferred_element_type=f32)
                s = jnp.where(mask, s, NEG_INF)
                m_old = m_scr[jj:jj + 1, :]
                m_new = jnp.maximum(m_old, jnp.max(s, axis=0, keepdims=True))
                p = jnp.where(mask, jnp.exp(s - m_new), 0.0)
                alpha = jnp.exp(m_old - m_new)
                l_scr[jj:jj + 1, :] = alpha * l_scr[jj:jj + 1, :] + jnp.sum(p, axis=0, keepdims=True)
                acc_scr[jj * D:(jj + 1) * D, :] = (alpha * acc_scr[jj * D:(jj + 1) * D, :]
                                                   + jnp.dot(vt, p.astype(bf16), preferred_element_type=f32))
                m_scr[jj:jj + 1, :] = m_new
        return carry

    if mode == 'win':
        lo = jnp.maximum(i - (NSA_WINDOW + tk - 1) // tk, 0)
    else:
        lo = 0
    lax.fori_loop(lo, i + 1, body, 0)
    outs = [acc_scr[jj * D:(jj + 1) * D, :] / jnp.maximum(l_scr[jj:jj + 1, :], 1e-30) for jj in range(nh)]
    o_ref[0] = jnp.concatenate(outs, axis=0).T


def flash_attention_t(q, kv, mode, batch, sel_t=None):
    f32 = jnp.float32
    t = q.shape[0] // batch
    tq = ATTN_TILE
    assert t % tq == 0
    n_pairs = NSA_KV_HEADS // 2
    pw = 2 * NSA_GQ * HEAD_DIM
    q3 = q.reshape(batch, t, N_Q)
    kv3 = kv.reshape(batch, t, N_KV)
    args = [q3, kv3, kv3]
    in_specs = [pl.BlockSpec((1, tq, pw), lambda b, g, i: (b, i, g)),
                pl.BlockSpec((1, t, 2 * HEAD_DIM), lambda b, g, i: (b, 0, g)),
                pl.BlockSpec((1, t, 2 * HEAD_DIM), lambda b, g, i: (b, 0, n_pairs + g))]
    scratch = [pltpu.VMEM((pw, tq), jnp.bfloat16), pltpu.VMEM((pw // HEAD_DIM, tq), f32),
               pltpu.VMEM((pw // HEAD_DIM, tq), f32), pltpu.VMEM((pw, tq), f32)]
    if mode == 'sel':
        args.append(sel_t)
        in_specs.append(pl.BlockSpec((1, 2, sel_t.shape[2], tq), lambda b, g, i: (b, g, 0, i)))
    if mode == 'moba':
        scratch.append(pltpu.VMEM((pw // HEAD_DIM, t // tq, tq), f32))
    return pl.pallas_call(
        functools.partial(_flash_t_kernel, mode=mode, n_kblocks=t // tq),
        grid=(batch, n_pairs, t // tq),
        in_specs=in_specs,
        out_specs=pl.BlockSpec((1, tq, pw), lambda b, g, i: (b, i, g)),
        out_shape=jax.ShapeDtypeStruct((batch, t, N_Q), f32),
        scratch_shapes=scratch,
        compiler_params=pltpu.CompilerParams(dimension_semantics=("arbitrary", "arbitrary", "arbitrary"),
                                             vmem_limit_bytes=VMEM_LIMIT_BYTES),
        name="flash_" + mode,
    )(*args)


PEER_N_EXPERTS = PEER_N_KEYS * PEER_N_KEYS
PEER_SCORE_TILE = 256
PEER_TOKEN_TILE = 512
PEER_EXPERT_TILE = 1024
REMOVED = -3e38


def _top_desc(s, k):
    outs = []
    for r in range(k):
        m = jnp.max(s, axis=0, keepdims=True)
        outs.append(m)
        if r + 1 < k:
            s = jnp.where(s == m, REMOVED, s)
    return jnp.concatenate(outs, axis=0)


def _peer_score_kernel(x_ref, g_ref, wq_ref, sk_ref, xt_ref, s1_ref, s2_ref, e1_ref, e2_ref, tau_ref):
    f32 = jnp.float32
    x = x_ref[...]
    xn = x * lax.rsqrt(jnp.mean(x * x, axis=-1, keepdims=True) + NORM_EPS) * g_ref[...]
    xt_ref[...] = xn.T.astype(jnp.bfloat16)
    q = jnp.dot(xn.astype(jnp.bfloat16), wq_ref[...], preferred_element_type=f32)
    half = PEER_KEY_DIM // 2
    taus = []
    for h in range(PEER_HEADS):
        st = []
        for c in range(2):
            qhc = q[:, (2 * h + c) * half:(2 * h + c + 1) * half]
            st.append(lax.dot_general(sk_ref[c], qhc, (((1,), (1,)), ((), ())),
                                      precision=lax.Precision.HIGHEST, preferred_element_type=f32))
        t1 = _top_desc(st[0], PEER_TOPK)
        t2 = _top_desc(st[1], PEER_TOPK)
        cand = jnp.concatenate([t1[i:i + 1] + t2[:PEER_TOPK // (i + 1)] for i in range(PEER_TOPK)], axis=0)
        tops = _top_desc(cand, PEER_TOPK)
        z = jnp.sum(jnp.exp(tops - tops[0:1]), axis=0, keepdims=True)
        taus.append(tops[PEER_TOPK - 1:PEER_TOPK])
        s1_ref[h] = st[0]
        s2_ref[h] = st[1]
        e1_ref[h] = jnp.exp(st[0] - t1[0:1]) / z
        e2_ref[h] = jnp.exp(st[1] - t2[0:1])
    tau_ref[...] = jnp.concatenate(taus, axis=0)


def _peer_dense_kernel(xres_ref, xt_ref, s1_ref, s2_ref, e1_ref, e2_ref, tau_ref, u_ref, vt_ref, o_ref,
                       act_scr, w_scr, yt_scr):
    f32 = jnp.float32
    j = pl.program_id(1)
    tm = xt_ref.shape[1]
    n_a = PEER_EXPERT_TILE // PEER_N_KEYS

    @pl.when(j == 0)
    def _():
        yt_scr[...] = jnp.zeros(yt_scr.shape, f32)

    act_scr[...] = jnp.dot(u_ref[...], xt_ref[...], preferred_element_type=f32)
    a_rows = pl.ds(pl.multiple_of(j * n_a, n_a), n_a)
    for aa in range(n_a):
        for tc in range(tm // LANES):
            lanes = slice(tc * LANES, (tc + 1) * LANES)
            acc = jnp.zeros((PEER_N_KEYS, LANES), f32)
            for h in range(PEER_HEADS):
                val = s2_ref[h, :, lanes] + s1_ref[h, a_rows, lanes][aa:aa + 1]
                gate = e2_ref[h, :, lanes] * e1_ref[h, a_rows, lanes][aa:aa + 1]
                acc = acc + jnp.where(val >= tau_ref[h:h + 1, lanes], gate, 0.0)
            act = act_scr[aa * PEER_N_KEYS:(aa + 1) * PEER_N_KEYS, lanes]
            w_scr[aa * PEER_N_KEYS:(aa + 1) * PEER_N_KEYS, lanes] = (acc * jax.nn.gelu(act)).astype(jnp.bfloat16)
    yt_scr[...] += jnp.dot(vt_ref[...], w_scr[...], preferred_element_type=f32)

    @pl.when(j == pl.num_programs(1) - 1)
    def _():
        o_ref[...] = xres_ref[...] + yt_scr[...].T


def peer_ffn_dense(x_res, g_norm, wq_bf16, subkeys, u_bf16, vt_bf16):
    n, d = x_res.shape
    tm = PEER_TOKEN_TILE if n % PEER_TOKEN_TILE == 0 else LANES
    ts = PEER_SCORE_TILE if n % PEER_SCORE_TILE == 0 else LANES
    n_pad = -(-n // tm) * tm
    xp = jnp.pad(x_res, ((0, n_pad - n), (0, 0)))
    hk = (PEER_HEADS, PEER_N_KEYS, n_pad)
    stat_spec = lambda t: pl.BlockSpec((PEER_HEADS, PEER_N_KEYS, t), lambda i, *_: (0, 0, i))
    xt, s1, s2, e1, e2, tau = pl.pallas_call(
        _peer_score_kernel,
        grid=(n_pad // ts,),
        in_specs=[pl.BlockSpec((ts, d), lambda i: (i, 0)),
                  pl.BlockSpec((1, d), lambda i: (0, 0)),
                  pl.BlockSpec(wq_bf16.shape, lambda i: (0, 0)),
                  pl.BlockSpec(subkeys.shape, lambda i: (0, 0, 0))],
        out_specs=[pl.BlockSpec((d, ts), lambda i: (0, i)), stat_spec(ts), stat_spec(ts), stat_spec(ts), stat_spec(ts),
                   pl.BlockSpec((PEER_HEADS, ts), lambda i: (0, i))],
        out_shape=[jax.ShapeDtypeStruct((d, n_pad), jnp.bfloat16)] + [jax.ShapeDtypeStruct(hk, jnp.float32)] * 4
                  + [jax.ShapeDtypeStruct((PEER_HEADS, n_pad), jnp.float32)],
        compiler_params=pltpu.CompilerParams(dimension_semantics=("arbitrary",),
                                             vmem_limit_bytes=VMEM_LIMIT_BYTES),
        name="peer_score",
    )(xp, g_norm.reshape(1, d).astype(jnp.float32), wq_bf16, subkeys.astype(jnp.float32))
    te = PEER_EXPERT_TILE
    out = pl.pallas_call(
        _peer_dense_kernel,
        grid=(n_pad // tm, PEER_N_EXPERTS // te),
        in_specs=[pl.BlockSpec((tm, d), lambda i, j: (i, 0)),
                  pl.BlockSpec((d, tm), lambda i, j: (0, i)),
                  stat_spec(tm), stat_spec(tm), stat_spec(tm), stat_spec(tm),
                  pl.BlockSpec((PEER_HEADS, tm), lambda i, j: (0, i)),
                  pl.BlockSpec((te, d), lambda i, j: (j, 0)),
                  pl.BlockSpec((d, te), lambda i, j: (0, j))],
        out_specs=pl.BlockSpec((tm, d), lambda i, j: (i, 0)),
        out_shape=jax.ShapeDtypeStruct((n_pad, d), jnp.float32),
        scratch_shapes=[pltpu.VMEM((te, tm), jnp.float32), pltpu.VMEM((te, tm), jnp.bfloat16),
                        pltpu.VMEM((d, tm), jnp.float32)],
        compiler_params=pltpu.CompilerParams(dimension_semantics=("arbitrary", "arbitrary"),
                                             vmem_limit_bytes=VMEM_LIMIT_BYTES),
        name="peer_dense",
    )(xp, xt, s1, s2, e1, e2, tau, u_bf16, vt_bf16)
    return out[:n]


def rmsnorm(x, g):
    xf = x.astype(jnp.float32)
    y = xf * lax.rsqrt(jnp.mean(xf * xf, axis=-1, keepdims=True) + NORM_EPS)
    return (y * g.astype(jnp.float32)).astype(x.dtype)


def split_cols(a, widths):
    outs, off = [], 0
    for w in widths:
        outs.append(a[..., off:off + w])
        off += w
    return outs


def pad_axis1(a, n):
    return jnp.pad(a, [(0, 0), (0, n - a.shape[1])] + [(0, 0)] * (a.ndim - 2))


def qblocks(T, qmax):
    qb = min(qmax, T)
    nb = -(-T // qb)
    return qb, nb, nb * qb


def run_blocks(fn, nb, qb, T):
    out = lax.map(fn, jnp.arange(nb))
    out = jnp.moveaxis(out, 0, 1)
    return out.reshape(out.shape[:1] + (nb * qb,) + out.shape[3:])[:, :T]


def masked_softmax(s, mask):
    s = jnp.where(mask, s.astype(jnp.float32), NEG_INF)
    m = jnp.max(s, axis=-1, keepdims=True)
    p = jnp.exp(s - m) * mask
    return p / jnp.maximum(jnp.sum(p, axis=-1, keepdims=True), 1e-30)


def partial_rope(x, pos):
    half = ROT_DIM // 2
    inv = ROPE_THETA ** (-jnp.arange(0, ROT_DIM, 2, dtype=jnp.float32) / ROT_DIM)
    ang = pos.astype(jnp.float32)[:, None] * inv[None, :]
    cos = jnp.cos(ang)[:, None, :]
    sin = jnp.sin(ang)[:, None, :]
    x1 = x[..., :half].astype(jnp.float32)
    x2 = x[..., half:ROT_DIM].astype(jnp.float32)
    rot = jnp.concatenate([x1 * cos - x2 * sin, x2 * cos + x1 * sin], axis=-1).astype(x.dtype)
    return jnp.concatenate([rot, x[..., ROT_DIM:]], axis=-1)


def rope_keys(kv, pos):
    return jnp.stack([partial_rope(kv[:, :, 0], pos), kv[:, :, 1]], axis=2)


def gather_pages(pool, page_table):
    g = pool[page_table]
    return g.reshape((g.shape[0], g.shape[1] * g.shape[2]) + g.shape[3:])


def ssd_chunked(x, dt, a, b_h, c_h, h0):
    B, L, H, P = x.shape
    N = b_h.shape[-1]
    f32 = jnp.float32
    Q = min(SSD_CHUNK, L)
    nc = -(-L // Q)
    Lp = nc * Q
    xdt = pad_axis1(x.astype(f32) * dt[..., None], Lp).reshape(B, nc, Q, H, P)
    da = pad_axis1(dt * a, Lp).reshape(B, nc, Q, H)
    bc = pad_axis1(b_h.astype(f32), Lp).reshape(B, nc, Q, H, N)
    cc = pad_axis1(c_h.astype(f32), Lp).reshape(B, nc, Q, H, N)
    acum = jnp.cumsum(da, axis=2)
    causal = jnp.tril(jnp.ones((Q, Q), bool))
    seg = acum[:, :, :, None, :] - acum[:, :, None, :, :]
    decay_in = jnp.exp(jnp.where(causal[None, None, :, :, None], seg, NEG_INF))
    scores = jnp.einsum('bclhn,bcshn->bclsh', cc, bc) * decay_in
    y_diag = jnp.einsum('bclsh,bcshp->bclhp', scores, xdt)
    decay_out = jnp.exp(acum[:, :, -1:] - acum)
    chunk_states = jnp.einsum('bcshn,bcshp->bchpn', bc * decay_out[..., None], xdt)
    chunk_decay = jnp.exp(acum[:, :, -1])

    def step(h, inp):
        st, dec = inp
        return h * dec[:, :, None, None] + st, h

    h_last, h_enter = lax.scan(step, h0.astype(f32),
                               (jnp.moveaxis(chunk_states, 1, 0), jnp.moveaxis(chunk_decay, 1, 0)))
    h_enter = jnp.moveaxis(h_enter, 0, 1)
    y_off = jnp.einsum('bclhn,bchpn->bclhp', cc * jnp.exp(acum)[..., None], h_enter)
    y = (y_diag + y_off).reshape(B, Lp, H, P)[:, :L]
    return y, h_last


def mamba2_ssd(z, xbc, dt_raw, conv_state, ssm_state, conv_w, conv_b, dt_bias, a_log, d_skip, norm_g):
    B, T, _ = xbc.shape
    f32 = jnp.float32
    xpad = jnp.concatenate([conv_state.astype(xbc.dtype), xbc], axis=1)
    new_conv = xpad[:, T:]
    conv = conv_b
    for k in range(SSM_CONV):
        conv = conv + xpad[:, k:k + T] * conv_w[k]
    xbc_c = jax.nn.silu(conv)
    n_bc = SSM_GROUPS * SSM_STATE
    rep = SSM_HEADS // SSM_GROUPS
    xs = xbc_c[..., :SSM_D_INNER].reshape(B, T, SSM_HEADS, SSM_HEAD_DIM)
    b_h = jnp.repeat(xbc_c[..., SSM_D_INNER:SSM_D_INNER + n_bc].reshape(B, T, SSM_GROUPS, SSM_STATE), rep, axis=2)
    c_h = jnp.repeat(xbc_c[..., SSM_D_INNER + n_bc:].reshape(B, T, SSM_GROUPS, SSM_STATE), rep, axis=2)
    dt = jax.nn.softplus(dt_raw.astype(f32) + dt_bias.astype(f32))
    a = -jnp.exp(a_log.astype(f32))
    y, h_last = ssd_chunked(xs, dt, a, b_h, c_h, ssm_state)
    y = y + d_skip.astype(f32)[:, None] * xs.astype(f32)
    y = y.reshape(B, T, SSM_D_INNER) * jax.nn.silu(z.astype(f32))
    yg = y.reshape(B, T, SSM_GROUPS, SSM_D_INNER // SSM_GROUPS)
    yg = yg * lax.rsqrt(jnp.mean(yg * yg, axis=-1, keepdims=True) + NORM_EPS)
    y = yg.reshape(B, T, SSM_D_INNER) * norm_g.astype(f32)
    return y.astype(xbc.dtype), new_conv, h_last.astype(ssm_state.dtype)


def nsa_compress(kv_all, pos_emb, w1, b1, w2, b2):
    B, L = kv_all.shape[:2]
    S = NSA_CMP_STRIDE
    r = NSA_CMP_BLOCK // S
    n_seg = L // S
    n_cmp = n_seg - r + 1
    seg = kv_all[:, :n_seg * S].reshape(B, n_seg, S, 2, NSA_KV_HEADS, HEAD_DIM)
    h = b1[None, None, :, None, :]
    for j in range(r):
        pe = jnp.transpose(pos_emb[:, j * S:(j + 1) * S], (1, 0, 2))[:, :, None, :]
        h = h + jnp.einsum('bnlcgd,cldh->bncgh', seg[:, j:j + n_cmp] + pe, w1[:, j * S:(j + 1) * S])
    h = jax.nn.gelu(h)
    return jnp.einsum('bncgh,chd->bncgd', h, w2) + b2[None, None, :, None, :]


def nsa_selected(qr, kvs_all, sel_idx, sel_ok, pos0):
    B, T = qr.shape[:2]
    Lk = kvs_all.shape[1]
    n_sel = -(-Lk // NSA_SEL_BLOCK)
    topn = sel_idx.shape[-1]
    kvb = pad_axis1(kvs_all, n_sel * NSA_SEL_BLOCK).reshape(B, n_sel, NSA_SEL_BLOCK, 2, NSA_KV_HEADS, HEAD_DIM)
    kvb = jnp.transpose(kvb, (0, 4, 1, 2, 3, 5))
    qb, nb, Tp = qblocks(T, SEL_QBLOCK)
    qp, ip, okp = pad_axis1(qr, Tp), pad_axis1(sel_idx, Tp), pad_axis1(sel_ok, Tp)
    bi = jnp.arange(B)[:, None, None, None]
    gi = jnp.arange(NSA_KV_HEADS)[None, None, :, None]
    n_keys = topn * NSA_SEL_BLOCK

    def blk(i):
        start = i * qb
        q_b = lax.dynamic_slice_in_dim(qp, start, qb, 1)
        i_b = lax.dynamic_slice_in_dim(ip, start, qb, 1)
        ok_b = lax.dynamic_slice_in_dim(okp, start, qb, 1)
        qpos = pos0 + start + jnp.arange(qb)
        g = kvb[bi, gi, i_b]
        kpos = i_b[..., None] * NSA_SEL_BLOCK + jnp.arange(NSA_SEL_BLOCK)
        mask = (ok_b[..., None] & (kpos <= qpos[None, :, None, None, None])).reshape(B, qb, NSA_KV_HEADS, 1, n_keys)
        kk = g[..., 0, :].reshape(B, qb, NSA_KV_HEADS, n_keys, HEAD_DIM)
        vv = g[..., 1, :].reshape(B, qb, NSA_KV_HEADS, n_keys, HEAD_DIM)
        p = masked_softmax(jnp.einsum('bqgjd,bqgkd->bqgjk', q_b, kk) * ATTN_SCALE, mask)
        return jnp.einsum('bqgjk,bqgkd->bqgjd', p, vv)

    return run_blocks(blk, nb, qb, T)


def nsa_window(qr, kvw_ext, pos0):
    B, T = qr.shape[:2]
    Wb = kvw_ext.shape[1] - T
    W = NSA_WINDOW
    qb, nb, Tp = qblocks(T, WIN_QBLOCK)
    qp = pad_axis1(qr, Tp)
    kvp = jnp.pad(kvw_ext, [(0, 0), (W, Tp - T), (0, 0), (0, 0), (0, 0)])
    n_kp = W + Wb + Tp
    idx = jnp.arange(n_kp)
    kpos = (pos0 - Wb - W) + idx
    kval = (idx >= W) & (idx < W + Wb + T)

    def blk(i):
        start = i * qb
        q_b = lax.dynamic_slice_in_dim(qp, start, qb, 1)
        kv_b = lax.dynamic_slice_in_dim(kvp, start + Wb, W + qb, 1)
        kp_b = lax.dynamic_slice_in_dim(kpos, start + Wb, W + qb, 0)
        ok_b = lax.dynamic_slice_in_dim(kval, start + Wb, W + qb, 0)
        qpos = pos0 + start + jnp.arange(qb)
        mask = ok_b[None, :] & (kp_b[None, :] <= qpos[:, None]) & (qpos[:, None] - kp_b[None, :] < W)
        s = jnp.einsum('bqgjd,bkgd->bqgjk', q_b, kv_b[:, :, 0]) * ATTN_SCALE
        p = masked_softmax(s, mask[None, :, None, None, :])
        return jnp.einsum('bqgjk,bkgd->bqgjd', p, kv_b[:, :, 1])

    return run_blocks(blk, nb, qb, T)


def nsa_attention(q, qr, kvc_all, kvs_all, kvw_ext, pos0, cmp_pos, cmp_w1, cmp_b1, cmp_w2, cmp_b2):
    B, T = q.shape[:2]
    pos_q = pos0 + jnp.arange(T, dtype=jnp.int32)
    qg = q.reshape(B, T, NSA_KV_HEADS, NSA_GQ, HEAD_DIM)
    qr = qr.reshape(B, T, NSA_KV_HEADS, NSA_GQ, HEAD_DIM)
    cmp = nsa_compress(kvc_all, cmp_pos, cmp_w1, cmp_b1, cmp_w2, cmp_b2)
    n_cmp = cmp.shape[1]
    cmp_start = NSA_CMP_STRIDE * jnp.arange(n_cmp, dtype=jnp.int32)
    cmp_mask = (cmp_start + NSA_CMP_BLOCK - 1)[None, :] <= pos_q[:, None]
    s = jnp.einsum('btgjd,bngd->btgjn', qg, cmp[:, :, 0]) * ATTN_SCALE
    p_cmp = masked_softmax(s, cmp_mask[None, :, None, None, :])
    o_cmp = jnp.einsum('btgjn,bngd->btgjd', p_cmp, cmp[:, :, 1])
    n_sel = -(-kvs_all.shape[1] // NSA_SEL_BLOCK)
    sel_start = NSA_SEL_BLOCK * jnp.arange(n_sel, dtype=jnp.int32)
    overlap = ((cmp_start[:, None] < sel_start[None, :] + NSA_SEL_BLOCK)
               & (cmp_start[:, None] + NSA_CMP_BLOCK > sel_start[None, :])).astype(jnp.float32)
    imp = jnp.einsum('btgjn,ns->btgs', p_cmp, overlap)
    blk_t = pos_q // NSA_SEL_BLOCK
    sidx = jnp.arange(n_sel, dtype=jnp.int32)
    valid = sidx[None, :] <= blk_t[:, None]
    forced = valid & ((sidx[None, :] == 0) | (sidx[None, :] > blk_t[:, None] - NSA_LOCAL_BLOCKS))
    imp = jnp.where(forced[None, :, None, :], NSA_FORCE, imp)
    imp = jnp.where(valid[None, :, None, :], imp, NEG_INF)
    sel_score, sel_idx = lax.top_k(imp, min(NSA_TOPN, n_sel))
    o_sel = nsa_selected(qr, kvs_all, sel_idx, sel_score > 0.5 * NEG_INF, pos0)
    o_win = nsa_window(qr, kvw_ext, pos0)
    return o_cmp, o_sel, o_win


def moba_attention(qr, kv_all, pos0):
    B, T = qr.shape[:2]
    Lk = kv_all.shape[1]
    nblk = -(-Lk // MOBA_BLOCK)
    kvb = pad_axis1(kv_all, nblk * MOBA_BLOCK).reshape(B, nblk, MOBA_BLOCK, 2, MOBA_KV_HEADS, HEAD_DIM)
    kvb = jnp.transpose(kvb, (0, 4, 1, 2, 3, 5))
    kmean = jnp.mean(kvb[..., 0, :].astype(jnp.float32), axis=3)
    pos_q = pos0 + jnp.arange(T, dtype=jnp.int32)
    qg = qr.reshape(B, T, MOBA_KV_HEADS, MOBA_GQ, HEAD_DIM)
    gate = jnp.einsum('btgjd,bgnd->btgjn', qg.astype(jnp.float32), kmean)
    past_ok = jnp.arange(nblk)[None, :] < (pos_q // MOBA_BLOCK)[:, None]
    gate = jnp.where(past_ok[None, :, None, None, :], gate, NEG_INF)
    k = min(MOBA_TOPK, nblk)
    sc, idx = lax.top_k(gate, k)
    ok = sc > 0.5 * NEG_INF
    qb, nb, Tp = qblocks(T, MOBA_QBLOCK)
    qp, ip, okp = pad_axis1(qg, Tp), pad_axis1(idx, Tp), pad_axis1(ok, Tp)
    bi = jnp.arange(B)[:, None, None, None, None]
    gi = jnp.arange(MOBA_KV_HEADS)[None, None, :, None, None]
    bo = jnp.arange(B)[:, None, None]
    go = jnp.arange(MOBA_KV_HEADS)[None, None, :]
    n_sel = k * MOBA_BLOCK

    def blk(i):
        start = i * qb
        q_b = lax.dynamic_slice_in_dim(qp, start, qb, 1)
        i_b = lax.dynamic_slice_in_dim(ip, start, qb, 1)
        ok_b = lax.dynamic_slice_in_dim(okp, start, qb, 1)
        qpos = pos0 + start + jnp.arange(qb)
        own_blk = jnp.minimum(qpos // MOBA_BLOCK, nblk - 1)
        g_sel = kvb[bi, gi, i_b]
        g_own = kvb[bo, go, own_blk[None, :, None]]
        s_sel = jnp.einsum('bqgjd,bqgjkld->bqgjkl', q_b, g_sel[..., 0, :]).reshape(B, qb, MOBA_KV_HEADS, MOBA_GQ, n_sel)
        s_own = jnp.einsum('bqgjd,bqgld->bqgjl', q_b, g_own[..., 0, :])
        m_sel = jnp.broadcast_to(ok_b[..., None], ok_b.shape + (MOBA_BLOCK,)).reshape(B, qb, MOBA_KV_HEADS, MOBA_GQ, n_sel)
        own_pos = own_blk[:, None] * MOBA_BLOCK + jnp.arange(MOBA_BLOCK)
        m_own = jnp.broadcast_to((own_pos <= qpos[:, None])[None, :, None, None, :], (B, qb, MOBA_KV_HEADS, MOBA_GQ, MOBA_BLOCK))
        p = masked_softmax(jnp.concatenate([s_sel, s_own], axis=-1) * ATTN_SCALE,
                           jnp.concatenate([m_sel, m_own], axis=-1))
        v_sel = g_sel[..., 1, :].reshape(B, qb, MOBA_KV_HEADS, MOBA_GQ, n_sel, HEAD_DIM)
        return (jnp.einsum('bqgjm,bqgjmd->bqgjd', p[..., :n_sel], v_sel)
                + jnp.einsum('bqgjl,bqgld->bqgjd', p[..., n_sel:], g_own[..., 1, :]))

    o = run_blocks(blk, nb, qb, T)
    return o.reshape(B, T, MOBA_HEADS * HEAD_DIM).astype(qr.dtype)


def kernel(x_prompt, x_sample, cache_nsa_cmp_kv, cache_nsa_sel_kv, state_nsa_win_kv, state_ssm, state_conv,
           cache_moba_kv, page_table, norm_mix, norm_ffn, norm_final, w_in_even, w_out_even,
           ssm_conv_w, ssm_conv_b, ssm_dt_bias, ssm_a_log, ssm_d_skip, ssm_norm,
           nsa_cmp_pos, nsa_cmp_w1, nsa_cmp_b1, nsa_cmp_w2, nsa_cmp_b2, nsa_gate_b,
           w_in_odd, w_out_odd, peer_wq, peer_subkeys, peer_u, peer_v):

    peer_wq_bf16 = peer_wq.astype(jnp.bfloat16)
    peer_u_bf16 = peer_u.astype(jnp.bfloat16)
    peer_vt_bf16 = jnp.transpose(peer_v, (0, 2, 1)).astype(jnp.bfloat16)
    w_even_bf16 = [relayout_even_weight(w_in_even[e]) for e in range(w_in_even.shape[0])]
    w_odd_bf16 = w_in_odd.astype(jnp.bfloat16)

    def trunk(x, pos0, get_past, prompt):
        B, T, _ = x.shape
        pos_q = pos0 + jnp.arange(T, dtype=jnp.int32)
        new_cmp, new_sel, new_win, new_ssm, new_conv, new_moba = [], [], [], [], [], []
        tabs = rope_tables(pos_q)
        if T % ATTN_TILE != 0:
            tabs = tuple(jnp.tile(a, (B, 1)) for a in tabs)
        for l in range(DEPTH):
            x2 = x.reshape(B * T, D_MODEL)
            if l % 2 == 0:
                e = l // 2
                z, xbc, q, qr, kvc2, kvs2, kvw2, gd, kvc, kvs, kvw = layer_projection(
                    x2, norm_mix[l], w_even_bf16[e], tabs, EVEN_OUT_WIDTHS, _even_proj_kernel)
                kv_shape = (B, T) + KV_CACHE_TAIL
                kvc, kvs, kvw = kvc.reshape(kv_shape), kvs.reshape(kv_shape), kvw.reshape(kv_shape)
                kvw_ext = jnp.concatenate([get_past('win', e), kvw], axis=1)
                if prompt:
                    conv_past = get_past('conv', e)
                    y_ssm, ssm_new = ssd_mixer(z, xbc, gd, conv_past, get_past('ssm', e), ssm_conv_w[e], ssm_conv_b[e],
                                               ssm_dt_bias[e], ssm_a_log[e], ssm_d_skip[e], ssm_norm[e], B)
                    conv_new = jnp.concatenate([conv_past, xbc.reshape(B, T, SSM_CONV_DIM)], axis=1)[:, T:]
                    o_cmp, sel = nsa_compressed_branch(q, kvc2, B, nsa_cmp_pos[e], nsa_cmp_w1[e],
                                                       nsa_cmp_b1[e], nsa_cmp_w2[e], nsa_cmp_b2[e])
                    o_sel = flash_attention_t(qr, kvs2, 'sel', B, sel)
                    o_win = flash_attention_t(qr, kvw2, 'win', B)
                else:
                    y_ssm, conv_new, ssm_new = mamba2_ssd(
                        z.reshape(B, T, -1), xbc.reshape(B, T, -1), gd[:, 3 * NSA_HEADS:].reshape(B, T, SSM_HEADS),
                        get_past('conv', e), get_past('ssm', e), ssm_conv_w[e], ssm_conv_b[e], ssm_dt_bias[e],
                        ssm_a_log[e], ssm_d_skip[e], ssm_norm[e])
                    o_cmp, o_sel, o_win = nsa_attention(
                        q.reshape(B, T, NSA_HEADS, HEAD_DIM), qr,
                        jnp.concatenate([get_past('cmp', e), kvc], axis=1),
                        jnp.concatenate([get_past('sel', e), kvs], axis=1), kvw_ext, pos0,
                        nsa_cmp_pos[e], nsa_cmp_w1[e], nsa_cmp_b1[e], nsa_cmp_w2[e], nsa_cmp_b2[e])
                flat = lambda a: a.reshape(B * T, -1)
                x = even_output(x2, flat(y_ssm), flat(o_cmp), flat(o_sel), flat(o_win), gd, nsa_gate_b[e],
                                w_out_even[e])
                keep = min(NSA_WINDOW, kvw_ext.shape[1])
                new_cmp.append(kvc)
                new_sel.append(kvs)
                new_win.append(kvw_ext[:, kvw_ext.shape[1] - keep:])
                new_ssm.append(ssm_new)
                new_conv.append(conv_new)
            else:
                o = l // 2
                q, kv2, kv = layer_projection(x2, norm_mix[l], w_odd_bf16[o], tabs, ODD_OUT_WIDTHS, _odd_proj_kernel)
                kv = kv.reshape((B, T) + KV_CACHE_TAIL)
                if prompt:
                    y_moba = flash_attention_t(q, kv2, 'moba', B)
                else:
                    y_moba = moba_attention(q.reshape(B, T, MOBA_HEADS, HEAD_DIM),
                                            jnp.concatenate([get_past('moba', o), kv], axis=1), pos0)
                x = odd_output(x2, y_moba.reshape(B * T, -1), w_out_odd[o])
                new_moba.append(kv)
            x = peer_ffn_dense(x, norm_ffn[l], peer_wq_bf16[l], peer_subkeys[l], peer_u_bf16[l], peer_vt_bf16[l])
        y_out = final_norm(x, norm_final).reshape(B, T, D_MODEL)
        return (y_out, jnp.stack(new_cmp), jnp.stack(new_sel), jnp.stack(new_win),
                jnp.stack(new_ssm), jnp.stack(new_conv), jnp.stack(new_moba))

    bp = x_prompt.shape[0]
    dtp = x_prompt.dtype

    def prompt_past(kind, i):
        if kind == 'ssm':
            return jnp.zeros((bp, SSM_HEADS, SSM_HEAD_DIM, SSM_STATE), state_ssm.dtype)
        if kind == 'conv':
            return jnp.zeros((bp, SSM_CONV - 1, SSM_CONV_DIM), dtp)
        if kind == 'moba':
            return jnp.zeros((bp, 0, 2, MOBA_KV_HEADS, HEAD_DIM), dtp)
        return jnp.zeros((bp, 0, 2, NSA_KV_HEADS, HEAD_DIM), dtp)

    def sample_past(kind, i):
        if kind == 'ssm':
            return state_ssm[i]
        if kind == 'conv':
            return state_conv[i]
        if kind == 'win':
            return state_nsa_win_kv[i]
        if kind == 'cmp':
            return gather_pages(cache_nsa_cmp_kv[i], page_table)
        if kind == 'sel':
            return gather_pages(cache_nsa_sel_kv[i], page_table)
        return gather_pages(cache_moba_kv[i], page_table)

    past_len = page_table.shape[1] * cache_nsa_cmp_kv.shape[2]
    y_prompt, p_cmp, p_sel, p_win, p_ssm, p_conv, p_moba = trunk(x_prompt, 0, prompt_past, True)
    y_sample, s_cmp, s_sel, s_win, s_ssm, s_conv, s_moba = trunk(x_sample, past_len, sample_past, False)
    return (y_prompt, y_sample, p_cmp, p_sel, p_win, p_ssm, p_conv, p_moba,
            s_cmp, s_sel, s_win, s_ssm, s_conv, s_moba)
```

```python
import functools
import math

import jax
import jax.numpy as jnp
from jax import lax
from jax.experimental import pallas as pl
from jax.experimental.pallas import tpu as pltpu

D_MODEL = 1024
DEPTH = 4
HEAD_DIM = 64
ROT_DIM = HEAD_DIM // 4
ROPE_THETA = 500000.0
ATTN_SCALE = HEAD_DIM ** -0.5
NORM_EPS = 1e-6
NEG_INF = -1e30

SSM_HEADS = 16
SSM_HEAD_DIM = 64
SSM_D_INNER = SSM_HEADS * SSM_HEAD_DIM
SSM_GROUPS = 2
SSM_STATE = 128
SSM_CONV = 4
SSM_CONV_DIM = SSM_D_INNER + 2 * SSM_GROUPS * SSM_STATE
SSD_CHUNK = 128

NSA_HEADS = 16
NSA_KV_HEADS = 4
NSA_GQ = NSA_HEADS // NSA_KV_HEADS
NSA_CMP_BLOCK = 32
NSA_CMP_STRIDE = 16
NSA_SEL_BLOCK = 64
NSA_TOPN = 8
NSA_LOCAL_BLOCKS = 2
NSA_FORCE = 1e6
NSA_WINDOW = 512

MOBA_HEADS = 16
MOBA_KV_HEADS = 4
MOBA_GQ = MOBA_HEADS // MOBA_KV_HEADS
MOBA_BLOCK = 256
MOBA_TOPK = 3

PEER_HEADS = 8
PEER_N_KEYS = 128
PEER_KEY_DIM = 256
PEER_TOPK = 16

EVEN_WIDTHS = (SSM_D_INNER, SSM_CONV_DIM, SSM_HEADS, NSA_HEADS * HEAD_DIM,
               2 * NSA_KV_HEADS * HEAD_DIM, 2 * NSA_KV_HEADS * HEAD_DIM, 2 * NSA_KV_HEADS * HEAD_DIM,
               3 * NSA_HEADS)
ODD_WIDTHS = (MOBA_HEADS * HEAD_DIM, 2 * MOBA_KV_HEADS * HEAD_DIM)

WIN_QBLOCK = 128
SEL_QBLOCK = 16
MOBA_QBLOCK = 4
PEER_TBLOCK = 128

VMEM_LIMIT_BYTES = 48 * 1024 * 1024


ROW_TILE = 256


def _row_call(body, m, row_inputs, fixed_inputs, n_out, name):
    tm = min(m, ROW_TILE)
    assert m % tm == 0
    return pl.pallas_call(
        body,
        grid=(m // tm,),
        in_specs=[pl.BlockSpec((tm, a.shape[1]), lambda i: (i, 0)) for a in row_inputs]
                 + [pl.BlockSpec(a.shape, lambda i: (0, 0)) for a in fixed_inputs],
        out_specs=pl.BlockSpec((tm, n_out), lambda i: (i, 0)),
        out_shape=jax.ShapeDtypeStruct((m, n_out), jnp.float32),
        compiler_params=pltpu.CompilerParams(dimension_semantics=("arbitrary",),
                                             vmem_limit_bytes=VMEM_LIMIT_BYTES),
        name=name,
    )(*row_inputs, *fixed_inputs)


def _odd_out_kernel(x_ref, y_ref, w_ref, o_ref):
    o_ref[...] = x_ref[...] + jnp.dot(y_ref[...].astype(jnp.bfloat16), w_ref[...], preferred_element_type=jnp.float32)


def _even_out_kernel(x_ref, ssm_ref, cmp_ref, sel_ref, win_ref, gd_ref, gb_ref, ex_ref, wa_ref, wb_ref, o_ref):
    f32, bf16 = jnp.float32, jnp.bfloat16
    gates = jax.nn.sigmoid(gd_ref[...] + gb_ref[...])
    y_nsa = None
    for c, branch in enumerate((cmp_ref, sel_ref, win_ref)):
        g_c = jnp.dot(gates, ex_ref[c], precision=lax.Precision.HIGHEST, preferred_element_type=f32)
        term = g_c * branch[...]
        y_nsa = term if y_nsa is None else y_nsa + term
    o_ref[...] = (x_ref[...] + jnp.dot(ssm_ref[...].astype(bf16), wa_ref[...], preferred_element_type=f32)
                  + jnp.dot(y_nsa.astype(bf16), wb_ref[...], preferred_element_type=f32))


def _final_norm_kernel(x_ref, g_ref, o_ref):
    x = x_ref[...]
    o_ref[...] = x * lax.rsqrt(jnp.mean(x * x, axis=-1, keepdims=True) + NORM_EPS) * g_ref[...]


def even_output(x2, y_ssm, o_cmp, o_sel, o_win, gd, gate_b, w_out):
    f32 = jnp.float32
    n_g = 3 * NSA_HEADS
    gb = jnp.concatenate([gate_b.astype(f32), jnp.zeros((N_GD - n_g,), f32)]).reshape(1, N_GD)
    lane_head = jnp.arange(N_Q) // HEAD_DIM
    expand = jnp.stack([(jnp.arange(N_GD)[:, None] == 3 * lane_head[None, :] + c).astype(f32) for c in range(3)])
    w = w_out.astype(jnp.bfloat16)
    m = x2.shape[0]
    tm = min(m, ROW_TILE)
    row = lambda a: pl.BlockSpec((tm, a.shape[1]), lambda i: (i, 0))
    whole = lambda a: pl.BlockSpec(a.shape, lambda i: (0,) * a.ndim)
    rows = (x2, y_ssm, o_cmp, o_sel, o_win, gd)
    fixed = (gb, expand, w[:SSM_D_INNER], w[SSM_D_INNER:])
    return pl.pallas_call(
        _even_out_kernel,
        grid=(m // tm,),
        in_specs=[row(a) for a in rows] + [whole(a) for a in fixed],
        out_specs=pl.BlockSpec((tm, D_MODEL), lambda i: (i, 0)),
        out_shape=jax.ShapeDtypeStruct((m, D_MODEL), f32),
        compiler_params=pltpu.CompilerParams(dimension_semantics=("arbitrary",),
                                             vmem_limit_bytes=VMEM_LIMIT_BYTES),
        name="even_out",
    )(*rows, *fixed)


def odd_output(x2, y, w_out):
    return _row_call(_odd_out_kernel, x2.shape[0], (x2, y), (w_out.astype(jnp.bfloat16),), D_MODEL, "odd_out")


def final_norm(x2, g):
    return _row_call(_final_norm_kernel, x2.shape[0], (x2,), (g.reshape(1, -1).astype(jnp.float32),), D_MODEL,
                     "final_norm")


LANES = 128
ROT_HALF = ROT_DIM // 2
PROJ_VMEM_LIMIT_BYTES = 56 * 1024 * 1024
N_Q = NSA_HEADS * HEAD_DIM
N_KV = 2 * NSA_KV_HEADS * HEAD_DIM
EVEN_OFF_XBC = SSM_D_INNER
EVEN_OFF_Q = EVEN_OFF_XBC + SSM_CONV_DIM
EVEN_OFF_KVC = EVEN_OFF_Q + N_Q
EVEN_OFF_KVS = EVEN_OFF_KVC + N_KV
EVEN_OFF_KVW = EVEN_OFF_KVS + N_KV
EVEN_OFF_GD = EVEN_OFF_KVW + N_KV
N_GD = 3 * NSA_HEADS + SSM_HEADS
EVEN_TOTAL = EVEN_OFF_GD + N_GD


def rope_tables(pos):
    f32 = jnp.float32
    inv = ROPE_THETA ** (-jnp.arange(0, ROT_DIM, 2, dtype=f32) / ROT_DIM)
    ang = pos.astype(f32)[:, None] * inv[None, :]
    cos, sin = jnp.cos(ang), jnp.sin(ang)
    r = pos.shape[0]
    zeros = lambda n: jnp.zeros((r, n), f32)
    c = jnp.concatenate([cos, cos, jnp.ones((r, HEAD_DIM - ROT_DIM), f32)], axis=1)
    sm = jnp.concatenate([-sin, zeros(HEAD_DIM - ROT_HALF)], axis=1)
    sp = jnp.concatenate([zeros(ROT_HALF), sin, zeros(HEAD_DIM - ROT_DIM)], axis=1)
    rep = LANES // HEAD_DIM
    return tuple(jnp.tile(a, (1, rep)) for a in (c, sm, sp))


def _rope_lanes(x, c, sm, sp):
    w = x.shape[1]
    reps = w // LANES
    tile = lambda a: jnp.concatenate([a] * reps, axis=1) if reps > 1 else a
    return (x * tile(c) + pltpu.roll(x, w - ROT_HALF, axis=1) * tile(sm)
            + pltpu.roll(x, ROT_HALF, axis=1) * tile(sp))


def _rope_keys_lanes(kv, c, sm, sp):
    half = kv.shape[1] // 2
    return jnp.concatenate([_rope_lanes(kv[:, :half], c, sm, sp), kv[:, half:]], axis=1)


def _normed_dot(x_ref, g_ref, w_ref):
    x = x_ref[...]
    xn = x * lax.rsqrt(jnp.mean(x * x, axis=-1, keepdims=True) + NORM_EPS) * g_ref[...]
    return jnp.dot(xn.astype(jnp.bfloat16), w_ref[...], preferred_element_type=jnp.float32)


KV_CACHE_TAIL = (2, NSA_KV_HEADS, HEAD_DIM)


def _store_cache_rows(ref, kv):
    for c in range(KV_CACHE_TAIL[0]):
        for g in range(KV_CACHE_TAIL[1]):
            off = (c * KV_CACHE_TAIL[1] + g) * HEAD_DIM
            ref[:, c, g, :] = kv[:, off:off + HEAD_DIM]


def _even_proj_kernel(x_ref, g_ref, w_ref, c_ref, sm_ref, sp_ref,
                      z_ref, xbc_ref, q_ref, qr_ref, kvc_ref, kvs_ref, kvw_ref, gd_ref,
                      kvc_cache_ref, kvs_cache_ref, kvw_cache_ref):
    o = _normed_dot(x_ref, g_ref, w_ref)
    rope = (c_ref[...], sm_ref[...], sp_ref[...])
    z_ref[...] = o[:, :EVEN_OFF_XBC]
    xbc_ref[...] = o[:, EVEN_OFF_XBC:EVEN_OFF_Q]
    q = o[:, EVEN_OFF_Q:EVEN_OFF_KVC]
    q_ref[...] = q
    qr_ref[...] = _rope_lanes(q, *rope)
    kvc = o[:, EVEN_OFF_KVC:EVEN_OFF_KVS]
    kvs = _rope_keys_lanes(o[:, EVEN_OFF_KVS:EVEN_OFF_KVW], *rope)
    kvw = _rope_keys_lanes(o[:, EVEN_OFF_KVW:EVEN_OFF_GD], *rope)
    kvc_ref[...] = kvc
    kvs_ref[...] = kvs
    kvw_ref[...] = kvw
    gd_ref[...] = o[:, EVEN_OFF_GD:EVEN_TOTAL]
    _store_cache_rows(kvc_cache_ref, kvc)
    _store_cache_rows(kvs_cache_ref, kvs)
    _store_cache_rows(kvw_cache_ref, kvw)


def _odd_proj_kernel(x_ref, g_ref, w_ref, c_ref, sm_ref, sp_ref, qr_ref, kv_ref, kv_cache_ref):
    o = _normed_dot(x_ref, g_ref, w_ref)
    rope = (c_ref[...], sm_ref[...], sp_ref[...])
    qr_ref[...] = _rope_lanes(o[:, :N_Q], *rope)
    kv = _rope_keys_lanes(o[:, N_Q:], *rope)
    kv_ref[...] = kv
    _store_cache_rows(kv_cache_ref, kv)


def relayout_even_weight(w):
    dt0 = SSM_D_INNER + SSM_CONV_DIM
    return jnp.concatenate([w[:, :dt0], w[:, dt0 + SSM_HEADS:], w[:, dt0:dt0 + SSM_HEADS]], axis=1).astype(jnp.bfloat16)


def layer_projection(x2, g, w_bf16, rope_tabs, widths, body):
    m, k = x2.shape
    r = rope_tabs[0].shape[0]
    tm = min(m, r, 256)
    assert m % tm == 0 and r % tm == 0
    nr = r // tm
    row = lambda i: (i, 0)
    fixed = lambda i: (0, 0)
    tab = pl.BlockSpec((tm, LANES), lambda i: (i % nr, 0))
    return pl.pallas_call(
        body,
        grid=(m // tm,),
        in_specs=[pl.BlockSpec((tm, k), row), pl.BlockSpec((1, k), fixed), pl.BlockSpec(w_bf16.shape, fixed),
                  tab, tab, tab],
        out_specs=[pl.BlockSpec((tm,) + tail, lambda i, nd=len(tail): (i,) + (0,) * nd) for tail in widths],
        out_shape=[jax.ShapeDtypeStruct((m,) + tail, jnp.float32) for tail in widths],
        compiler_params=pltpu.CompilerParams(dimension_semantics=("arbitrary",),
                                             vmem_limit_bytes=PROJ_VMEM_LIMIT_BYTES),
        name=body.__name__.strip("_"),
    )(x2, g.reshape(1, k).astype(jnp.float32), w_bf16, *rope_tabs)


EVEN_OUT_WIDTHS = ((SSM_D_INNER,), (SSM_CONV_DIM,), (N_Q,), (N_Q,), (N_KV,), (N_KV,), (N_KV,), (N_GD,),
                   KV_CACHE_TAIL, KV_CACHE_TAIL, KV_CACHE_TAIL)
ODD_OUT_WIDTHS = ((N_Q,), (N_KV,), KV_CACHE_TAIL)


def _ssd_kernel(z_ref, xbc_ref, gd_ref, dtt_ref, conv0_ref, h0_ref, cw_ref, cb_ref, dtb_ref, dtbt_ref,
                alog_ref, alogt_ref, dskip_ref, ng_ref, y_ref, hlast_ref, h_scr, xw_scr):
    f32, bf16 = jnp.float32, jnp.bfloat16
    hi = lax.Precision.HIGHEST
    Q = SSD_CHUNK
    c_idx = pl.program_id(1)
    tail = 8

    @pl.when(c_idx == 0)
    def _():
        h_scr[...] = h0_ref[0]
        xw_scr[0:tail, :] = conv0_ref[0]

    xw_scr[tail:tail + Q, :] = xbc_ref[...]
    conv = cb_ref[...]
    for k in range(SSM_CONV):
        start = tail - (SSM_CONV - 1) + k
        conv = conv + xw_scr[start:start + Q, :] * cw_ref[k:k + 1, :]
    xw_scr[0:tail, :] = xw_scr[Q:Q + tail, :]
    xc = conv * jax.nn.sigmoid(conv)
    xs = xc[:, :SSM_D_INNER]
    n_bc = SSM_GROUPS * SSM_STATE
    bm = xc[:, SSM_D_INNER:SSM_D_INNER + n_bc]
    cm = xc[:, SSM_D_INNER + n_bc:]

    dt = jax.nn.softplus(gd_ref[:, 3 * NSA_HEADS:] + dtb_ref[...])
    dtt = jax.nn.softplus(dtt_ref[0] + dtbt_ref[...])
    da = dt * (-jnp.exp(alog_ref[...]))
    dat = dtt * (-jnp.exp(alogt_ref[...]))
    ri = lax.broadcasted_iota(jnp.int32, (Q, Q), 0)
    ci = lax.broadcasted_iota(jnp.int32, (Q, Q), 1)
    causal = ci <= ri
    acum = jnp.dot(jnp.where(causal, 1.0, 0.0), da, precision=hi, preferred_element_type=f32)
    acumt = jnp.dot(dat, jnp.where(ri <= ci, 1.0, 0.0), precision=hi, preferred_element_type=f32)
    tot_t = acumt[:, Q - 1:Q]
    hh = lax.broadcasted_iota(jnp.int32, (SSM_HEADS, SSM_D_INNER), 0)
    ch = lax.broadcasted_iota(jnp.int32, (SSM_HEADS, SSM_D_INNER), 1) // SSM_HEAD_DIM
    expand = jnp.where(hh == ch, 1.0, 0.0)
    xdt = xs * jnp.dot(dt, expand, precision=hi, preferred_element_type=f32)
    xst = xs.T
    rep = SSM_HEADS // SSM_GROUPS
    ys = []
    for g in range(SSM_GROUPS):
        b_g = bm[:, g * SSM_STATE:(g + 1) * SSM_STATE]
        c_g = cm[:, g * SSM_STATE:(g + 1) * SSM_STATE]
        b_bf, c_bf = b_g.astype(bf16), c_g.astype(bf16)
        cb = lax.dot_general(c_bf, b_bf, (((1,), (1,)), ((), ())), preferred_element_type=f32)
        for h in range(g * rep, (g + 1) * rep):
            p0 = h * SSM_HEAD_DIM
            acol = acum[:, h:h + 1]
            arow = acumt[h:h + 1, :]
            decay = jnp.exp(jnp.where(causal, acol - arow, NEG_INF))
            y_h = jnp.dot((cb * decay).astype(bf16), xdt[:, p0:p0 + SSM_HEAD_DIM].astype(bf16),
                          preferred_element_type=f32)
            h_old = h_scr[h]
            y_h = y_h + jnp.exp(acol) * lax.dot_general(c_bf, h_old.astype(bf16), (((1,), (1,)), ((), ())),
                                                        preferred_element_type=f32)
            ys.append(y_h)
            w_row = dtt[h:h + 1, :] * jnp.exp(tot_t[h:h + 1, :] - arow)
            xdt_t = (xst[p0:p0 + SSM_HEAD_DIM, :] * w_row).astype(bf16)
            h_scr[h] = h_old * jnp.exp(tot_t[h:h + 1, :]) + jnp.dot(xdt_t, b_bf, preferred_element_type=f32)
    y = jnp.concatenate(ys, axis=1) + dskip_ref[...] * xs
    zz = z_ref[...]
    y = y * (zz * jax.nn.sigmoid(zz))
    gw = SSM_D_INNER // SSM_GROUPS
    outs = []
    for g in range(SSM_GROUPS):
        yg = y[:, g * gw:(g + 1) * gw]
        outs.append(yg * lax.rsqrt(jnp.mean(yg * yg, axis=-1, keepdims=True) + NORM_EPS))
    y_ref[...] = jnp.concatenate(outs, axis=1) * ng_ref[...]

    @pl.when(c_idx == pl.num_programs(1) - 1)
    def _():
        hlast_ref[0] = h_scr[...]


def ssd_mixer(z, xbc, gd, conv_state, ssm_state, conv_w, conv_b, dt_bias, a_log, d_skip, norm_g, batch):
    m = z.shape[0]
    t = m // batch
    assert t % SSD_CHUNK == 0
    nc = t // SSD_CHUNK
    f32 = jnp.float32
    dtt = jnp.transpose(gd[:, 3 * NSA_HEADS:].reshape(batch, t, SSM_HEADS), (0, 2, 1))
    conv0 = jnp.pad(conv_state.astype(f32), ((0, 0), (8 - (SSM_CONV - 1), 0), (0, 0)))
    row = lambda b, c: (b * nc + c, 0)
    fixed2 = lambda b, c: (0, 0)
    vec = lambda a: a.reshape(1, -1).astype(f32)
    col = lambda a: a.reshape(-1, 1).astype(f32)
    y, h_last = pl.pallas_call(
        _ssd_kernel,
        grid=(batch, nc),
        in_specs=[pl.BlockSpec((SSD_CHUNK, SSM_D_INNER), row),
                  pl.BlockSpec((SSD_CHUNK, SSM_CONV_DIM), row),
                  pl.BlockSpec((SSD_CHUNK, N_GD), row),
                  pl.BlockSpec((1, SSM_HEADS, SSD_CHUNK), lambda b, c: (b, 0, c)),
                  pl.BlockSpec((1, 8, SSM_CONV_DIM), lambda b, c: (b, 0, 0)),
                  pl.BlockSpec((1, SSM_HEADS, SSM_HEAD_DIM, SSM_STATE), lambda b, c: (b, 0, 0, 0)),
                  pl.BlockSpec((SSM_CONV, SSM_CONV_DIM), fixed2),
                  pl.BlockSpec((1, SSM_CONV_DIM), fixed2),
                  pl.BlockSpec((1, SSM_HEADS), fixed2), pl.BlockSpec((SSM_HEADS, 1), fixed2),
                  pl.BlockSpec((1, SSM_HEADS), fixed2), pl.BlockSpec((SSM_HEADS, 1), fixed2),
                  pl.BlockSpec((1, SSM_D_INNER), fixed2), pl.BlockSpec((1, SSM_D_INNER), fixed2)],
        out_specs=[pl.BlockSpec((SSD_CHUNK, SSM_D_INNER), row),
                   pl.BlockSpec((1, SSM_HEADS, SSM_HEAD_DIM, SSM_STATE), lambda b, c: (b, 0, 0, 0))],
        out_shape=[jax.ShapeDtypeStruct((m, SSM_D_INNER), f32),
                   jax.ShapeDtypeStruct((batch, SSM_HEADS, SSM_HEAD_DIM, SSM_STATE), f32)],
        scratch_shapes=[pltpu.VMEM((SSM_HEADS, SSM_HEAD_DIM, SSM_STATE), f32),
                        pltpu.VMEM((SSD_CHUNK + 8, SSM_CONV_DIM), f32)],
        compiler_params=pltpu.CompilerParams(dimension_semantics=("arbitrary", "arbitrary"),
                                             vmem_limit_bytes=VMEM_LIMIT_BYTES),
        name="ssd_mixer",
    )(z, xbc, gd, dtt, conv0, ssm_state.astype(f32), conv_w.astype(f32), vec(conv_b),
      vec(dt_bias), col(dt_bias), vec(a_log), col(a_log),
      vec(jnp.repeat(d_skip, SSM_HEAD_DIM)), vec(norm_g))
    return y, h_last


def _nsa_cmp_kernel(q_ref, x_ref, pe_ref, w1_ref, b1_ref, w2_ref, b2_ref, ov_ref, o_ref, sel_ref, cmp_scr):
    f32, bf16 = jnp.float32, jnp.bfloat16
    i = pl.program_id(1)
    tq = q_ref.shape[1]
    n_seg = x_ref.shape[2]
    n_sel = ov_ref.shape[1]
    n_cg = 2 * NSA_KV_HEADS

    @pl.when(i == 0)
    def _():
        for cg in range(n_cg):
            c = cg // NSA_KV_HEADS
            x = x_ref[0, cg]
            a0 = jnp.dot((x + pe_ref[c, 0]).astype(bf16), w1_ref[c, 0], preferred_element_type=f32)
            a1 = jnp.dot((x + pe_ref[c, 1]).astype(bf16), w1_ref[c, 1], preferred_element_type=f32)
            hid = jax.nn.gelu(b1_ref[c] + a0 + pltpu.roll(a1, n_seg - 1, axis=0))
            cmp_scr[cg] = jnp.dot(hid.astype(bf16), w2_ref[c], preferred_element_type=f32) + b2_ref[c]

    q = q_ref[0]
    tpos = i * tq + lax.broadcasted_iota(jnp.int32, (tq, 1), 0)
    n_io = lax.broadcasted_iota(jnp.int32, (1, n_seg), 1)
    cmp_ok = (n_io * NSA_CMP_STRIDE + NSA_CMP_BLOCK - 1 <= tpos) & (n_io < n_seg - 1)
    sidx = lax.broadcasted_iota(jnp.int32, (1, n_sel), 1)
    blk_t = tpos // NSA_SEL_BLOCK
    valid = sidx <= blk_t
    forced = valid & ((sidx == 0) | (sidx > blk_t - NSA_LOCAL_BLOCKS))
    eye = jnp.where(lax.broadcasted_iota(jnp.int32, (n_sel, n_sel), 0)
                    == lax.broadcasted_iota(jnp.int32, (n_sel, n_sel), 1), 1.0, 0.0)
    outs = []
    for g in range(NSA_KV_HEADS):
        kc = cmp_scr[g].astype(bf16)
        vc = cmp_scr[NSA_KV_HEADS + g].astype(bf16)
        psum = jnp.zeros((tq, n_seg), f32)
        for j in range(NSA_GQ):
            h0 = (g * NSA_GQ + j) * HEAD_DIM
            hq = (q[:, h0:h0 + HEAD_DIM] * ATTN_SCALE).astype(bf16)
            s = lax.dot_general(hq, kc, (((1,), (1,)), ((), ())), preferred_element_type=f32)
            s = jnp.where(cmp_ok, s, NEG_INF)
            p = jnp.where(cmp_ok, jnp.exp(s - jnp.max(s, axis=-1, keepdims=True)), 0.0)
            p = p / jnp.maximum(jnp.sum(p, axis=-1, keepdims=True), 1e-30)
            outs.append(jnp.dot(p.astype(bf16), vc, preferred_element_type=f32))
            psum = psum + p
        imp = jnp.dot(psum, ov_ref[...], precision=lax.Precision.HIGHEST, preferred_element_type=f32)
        imp = jnp.where(forced, NSA_FORCE, imp)
        imp = jnp.where(valid, imp, NEG_INF)
        sel = jnp.zeros((tq, n_sel), f32)
        for _ in range(min(NSA_TOPN, n_sel)):
            mx = jnp.max(imp, axis=-1, keepdims=True)
            first = jnp.min(jnp.where(imp == mx, sidx, n_sel), axis=-1, keepdims=True)
            hit = sidx == first
            sel = jnp.where(hit & (mx > 0.5 * NEG_INF), 1.0, sel)
            imp = jnp.where(hit, REMOVED, imp)
        sel_ref[0, g] = lax.dot_general(eye, sel, (((1,), (1,)), ((), ())), preferred_element_type=f32)
    o_ref[0] = jnp.concatenate(outs, axis=1)


def nsa_compressed_branch(q, kvc, batch, cmp_pos, w1, b1, w2, b2):
    f32, bf16 = jnp.float32, jnp.bfloat16
    t = q.shape[0] // batch
    tq = ATTN_TILE
    S = NSA_CMP_STRIDE
    r = NSA_CMP_BLOCK // S
    assert t % tq == 0 and r == 2
    n_seg, n_sel = t // S, t // NSA_SEL_BLOCK
    n_cg = 2 * NSA_KV_HEADS
    x = jnp.transpose(kvc.reshape(batch, n_seg, S, n_cg, HEAD_DIM), (0, 3, 1, 2, 4)).reshape(batch, n_cg, n_seg, S * HEAD_DIM)
    pe = cmp_pos.reshape(2, r, 1, S * HEAD_DIM).astype(f32)
    w1r = w1.reshape(2, r, S * HEAD_DIM, -1).astype(bf16)
    hid = w1r.shape[-1]
    cmp_start = S * jnp.arange(n_seg, dtype=jnp.int32)
    sel_start = NSA_SEL_BLOCK * jnp.arange(n_sel, dtype=jnp.int32)
    overlap = ((cmp_start[:, None] < sel_start[None, :] + NSA_SEL_BLOCK)
               & (cmp_start[:, None] + NSA_CMP_BLOCK > sel_start[None, :])).astype(f32)
    full = lambda a: pl.BlockSpec(a.shape, lambda b, i: (0,) * a.ndim)
    b1r, b2r, w2r = b1.reshape(2, 1, hid).astype(f32), b2.reshape(2, 1, HEAD_DIM).astype(f32), w2.astype(bf16)
    o_cmp, sel = pl.pallas_call(
        _nsa_cmp_kernel,
        grid=(batch, t // tq),
        in_specs=[pl.BlockSpec((1, tq, N_Q), lambda b, i: (b, i, 0)),
                  pl.BlockSpec((1, n_cg, n_seg, S * HEAD_DIM), lambda b, i: (b, 0, 0, 0)),
                  full(pe), full(w1r), full(b1r), full(w2r), full(b2r), full(overlap)],
        out_specs=[pl.BlockSpec((1, tq, N_Q), lambda b, i: (b, i, 0)),
                   pl.BlockSpec((1, NSA_KV_HEADS, n_sel, tq), lambda b, i: (b, 0, 0, i))],
        out_shape=[jax.ShapeDtypeStruct((batch, t, N_Q), f32),
                   jax.ShapeDtypeStruct((batch, NSA_KV_HEADS, n_sel, t), f32)],
        scratch_shapes=[pltpu.VMEM((n_cg, n_seg, HEAD_DIM), f32)],
        compiler_params=pltpu.CompilerParams(dimension_semantics=("arbitrary", "arbitrary"),
                                             vmem_limit_bytes=VMEM_LIMIT_BYTES),
        name="nsa_cmp",
    )(q.reshape(batch, t, N_Q), x, pe, w1r, b1r, w2r, b2r, overlap)
    return o_cmp, sel


ATTN_TILE = 256


def _flash_t_kernel(*refs, mode, n_kblocks):
    f32, bf16 = jnp.float32, jnp.bfloat16
    if mode == 'sel':
        q_ref, k_ref, v_ref, sel_ref, o_ref, qt_scr, m_scr, l_scr, acc_scr = refs
    elif mode == 'moba':
        q_ref, k_ref, v_ref, o_ref, qt_scr, m_scr, l_scr, acc_scr, selm_scr = refs
    else:
        q_ref, k_ref, v_ref, o_ref, qt_scr, m_scr, l_scr, acc_scr = refs
    tq = tk = ATTN_TILE
    D = HEAD_DIM
    i = pl.program_id(2)
    nh = q_ref.shape[2] // D
    gq = nh // 2
    qt = q_ref[0].T
    qt_scr[...] = (qt * ATTN_SCALE).astype(bf16)
    m_scr[...] = jnp.full(m_scr.shape, NEG_INF, f32)
    l_scr[...] = jnp.zeros(l_scr.shape, f32)
    acc_scr[...] = jnp.zeros(acc_scr.shape, f32)
    tpos = i * tq + lax.broadcasted_iota(jnp.int32, (1, tq), 1)
    krow = lax.broadcasted_iota(jnp.int32, (tk, 1), 0)
    brow = lax.broadcasted_iota(jnp.int32, (n_kblocks, tq), 0)

    if mode == 'moba':
        kmean = jnp.mean(k_ref[0].reshape(n_kblocks, tk, 2 * D), axis=1)
        valid = brow < i
        for jj in range(nh):
            gg = jj // gq
            gate = jnp.dot(kmean[:, gg * D:(gg + 1) * D], qt[jj * D:(jj + 1) * D, :],
                           precision=lax.Precision.HIGHEST, preferred_element_type=f32)
            gate = jnp.where(valid, gate, NEG_INF)
            sel = jnp.zeros((n_kblocks, tq), f32)
            for _ in range(min(MOBA_TOPK, n_kblocks)):
                mx = jnp.max(gate, axis=0, keepdims=True)
                first = jnp.min(jnp.where(gate == mx, brow, n_kblocks), axis=0, keepdims=True)
                hit = brow == first
                sel = jnp.where(hit, 1.0, sel)
                gate = jnp.where(hit, REMOVED, gate)
            selm_scr[jj] = jnp.where(valid, sel, 0.0)

    def body(n, carry):
        off = pl.multiple_of(n * tk, tk)
        kblk = k_ref[0, pl.ds(off, tk), :]
        vblk_t = v_ref[0, pl.ds(off, tk), :].T
        kpos = n * tk + krow
        base = kpos <= tpos
        if mode == 'win':
            base = base & (tpos - kpos < NSA_WINDOW)
        for gg in range(2):
            kb = kblk[:, gg * D:(gg + 1) * D].astype(bf16)
            vt = vblk_t[gg * D:(gg + 1) * D, :].astype(bf16)
            mask_g = base
            if mode == 'sel':
                st = sel_ref[0, gg]
                per = tk // NSA_SEL_BLOCK
                srow = lax.broadcasted_iota(jnp.int32, (st.shape[0], 1), 0)
                pieces = []
                for r in range(per):
                    row = jnp.sum(jnp.where(srow == n * per + r, st, 0.0), axis=0, keepdims=True)
                    pieces.append(jnp.broadcast_to(row, (NSA_SEL_BLOCK, tq)))
                mask_g = base & (jnp.concatenate(pieces, axis=0) > 0.0)
            for j in range(gq):
                jj = gg * gq + j
                mask = mask_g
                if mode == 'moba':
                    row = jnp.sum(jnp.where(brow == n, selm_scr[jj], 0.0), axis=0, keepdims=True)
                    own = jnp.where(n == i, 1.0, 0.0)
                    mask = base & ((row + own) > 0.0)
                s = jnp.dot(kb, qt_scr[jj * D:(jj + 1) * D, :], preferred_element_type=f32)
                s = jnp.where(mask, s, NEG_INF)
                m_old = m_scr[jj:jj + 1, :]
                m_new = jnp.maximum(m_old, jnp.max(s, axis=0, keepdims=True))
                p = jnp.where(mask, jnp.exp(s - m_new), 0.0)
                alpha = jnp.exp(m_old - m_new)
                l_scr[jj:jj + 1, :] = alpha * l_scr[jj:jj + 1, :] + jnp.sum(p, axis=0, keepdims=True)
                acc_scr[jj * D:(jj + 1) * D, :] = (alpha * acc_scr[jj * D:(jj + 1) * D, :]
                                                   + jnp.dot(vt, p.astype(bf16), preferred_element_type=f32))
                m_scr[jj:jj + 1, :] = m_new
        return carry

    if mode == 'win':
        lo = jnp.maximum(i - (NSA_WINDOW + tk - 1) // tk, 0)
    else:
        lo = 0
    lax.fori_loop(lo, i + 1, body, 0)
    outs = [acc_scr[jj * D:(jj + 1) * D, :] / jnp.maximum(l_scr[jj:jj + 1, :], 1e-30) for jj in range(nh)]
    o_ref[0] = jnp.concatenate(outs, axis=0).T


def flash_attention_t(q, kv, mode, batch, sel_t=None):
    f32 = jnp.float32
    t = q.shape[0] // batch
    tq = ATTN_TILE
    assert t % tq == 0
    n_pairs = NSA_KV_HEADS // 2
    pw = 2 * NSA_GQ * HEAD_DIM
    q3 = q.reshape(batch, t, N_Q)
    kv3 = kv.reshape(batch, t, N_KV)
    args = [q3, kv3, kv3]
    in_specs = [pl.BlockSpec((1, tq, pw), lambda b, g, i: (b, i, g)),
                pl.BlockSpec((1, t, 2 * HEAD_DIM), lambda b, g, i: (b, 0, g)),
                pl.BlockSpec((1, t, 2 * HEAD_DIM), lambda b, g, i: (b, 0, n_pairs + g))]
    scratch = [pltpu.VMEM((pw, tq), jnp.bfloat16), pltpu.VMEM((pw // HEAD_DIM, tq), f32),
               pltpu.VMEM((pw // HEAD_DIM, tq), f32), pltpu.VMEM((pw, tq), f32)]
    if mode == 'sel':
        args.append(sel_t)
        in_specs.append(pl.BlockSpec((1, 2, sel_t.shape[2], tq), lambda b, g, i: (b, g, 0, i)))
    if mode == 'moba':
        scratch.append(pltpu.VMEM((pw // HEAD_DIM, t // tq, tq), f32))
    return pl.pallas_call(
        functools.partial(_flash_t_kernel, mode=mode, n_kblocks=t // tq),
        grid=(batch, n_pairs, t // tq),
        in_specs=in_specs,
        out_specs=pl.BlockSpec((1, tq, pw), lambda b, g, i: (b, i, g)),
        out_shape=jax.ShapeDtypeStruct((batch, t, N_Q), f32),
        scratch_shapes=scratch,
        compiler_params=pltpu.CompilerParams(dimension_semantics=("arbitrary", "arbitrary", "arbitrary"),
                                             vmem_limit_bytes=VMEM_LIMIT_BYTES),
        name="flash_" + mode,
    )(*args)


PEER_N_EXPERTS = PEER_N_KEYS * PEER_N_KEYS
PEER_SCORE_TILE = 256
PEER_TOKEN_TILE = 512
PEER_EXPERT_TILE = 1024
REMOVED = -3e38


def _top_desc(s, k):
    outs = []
    for r in range(k):
        m = jnp.max(s, axis=0, keepdims=True)
        outs.append(m)
        if r + 1 < k:
            s = jnp.where(s == m, REMOVED, s)
    return jnp.concatenate(outs, axis=0)


def _peer_score_kernel(x_ref, g_ref, wq_ref, sk_ref, xt_ref, s1_ref, s2_ref, e1_ref, e2_ref, tau_ref):
    f32 = jnp.float32
    x = x_ref[...]
    xn = x * lax.rsqrt(jnp.mean(x * x, axis=-1, keepdims=True) + NORM_EPS) * g_ref[...]
    xt_ref[...] = xn.T.astype(jnp.bfloat16)
    q = jnp.dot(xn.astype(jnp.bfloat16), wq_ref[...], preferred_element_type=f32)
    half = PEER_KEY_DIM // 2
    taus = []
    for h in range(PEER_HEADS):
        st = []
        for c in range(2):
            qhc = q[:, (2 * h + c) * half:(2 * h + c + 1) * half]
            st.append(lax.dot_general(sk_ref[c], qhc, (((1,), (1,)), ((), ())),
                                      precision=lax.Precision.HIGHEST, preferred_element_type=f32))
        t1 = _top_desc(st[0], PEER_TOPK)
        t2 = _top_desc(st[1], PEER_TOPK)
        cand = jnp.concatenate([t1[i:i + 1] + t2[:PEER_TOPK // (i + 1)] for i in range(PEER_TOPK)], axis=0)
        tops = _top_desc(cand, PEER_TOPK)
        z = jnp.sum(jnp.exp(tops - tops[0:1]), axis=0, keepdims=True)
        taus.append(tops[PEER_TOPK - 1:PEER_TOPK])
        s1_ref[h] = st[0]
        s2_ref[h] = st[1]
        e1_ref[h] = jnp.exp(st[0] - t1[0:1]) / z
        e2_ref[h] = jnp.exp(st[1] - t2[0:1])
    tau_ref[...] = jnp.concatenate(taus, axis=0)


def _peer_dense_kernel(xres_ref, xt_ref, s1_ref, s2_ref, e1_ref, e2_ref, tau_ref, u_ref, vt_ref, o_ref,
                       act_scr, w_scr, yt_scr):
    f32 = jnp.float32
    j = pl.program_id(1)
    tm = xt_ref.shape[1]
    n_a = PEER_EXPERT_TILE // PEER_N_KEYS

    @pl.when(j == 0)
    def _():
        yt_scr[...] = jnp.zeros(yt_scr.shape, f32)

    act_scr[...] = jnp.dot(u_ref[...], xt_ref[...], preferred_element_type=f32)
    a_rows = pl.ds(pl.multiple_of(j * n_a, n_a), n_a)
    for aa in range(n_a):
        for tc in range(tm // LANES):
            lanes = slice(tc * LANES, (tc + 1) * LANES)
            acc = jnp.zeros((PEER_N_KEYS, LANES), f32)
            for h in range(PEER_HEADS):
                val = s2_ref[h, :, lanes] + s1_ref[h, a_rows, lanes][aa:aa + 1]
                gate = e2_ref[h, :, lanes] * e1_ref[h, a_rows, lanes][aa:aa + 1]
                acc = acc + jnp.where(val >= tau_ref[h:h + 1, lanes], gate, 0.0)
            act = act_scr[aa * PEER_N_KEYS:(aa + 1) * PEER_N_KEYS, lanes]
            w_scr[aa * PEER_N_KEYS:(aa + 1) * PEER_N_KEYS, lanes] = (acc * jax.nn.gelu(act)).astype(jnp.bfloat16)
    yt_scr[...] += jnp.dot(vt_ref[...], w_scr[...], preferred_element_type=f32)

    @pl.when(j == pl.num_programs(1) - 1)
    def _():
        o_ref[...] = xres_ref[...] + yt_scr[...].T


def peer_ffn_dense(x_res, g_norm, wq_bf16, subkeys, u_bf16, vt_bf16):
    n, d = x_res.shape
    tm = PEER_TOKEN_TILE if n % PEER_TOKEN_TILE == 0 else LANES
    ts = PEER_SCORE_TILE if n % PEER_SCORE_TILE == 0 else LANES
    n_pad = -(-n // tm) * tm
    xp = jnp.pad(x_res, ((0, n_pad - n), (0, 0)))
    hk = (PEER_HEADS, PEER_N_KEYS, n_pad)
    stat_spec = lambda t: pl.BlockSpec((PEER_HEADS, PEER_N_KEYS, t), lambda i, *_: (0, 0, i))
    xt, s1, s2, e1, e2, tau = pl.pallas_call(
        _peer_score_kernel,
        grid=(n_pad // ts,),
        in_specs=[pl.BlockSpec((ts, d), lambda i: (i, 0)),
                  pl.BlockSpec((1, d), lambda i: (0, 0)),
                  pl.BlockSpec(wq_bf16.shape, lambda i: (0, 0)),
                  pl.BlockSpec(subkeys.shape, lambda i: (0, 0, 0))],
        out_specs=[pl.BlockSpec((d, ts), lambda i: (0, i)), stat_spec(ts), stat_spec(ts), stat_spec(ts), stat_spec(ts),
                   pl.BlockSpec((PEER_HEADS, ts), lambda i: (0, i))],
        out_shape=[jax.ShapeDtypeStruct((d, n_pad), jnp.bfloat16)] + [jax.ShapeDtypeStruct(hk, jnp.float32)] * 4
                  + [jax.ShapeDtypeStruct((PEER_HEADS, n_pad), jnp.float32)],
        compiler_params=pltpu.CompilerParams(dimension_semantics=("arbitrary",),
                                             vmem_limit_bytes=VMEM_LIMIT_BYTES),
        name="peer_score",
    )(xp, g_norm.reshape(1, d).astype(jnp.float32), wq_bf16, subkeys.astype(jnp.float32))
    te = PEER_EXPERT_TILE
    out = pl.pallas_call(
        _peer_dense_kernel,
        grid=(n_pad // tm, PEER_N_EXPERTS // te),
        in_specs=[pl.BlockSpec((tm, d), lambda i, j: (i, 0)),
                  pl.BlockSpec((d, tm), lambda i, j: (0, i)),
                  stat_spec(tm), stat_spec(tm), stat_spec(tm), stat_spec(tm),
                  pl.BlockSpec((PEER_HEADS, tm), lambda i, j: (0, i)),
                  pl.BlockSpec((te, d), lambda i, j: (j, 0)),
                  pl.BlockSpec((d, te), lambda i, j: (0, j))],
        out_specs=pl.BlockSpec((tm, d), lambda i, j: (i, 0)),
        out_shape=jax.ShapeDtypeStruct((n_pad, d), jnp.float32),
        scratch_shapes=[pltpu.VMEM((te, tm), jnp.float32), pltpu.VMEM((te, tm), jnp.bfloat16),
                        pltpu.VMEM((d, tm), jnp.float32)],
        compiler_params=pltpu.CompilerParams(dimension_semantics=("arbitrary", "arbitrary"),
                                             vmem_limit_bytes=VMEM_LIMIT_BYTES),
        name="peer_dense",
    )(xp, xt, s1, s2, e1, e2, tau, u_bf16, vt_bf16)
    return out[:n]


def rmsnorm(x, g):
    xf = x.astype(jnp.float32)
    y = xf * lax.rsqrt(jnp.mean(xf * xf, axis=-1, keepdims=True) + NORM_EPS)
    return (y * g.astype(jnp.float32)).astype(x.dtype)


def split_cols(a, widths):
    outs, off = [], 0
    for w in widths:
        outs.append(a[..., off:off + w])
        off += w
    return outs


def pad_axis1(a, n):
    return jnp.pad(a, [(0, 0), (0, n - a.shape[1])] + [(0, 0)] * (a.ndim - 2))


def qblocks(T, qmax):
    qb = min(qmax, T)
    nb = -(-T // qb)
    return qb, nb, nb * qb


def run_blocks(fn, nb, qb, T):
    out = lax.map(fn, jnp.arange(nb))
    out = jnp.moveaxis(out, 0, 1)
    return out.reshape(out.shape[:1] + (nb * qb,) + out.shape[3:])[:, :T]


def masked_softmax(s, mask):
    s = jnp.where(mask, s.astype(jnp.float32), NEG_INF)
    m = jnp.max(s, axis=-1, keepdims=True)
    p = jnp.exp(s - m) * mask
    return p / jnp.maximum(jnp.sum(p, axis=-1, keepdims=True), 1e-30)


def partial_rope(x, pos):
    half = ROT_DIM // 2
    inv = ROPE_THETA ** (-jnp.arange(0, ROT_DIM, 2, dtype=jnp.float32) / ROT_DIM)
    ang = pos.astype(jnp.float32)[:, None] * inv[None, :]
    cos = jnp.cos(ang)[:, None, :]
    sin = jnp.sin(ang)[:, None, :]
    x1 = x[..., :half].astype(jnp.float32)
    x2 = x[..., half:ROT_DIM].astype(jnp.float32)
    rot = jnp.concatenate([x1 * cos - x2 * sin, x2 * cos + x1 * sin], axis=-1).astype(x.dtype)
    return jnp.concatenate([rot, x[..., ROT_DIM:]], axis=-1)


def rope_keys(kv, pos):
    return jnp.stack([partial_rope(kv[:, :, 0], pos), kv[:, :, 1]], axis=2)


def gather_pages(pool, page_table):
    g = pool[page_table]
    return g.reshape((g.shape[0], g.shape[1] * g.shape[2]) + g.shape[3:])


def ssd_chunked(x, dt, a, b_h, c_h, h0):
    B, L, H, P = x.shape
    N = b_h.shape[-1]
    f32 = jnp.float32
    Q = min(SSD_CHUNK, L)
    nc = -(-L // Q)
    Lp = nc * Q
    xdt = pad_axis1(x.astype(f32) * dt[..., None], Lp).reshape(B, nc, Q, H, P)
    da = pad_axis1(dt * a, Lp).reshape(B, nc, Q, H)
    bc = pad_axis1(b_h.astype(f32), Lp).reshape(B, nc, Q, H, N)
    cc = pad_axis1(c_h.astype(f32), Lp).reshape(B, nc, Q, H, N)
    acum = jnp.cumsum(da, axis=2)
    causal = jnp.tril(jnp.ones((Q, Q), bool))
    seg = acum[:, :, :, None, :] - acum[:, :, None, :, :]
    decay_in = jnp.exp(jnp.where(causal[None, None, :, :, None], seg, NEG_INF))
    scores = jnp.einsum('bclhn,bcshn->bclsh', cc, bc) * decay_in
    y_diag = jnp.einsum('bclsh,bcshp->bclhp', scores, xdt)
    decay_out = jnp.exp(acum[:, :, -1:] - acum)
    chunk_states = jnp.einsum('bcshn,bcshp->bchpn', bc * decay_out[..., None], xdt)
    chunk_decay = jnp.exp(acum[:, :, -1])

    def step(h, inp):
        st, dec = inp
        return h * dec[:, :, None, None] + st, h

    h_last, h_enter = lax.scan(step, h0.astype(f32),
                               (jnp.moveaxis(chunk_states, 1, 0), jnp.moveaxis(chunk_decay, 1, 0)))
    h_enter = jnp.moveaxis(h_enter, 0, 1)
    y_off = jnp.einsum('bclhn,bchpn->bclhp', cc * jnp.exp(acum)[..., None], h_enter)
    y = (y_diag + y_off).reshape(B, Lp, H, P)[:, :L]
    return y, h_last


def mamba2_ssd(z, xbc, dt_raw, conv_state, ssm_state, conv_w, conv_b, dt_bias, a_log, d_skip, norm_g):
    B, T, _ = xbc.shape
    f32 = jnp.float32
    xpad = jnp.concatenate([conv_state.astype(xbc.dtype), xbc], axis=1)
    new_conv = xpad[:, T:]
    conv = conv_b
    for k in range(SSM_CONV):
        conv = conv + xpad[:, k:k + T] * conv_w[k]
    xbc_c = jax.nn.silu(conv)
    n_bc = SSM_GROUPS * SSM_STATE
    rep = SSM_HEADS // SSM_GROUPS
    xs = xbc_c[..., :SSM_D_INNER].reshape(B, T, SSM_HEADS, SSM_HEAD_DIM)
    b_h = jnp.repeat(xbc_c[..., SSM_D_INNER:SSM_D_INNER + n_bc].reshape(B, T, SSM_GROUPS, SSM_STATE), rep, axis=2)
    c_h = jnp.repeat(xbc_c[..., SSM_D_INNER + n_bc:].reshape(B, T, SSM_GROUPS, SSM_STATE), rep, axis=2)
    dt = jax.nn.softplus(dt_raw.astype(f32) + dt_bias.astype(f32))
    a = -jnp.exp(a_log.astype(f32))
    y, h_last = ssd_chunked(xs, dt, a, b_h, c_h, ssm_state)
    y = y + d_skip.astype(f32)[:, None] * xs.astype(f32)
    y = y.reshape(B, T, SSM_D_INNER) * jax.nn.silu(z.astype(f32))
    yg = y.reshape(B, T, SSM_GROUPS, SSM_D_INNER // SSM_GROUPS)
    yg = yg * lax.rsqrt(jnp.mean(yg * yg, axis=-1, keepdims=True) + NORM_EPS)
    y = yg.reshape(B, T, SSM_D_INNER) * norm_g.astype(f32)
    return y.astype(xbc.dtype), new_conv, h_last.astype(ssm_state.dtype)


def nsa_compress(kv_all, pos_emb, w1, b1, w2, b2):
    B, L = kv_all.shape[:2]
    S = NSA_CMP_STRIDE
    r = NSA_CMP_BLOCK // S
    n_seg = L // S
    n_cmp = n_seg - r + 1
    seg = kv_all[:, :n_seg * S].reshape(B, n_seg, S, 2, NSA_KV_HEADS, HEAD_DIM)
    h = b1[None, None, :, None, :]
    for j in range(r):
        pe = jnp.transpose(pos_emb[:, j * S:(j + 1) * S], (1, 0, 2))[:, :, None, :]
        h = h + jnp.einsum('bnlcgd,cldh->bncgh', seg + pe, w1[:, j * S:(j + 1) * S])[:, j:j + n_cmp]
    h = jax.nn.gelu(h)
    return jnp.einsum('bncgh,chd->bncgd', h, w2) + b2[None, None, :, None, :]


def nsa_selected(qr, kvs_all, sel_idx, sel_ok, pos0):
    B, T = qr.shape[:2]
    Lk = kvs_all.shape[1]
    n_sel = -(-Lk // NSA_SEL_BLOCK)
    topn = sel_idx.shape[-1]
    kvb = pad_axis1(kvs_all, n_sel * NSA_SEL_BLOCK).reshape(B, n_sel, NSA_SEL_BLOCK, 2, NSA_KV_HEADS, HEAD_DIM)
    qb, nb, Tp = qblocks(T, SEL_QBLOCK)
    qp, ip, okp = pad_axis1(qr, Tp), pad_axis1(sel_idx, Tp), pad_axis1(sel_ok, Tp)
    bi = jnp.arange(B)[:, None, None, None]
    n_keys = topn * NSA_SEL_BLOCK

    def blk(i):
        start = i * qb
        q_b = lax.dynamic_slice_in_dim(qp, start, qb, 1)
        i_b = lax.dynamic_slice_in_dim(ip, start, qb, 1)
        ok_b = lax.dynamic_slice_in_dim(okp, start, qb, 1)
        qpos = pos0 + start + jnp.arange(qb)
        g_all = kvb[bi, i_b]
        g = jnp.stack([g_all[:, :, gg, :, :, :, gg, :] for gg in range(NSA_KV_HEADS)], axis=2)
        kpos = i_b[..., None] * NSA_SEL_BLOCK + jnp.arange(NSA_SEL_BLOCK)
        mask = (ok_b[..., None] & (kpos <= qpos[None, :, None, None, None])).reshape(B, qb, NSA_KV_HEADS, 1, n_keys)
        kk = g[..., 0, :].reshape(B, qb, NSA_KV_HEADS, n_keys, HEAD_DIM)
        vv = g[..., 1, :].reshape(B, qb, NSA_KV_HEADS, n_keys, HEAD_DIM)
        p = masked_softmax(jnp.einsum('bqgjd,bqgkd->bqgjk', q_b, kk) * ATTN_SCALE, mask)
        return jnp.einsum('bqgjk,bqgkd->bqgjd', p, vv)

    return run_blocks(blk, nb, qb, T)


def nsa_window(qr, kvw_ext, pos0):
    B, T = qr.shape[:2]
    Wb = kvw_ext.shape[1] - T
    W = NSA_WINDOW
    qb, nb, Tp = qblocks(T, WIN_QBLOCK)
    qp = pad_axis1(qr, Tp)
    kvp = jnp.pad(kvw_ext, [(0, 0), (W, Tp - T), (0, 0), (0, 0), (0, 0)])
    n_kp = W + Wb + Tp
    idx = jnp.arange(n_kp)
    kpos = (pos0 - Wb - W) + idx
    kval = (idx >= W) & (idx < W + Wb + T)

    def blk(i):
        start = i * qb
        q_b = lax.dynamic_slice_in_dim(qp, start, qb, 1)
        kv_b = lax.dynamic_slice_in_dim(kvp, start + Wb, W + qb, 1)
        kp_b = lax.dynamic_slice_in_dim(kpos, start + Wb, W + qb, 0)
        ok_b = lax.dynamic_slice_in_dim(kval, start + Wb, W + qb, 0)
        qpos = pos0 + start + jnp.arange(qb)
        mask = ok_b[None, :] & (kp_b[None, :] <= qpos[:, None]) & (qpos[:, None] - kp_b[None, :] < W)
        s = jnp.einsum('bqgjd,bkgd->bqgjk', q_b, kv_b[:, :, 0]) * ATTN_SCALE
        p = masked_softmax(s, mask[None, :, None, None, :])
        return jnp.einsum('bqgjk,bkgd->bqgjd', p, kv_b[:, :, 1])

    return run_blocks(blk, nb, qb, T)


def nsa_attention(q, qr, kvc_all, kvs_all, kvw_ext, pos0, cmp_pos, cmp_w1, cmp_b1, cmp_w2, cmp_b2):
    B, T = q.shape[:2]
    pos_q = pos0 + jnp.arange(T, dtype=jnp.int32)
    qg = q.reshape(B, T, NSA_KV_HEADS, NSA_GQ, HEAD_DIM)
    qr = qr.reshape(B, T, NSA_KV_HEADS, NSA_GQ, HEAD_DIM)
    cmp = nsa_compress(kvc_all, cmp_pos, cmp_w1, cmp_b1, cmp_w2, cmp_b2)
    n_cmp = cmp.shape[1]
    cmp_start = NSA_CMP_STRIDE * jnp.arange(n_cmp, dtype=jnp.int32)
    cmp_mask = (cmp_start + NSA_CMP_BLOCK - 1)[None, :] <= pos_q[:, None]
    s = jnp.einsum('btgjd,bngd->btgjn', qg, cmp[:, :, 0]) * ATTN_SCALE
    p_cmp = masked_softmax(s, cmp_mask[None, :, None, None, :])
    o_cmp = jnp.einsum('btgjn,bngd->btgjd', p_cmp, cmp[:, :, 1])
    n_sel = -(-kvs_all.shape[1] // NSA_SEL_BLOCK)
    sel_start = NSA_SEL_BLOCK * jnp.arange(n_sel, dtype=jnp.int32)
    overlap = ((cmp_start[:, None] < sel_start[None, :] + NSA_SEL_BLOCK)
               & (cmp_start[:, None] + NSA_CMP_BLOCK > sel_start[None, :])).astype(jnp.float32)
    imp = jnp.einsum('btgjn,ns->btgs', p_cmp, overlap)
    blk_t = pos_q // NSA_SEL_BLOCK
    sidx = jnp.arange(n_sel, dtype=jnp.int32)
    valid = sidx[None, :] <= blk_t[:, None]
    forced = valid & ((sidx[None, :] == 0) | (sidx[None, :] > blk_t[:, None] - NSA_LOCAL_BLOCKS))
    imp = jnp.where(forced[None, :, None, :], NSA_FORCE, imp)
    imp = jnp.where(valid[None, :, None, :], imp, NEG_INF)
    sel_score, sel_idx = lax.top_k(imp, min(NSA_TOPN, n_sel))
    o_sel = nsa_selected(qr, kvs_all, sel_idx, sel_score > 0.5 * NEG_INF, pos0)
    o_win = nsa_window(qr, kvw_ext, pos0)
    return o_cmp, o_sel, o_win


def moba_attention(qr, kv_all, pos0):
    B, T = qr.shape[:2]
    Lk = kv_all.shape[1]
    nblk = -(-Lk // MOBA_BLOCK)
    kvb = pad_axis1(kv_all, nblk * MOBA_BLOCK).reshape(B, nblk, MOBA_BLOCK, 2, MOBA_KV_HEADS, HEAD_DIM)
    kvb = jnp.transpose(kvb, (0, 4, 1, 2, 3, 5))
    kmean = jnp.mean(kvb[..., 0, :].astype(jnp.float32), axis=3)
    pos_q = pos0 + jnp.arange(T, dtype=jnp.int32)
    qg = qr.reshape(B, T, MOBA_KV_HEADS, MOBA_GQ, HEAD_DIM)
    gate = jnp.einsum('btgjd,bgnd->btgjn', qg.astype(jnp.float32), kmean)
    past_ok = jnp.arange(nblk)[None, :] < (pos_q // MOBA_BLOCK)[:, None]
    gate = jnp.where(past_ok[None, :, None, None, :], gate, NEG_INF)
    k = min(MOBA_TOPK, nblk)
    sc, idx = lax.top_k(gate, k)
    ok = sc > 0.5 * NEG_INF
    qb, nb, Tp = qblocks(T, MOBA_QBLOCK)
    qp, ip, okp = pad_axis1(qg, Tp), pad_axis1(idx, Tp), pad_axis1(ok, Tp)
    bi = jnp.arange(B)[:, None, None, None, None]
    gi = jnp.arange(MOBA_KV_HEADS)[None, None, :, None, None]
    bo = jnp.arange(B)[:, None, None]
    go = jnp.arange(MOBA_KV_HEADS)[None, None, :]
    n_sel = k * MOBA_BLOCK

    def blk(i):
        start = i * qb
        q_b = lax.dynamic_slice_in_dim(qp, start, qb, 1)
        i_b = lax.dynamic_slice_in_dim(ip, start, qb, 1)
        ok_b = lax.dynamic_slice_in_dim(okp, start, qb, 1)
        qpos = pos0 + start + jnp.arange(qb)
        own_blk = jnp.minimum(qpos // MOBA_BLOCK, nblk - 1)
        g_sel = kvb[bi, gi, i_b]
        g_own = kvb[bo, go, own_blk[None, :, None]]
        s_sel = jnp.einsum('bqgjd,bqgjkld->bqgjkl', q_b, g_sel[..., 0, :]).reshape(B, qb, MOBA_KV_HEADS, MOBA_GQ, n_sel)
        s_own = jnp.einsum('bqgjd,bqgld->bqgjl', q_b, g_own[..., 0, :])
        m_sel = jnp.broadcast_to(ok_b[..., None], ok_b.shape + (MOBA_BLOCK,)).reshape(B, qb, MOBA_KV_HEADS, MOBA_GQ, n_sel)
        own_pos = own_blk[:, None] * MOBA_BLOCK + jnp.arange(MOBA_BLOCK)
        m_own = jnp.broadcast_to((own_pos <= qpos[:, None])[None, :, None, None, :], (B, qb, MOBA_KV_HEADS, MOBA_GQ, MOBA_BLOCK))
        p = masked_softmax(jnp.concatenate([s_sel, s_own], axis=-1) * ATTN_SCALE,
                           jnp.concatenate([m_sel, m_own], axis=-1))
        v_sel = g_sel[..., 1, :].reshape(B, qb, MOBA_KV_HEADS, MOBA_GQ, n_sel, HEAD_DIM)
        return (jnp.einsum('bqgjm,bqgjmd->bqgjd', p[..., :n_sel], v_sel)
                + jnp.einsum('bqgjl,bqgld->bqgjd', p[..., n_sel:], g_own[..., 1, :]))

    o = run_blocks(blk, nb, qb, T)
    return o.reshape(B, T, MOBA_HEADS * HEAD_DIM).astype(qr.dtype)


def kernel(x_prompt, x_sample, cache_nsa_cmp_kv, cache_nsa_sel_kv, state_nsa_win_kv, state_ssm, state_conv,
           cache_moba_kv, page_table, norm_mix, norm_ffn, norm_final, w_in_even, w_out_even,
           ssm_conv_w, ssm_conv_b, ssm_dt_bias, ssm_a_log, ssm_d_skip, ssm_norm,
           nsa_cmp_pos, nsa_cmp_w1, nsa_cmp_b1, nsa_cmp_w2, nsa_cmp_b2, nsa_gate_b,
           w_in_odd, w_out_odd, peer_wq, peer_subkeys, peer_u, peer_v):

    peer_wq_bf16 = peer_wq.astype(jnp.bfloat16)
    peer_u_bf16 = peer_u.astype(jnp.bfloat16)
    peer_vt_bf16 = jnp.transpose(peer_v, (0, 2, 1)).astype(jnp.bfloat16)
    w_even_bf16 = [relayout_even_weight(w_in_even[e]) for e in range(w_in_even.shape[0])]
    w_odd_bf16 = w_in_odd.astype(jnp.bfloat16)

    def trunk(x, pos0, get_past, prompt):
        B, T, _ = x.shape
        pos_q = pos0 + jnp.arange(T, dtype=jnp.int32)
        new_cmp, new_sel, new_win, new_ssm, new_conv, new_moba = [], [], [], [], [], []
        tabs = rope_tables(pos_q)
        if T % ATTN_TILE != 0:
            tabs = tuple(jnp.tile(a, (B, 1)) for a in tabs)
        for l in range(DEPTH):
            x2 = x.reshape(B * T, D_MODEL)
            if l % 2 == 0:
                e = l // 2
                z, xbc, q, qr, kvc2, kvs2, kvw2, gd, kvc, kvs, kvw = layer_projection(
                    x2, norm_mix[l], w_even_bf16[e], tabs, EVEN_OUT_WIDTHS, _even_proj_kernel)
                kv_shape = (B, T) + KV_CACHE_TAIL
                kvc, kvs, kvw = kvc.reshape(kv_shape), kvs.reshape(kv_shape), kvw.reshape(kv_shape)
                kvw_ext = jnp.concatenate([get_past('win', e), kvw], axis=1)
                if prompt:
                    conv_past = get_past('conv', e)
                    y_ssm, ssm_new = ssd_mixer(z, xbc, gd, conv_past, get_past('ssm', e), ssm_conv_w[e], ssm_conv_b[e],
                                               ssm_dt_bias[e], ssm_a_log[e], ssm_d_skip[e], ssm_norm[e], B)
                    conv_new = jnp.concatenate([conv_past, xbc.reshape(B, T, SSM_CONV_DIM)], axis=1)[:, T:]
                    o_cmp, sel = nsa_compressed_branch(q, kvc2, B, nsa_cmp_pos[e], nsa_cmp_w1[e],
                                                       nsa_cmp_b1[e], nsa_cmp_w2[e], nsa_cmp_b2[e])
                    o_sel = flash_attention_t(qr, kvs2, 'sel', B, sel)
                    o_win = flash_attention_t(qr, kvw2, 'win', B)
                else:
                    y_ssm, conv_new, ssm_new = mamba2_ssd(
                        z.reshape(B, T, -1), xbc.reshape(B, T, -1), gd[:, 3 * NSA_HEADS:].reshape(B, T, SSM_HEADS),
                        get_past('conv', e), get_past('ssm', e), ssm_conv_w[e], ssm_conv_b[e], ssm_dt_bias[e],
                        ssm_a_log[e], ssm_d_skip[e], ssm_norm[e])
                    o_cmp, o_sel, o_win = nsa_attention(
                        q.reshape(B, T, NSA_HEADS, HEAD_DIM), qr,
                        jnp.concatenate([get_past('cmp', e), kvc], axis=1),
                        jnp.concatenate([get_past('sel', e), kvs], axis=1), kvw_ext, pos0,
                        nsa_cmp_pos[e], nsa_cmp_w1[e], nsa_cmp_b1[e], nsa_cmp_w2[e], nsa_cmp_b2[e])
                flat = lambda a: a.reshape(B * T, -1)
                x = even_output(x2, flat(y_ssm), flat(o_cmp), flat(o_sel), flat(o_win), gd, nsa_gate_b[e],
                                w_out_even[e])
                keep = min(NSA_WINDOW, kvw_ext.shape[1])
                new_cmp.append(kvc)
                new_sel.append(kvs)
                new_win.append(kvw_ext[:, kvw_ext.shape[1] - keep:])
                new_ssm.append(ssm_new)
                new_conv.append(conv_new)
            else:
                o = l // 2
                q, kv2, kv = layer_projection(x2, norm_mix[l], w_odd_bf16[o], tabs, ODD_OUT_WIDTHS, _odd_proj_kernel)
                kv = kv.reshape((B, T) + KV_CACHE_TAIL)
                if prompt:
                    y_moba = flash_attention_t(q, kv2, 'moba', B)
                else:
                    y_moba = moba_attention(q.reshape(B, T, MOBA_HEADS, HEAD_DIM),
                                            jnp.concatenate([get_past('moba', o), kv], axis=1), pos0)
                x = odd_output(x2, y_moba.reshape(B * T, -1), w_out_odd[o])
                new_moba.append(kv)
            x = peer_ffn_dense(x, norm_ffn[l], peer_wq_bf16[l], peer_subkeys[l], peer_u_bf16[l], peer_vt_bf16[l])
        y_out = final_norm(x, norm_final).reshape(B, T, D_MODEL)
        return (y_out, jnp.stack(new_cmp), jnp.stack(new_sel), jnp.stack(new_win),
                jnp.stack(new_ssm), jnp.stack(new_conv), jnp.stack(new_moba))

    bp = x_prompt.shape[0]
    dtp = x_prompt.dtype

    def prompt_past(kind, i):
        if kind == 'ssm':
            return jnp.zeros((bp, SSM_HEADS, SSM_HEAD_DIM, SSM_STATE), state_ssm.dtype)
        if kind == 'conv':
            return jnp.zeros((bp, SSM_CONV - 1, SSM_CONV_DIM), dtp)
        if kind == 'moba':
            return jnp.zeros((bp, 0, 2, MOBA_KV_HEADS, HEAD_DIM), dtp)
        return jnp.zeros((bp, 0, 2, NSA_KV_HEADS, HEAD_DIM), dtp)

    def sample_past(kind, i):
        if kind == 'ssm':
            return state_ssm[i]
        if kind == 'conv':
            return state_conv[i]
        if kind == 'win':
            return state_nsa_win_kv[i]
        if kind == 'cmp':
            return gather_pages(cache_nsa_cmp_kv[i], page_table)
        if kind == 'sel':
            return gather_pages(cache_nsa_sel_kv[i], page_table)
        return gather_pages(cache_moba_kv[i], page_table)

    past_len = page_table.shape[1] * cache_nsa_cmp_kv.shape[2]
    y_prompt, p_cmp, p_sel, p_win, p_ssm, p_conv, p_moba = trunk(x_prompt, 0, prompt_past, True)
    y_sample, s_cmp, s_sel, s_win, s_ssm, s_conv, s_moba = trunk(x_sample, past_len, sample_past, False)
    return (y_prompt, y_sample, p_cmp, p_sel, p_win, p_ssm, p_conv, p_moba,
            s_cmp, s_sel, s_win, s_ssm, s_conv, s_moba)
```
